```python
import jax, jax.numpy as jnp
from jax import lax
import numpy as np

D_MODEL = 2048
BATCH = 4
SEQ = 4096
DEPTH = 2

GRID_W = 64
CTX_LEN = 256
HEAD_DIM = 128
EPS = 1e-6
N_MOD = 9
D_FF = 5504
LRU_WIDTH = D_MODEL // 2
LRU_BLOCKS = LRU_WIDTH // HEAD_DIM
CONV_W = 4
LRU_C = 8.0
ATT_HEADS = (D_MODEL // 2) // HEAD_DIM
ATT_KV_HEADS = ATT_HEADS // 4
WINDOW = 128
BLOCK = 128
ROPE_THETA = 10000.0
NA_HEADS = D_MODEL // HEAD_DIM
NA_D = NA_HEADS * HEAD_DIM
NA_KH = 8
NA_KW = 16
NA_COL_BLOCK = 16
NA_COL_SPAN = 2 * NA_KW
N_EVEN = (DEPTH + 1) // 2
N_ODD = DEPTH // 2
AB_IN = 2 * LRU_WIDTH + (ATT_HEADS + 2 * ATT_KV_HEADS) * HEAD_DIM
ATTN_SCALE = HEAD_DIM ** -0.5

kernel_name = "hybrid_rglru_swa_natten_prefix_dit"


def rmsnorm(x, g):
    x32 = x.astype(jnp.float32)
    y = x32 * lax.rsqrt(jnp.mean(x32 * x32, axis=-1, keepdims=True) + EPS)
    return (y * g).astype(x.dtype)


def modulate(h, shift, scale):
    return h * (1.0 + scale) + shift


def half_ffn(x, g, shift, scale, gate, w_in, w_out):
    h = modulate(rmsnorm(x, g), shift, scale)
    gt, up = jnp.split(h @ w_in, 2, axis=-1)
    return x + 0.5 * gate * ((jax.nn.silu(gt) * up) @ w_out)


def heads(t, n):
    return t.reshape(*t.shape[:-1], n, HEAD_DIM)


def axial_rope(x, pos_row, pos_col):
    half = HEAD_DIM // 2
    nf = half // 2
    inv = ROPE_THETA ** (-jnp.arange(nf, dtype=jnp.float32) / nf)

    def rot(xa, pos):
        ang = pos.astype(jnp.float32)[:, None] * inv[None, :]
        cos = jnp.cos(ang)[None, :, None, :]
        sin = jnp.sin(ang)[None, :, None, :]
        x1, x2 = xa[..., :nf], xa[..., nf:]
        return jnp.concatenate([x1 * cos - x2 * sin, x2 * cos + x1 * sin], axis=-1)

    return jnp.concatenate([rot(x[..., :half], pos_row), rot(x[..., half:], pos_col)],
                           axis=-1).astype(x.dtype)


def centred_depthwise_conv(u, w, b):
    L = u.shape[1]
    left = CONV_W // 2
    up = jnp.pad(u, ((0, 0), (left, CONV_W - 1 - left), (0, 0)))
    out = b
    for k in range(CONV_W):
        out = out + up[:, k:k + L] * w[k]
    return out


def rglru_coeffs(u, w_a, b_a, w_x, b_x, lam):
    ub = u.reshape(*u.shape[:-1], LRU_BLOCKS, HEAD_DIM)
    gate_r = jnp.einsum('blnc,ncd->blnd', ub, w_a).reshape(u.shape) + b_a
    gate_i = jnp.einsum('blnc,ncd->blnd', ub, w_x).reshape(u.shape) + b_x
    r = jax.nn.sigmoid(gate_r.astype(jnp.float32))
    i = jax.nn.sigmoid(gate_i.astype(jnp.float32))
    log_a = -LRU_C * r * jax.nn.softplus(-lam.astype(jnp.float32))
    a = jnp.exp(log_a)
    b = jnp.sqrt(-jnp.expm1(2.0 * log_a)) * (i * u.astype(jnp.float32))
    return a, b


def linear_scan(a, b, h0, reverse):
    def combine(e1, e2):
        a1, b1 = e1
        a2, b2 = e2
        return a1 * a2, a2 * b1 + b2

    a_cum, h = lax.associative_scan(combine, (a, b), axis=1, reverse=reverse)
    if h0 is not None:
        h = h + a_cum * h0[:, None]
    return h


def with_sink(s, sink):
    g = ATT_HEADS // ATT_KV_HEADS
    col = jnp.broadcast_to(sink.astype(jnp.float32).reshape(ATT_KV_HEADS, g, 1, 1),
                           s.shape[:-1] + (1,))
    return jnp.concatenate([s, col], axis=-1)


def windowed_gqa(q, k, v, kc, vc, sink):
    B, L, H, hd = q.shape
    nb = L // BLOCK
    g = H // ATT_KV_HEADS
    n_ctx = kc.shape[1]
    qb = q.reshape(B, nb, BLOCK, ATT_KV_HEADS, g, hd)

    def band(t):
        tp = jnp.pad(t, ((0, 0), (BLOCK, BLOCK), (0, 0), (0, 0)))
        tp = tp.reshape(B, nb + 2, BLOCK, ATT_KV_HEADS, hd)
        return jnp.concatenate([tp[:, :-2], tp[:, 1:-1], tp[:, 2:]], axis=2)

    kw, vw = band(k), band(v)
    nk = 3 * BLOCK
    s_loc = jnp.einsum('bnqkgd,bnskd->bnkgqs', qb, kw).astype(jnp.float32) * ATTN_SCALE
    qpos = np.arange(nb)[:, None] * BLOCK + np.arange(BLOCK)[None, :]
    kpos = (np.arange(nb)[:, None] - 1) * BLOCK + np.arange(nk)[None, :]
    valid = ((np.abs(qpos[:, :, None] - kpos[:, None, :]) <= WINDOW)
             & (kpos[:, None, :] >= 0) & (kpos[:, None, :] < L))
    s_loc = jnp.where(valid[None, :, None, None], s_loc, -jnp.inf)
    s_ctx = jnp.einsum('bnqkgd,bskd->bnkgqs', qb, kc).astype(jnp.float32) * ATTN_SCALE
    p = jax.nn.softmax(with_sink(jnp.concatenate([s_loc, s_ctx], axis=-1), sink), axis=-1)
    o = (jnp.einsum('bnkgqs,bnskd->bnqkgd', p[..., :nk].astype(v.dtype), vw)
         + jnp.einsum('bnkgqs,bskd->bnqkgd', p[..., nk:nk + n_ctx].astype(vc.dtype), vc))
    return o.reshape(B, L, H * hd)


def context_gqa(qc, kc, vc, sink):
    B, C, H, hd = qc.shape
    g = H // ATT_KV_HEADS
    qg = qc.reshape(B, C, ATT_KV_HEADS, g, hd)
    s = jnp.einsum('bqkgd,bskd->bkgqs', qg, kc).astype(jnp.float32) * ATTN_SCALE
    p = jax.nn.softmax(with_sink(s, sink), axis=-1)[..., :C]
    o = jnp.einsum('bkgqs,bskd->bqkgd', p.astype(vc.dtype), vc)
    return o.reshape(B, C, H * hd)


def mixer_ab(h_lat, h_ctx, w_in, conv_w, conv_b, w_a, b_a, w_x, b_x, lam,
             q_g, k_g, sink, w_out, pos_row, pos_col, ctx_out):
    splits = np.cumsum([LRU_WIDTH, LRU_WIDTH, ATT_HEADS * HEAD_DIM, ATT_KV_HEADS * HEAD_DIM]).tolist()
    xl_lat, gl_lat, q_lat, k_lat, v_lat = jnp.split(h_lat @ w_in, splits, axis=-1)
    if ctx_out:
        xl_ctx, gl_ctx, q_ctx, k_ctx, v_ctx = jnp.split(h_ctx @ w_in, splits, axis=-1)
    else:
        w_ctx = jnp.concatenate([w_in[:, :LRU_WIDTH], w_in[:, splits[2]:]], axis=1)
        xl_ctx, k_ctx, v_ctx = jnp.split(h_ctx @ w_ctx, [LRU_WIDTH, LRU_WIDTH + ATT_KV_HEADS * HEAD_DIM], axis=-1)

    u_lat = centred_depthwise_conv(xl_lat, conv_w, conv_b)
    u_ctx = centred_depthwise_conv(xl_ctx, conv_w, conv_b)
    h_lat_sum, h_ctx_sum = None, None
    for d, rev in enumerate((False, True)):
        a_c, b_c = rglru_coeffs(u_ctx, w_a[d], b_a[d], w_x[d], b_x[d], lam[d])
        hc = linear_scan(a_c, b_c, None, rev)
        h0 = hc[:, 0] if rev else hc[:, -1]
        a_l, b_l = rglru_coeffs(u_lat, w_a[d], b_a[d], w_x[d], b_x[d], lam[d])
        hl = linear_scan(a_l, b_l, h0, rev)
        h_lat_sum = hl if h_lat_sum is None else h_lat_sum + hl
        h_ctx_sum = hc if h_ctx_sum is None else h_ctx_sum + hc
    lru_lat = (h_lat_sum * jax.nn.gelu(gl_lat.astype(jnp.float32))).astype(h_lat.dtype)

    q_lat = axial_rope(rmsnorm(heads(q_lat, ATT_HEADS), q_g), pos_row, pos_col)
    k_lat = axial_rope(rmsnorm(heads(k_lat, ATT_KV_HEADS), k_g), pos_row, pos_col)
    v_lat = heads(v_lat, ATT_KV_HEADS)
    k_ctx = rmsnorm(heads(k_ctx, ATT_KV_HEADS), k_g)
    v_ctx = heads(v_ctx, ATT_KV_HEADS)
    att_lat = windowed_gqa(q_lat, k_lat, v_lat, k_ctx, v_ctx, sink)
    y_lat = jnp.concatenate([lru_lat, att_lat], axis=-1) @ w_out
    y_ctx = None
    if ctx_out:
        lru_ctx = (h_ctx_sum * jax.nn.gelu(gl_ctx.astype(jnp.float32))).astype(h_ctx.dtype)
        att_ctx = context_gqa(rmsnorm(heads(q_ctx, ATT_HEADS), q_g), k_ctx, v_ctx, sink)
        y_ctx = jnp.concatenate([lru_ctx, att_ctx], axis=-1) @ w_out
    return y_lat, y_ctx


def mixer_na(h_lat, h_ctx, w_in, q_g, k_g, rpb, w_out, ctx_out):
    B, L, _ = h_lat.shape
    rows = L // GRID_W
    kh = min(NA_KH, rows)
    q, k, v = jnp.split(h_lat @ w_in, 3, axis=-1)
    q = rmsnorm(heads(q, NA_HEADS), q_g)
    k = rmsnorm(heads(k, NA_HEADS), k_g)
    v = heads(v, NA_HEADS)
    if ctx_out:
        qc, kc, vc = jnp.split(h_ctx @ w_in, 3, axis=-1)
    else:
        kc, vc = jnp.split(h_ctx @ w_in[:, NA_D:], 2, axis=-1)
    kc = rmsnorm(heads(kc, NA_HEADS), k_g)
    vc = heads(vc, NA_HEADS)
    n_ctx = kc.shape[1]

    qg = q.reshape(B, rows, GRID_W, NA_HEADS, HEAD_DIM)
    kg = k.reshape(B, rows, GRID_W, NA_HEADS, HEAD_DIM)
    vg = v.reshape(B, rows, GRID_W, NA_HEADS, HEAD_DIM)

    ncb = GRID_W // NA_COL_BLOCK
    m_idx = np.arange(ncb)
    col_start = np.clip(m_idx * NA_COL_BLOCK - NA_KW // 2, 0, GRID_W - NA_COL_SPAN)
    key_col = col_start[:, None] + np.arange(NA_COL_SPAN)[None, :]
    q_col = m_idx[:, None] * NA_COL_BLOCK + np.arange(NA_COL_BLOCK)[None, :]
    win_start = np.clip(q_col - NA_KW // 2, 0, GRID_W - NA_KW)
    kcol = key_col[:, None, :]
    col_valid = (kcol >= win_start[..., None]) & (kcol < win_start[..., None] + NA_KW)
    col_off = np.clip(kcol - q_col[..., None] + NA_KW - 1, 0, 2 * NA_KW - 2)
    n_loc = kh * NA_COL_SPAN

    def row_step(r):
        rs = jnp.clip(r - kh // 2, 0, rows - kh)
        qr = lax.dynamic_index_in_dim(qg, r, axis=1, keepdims=False)
        kr = lax.dynamic_slice_in_dim(kg, rs, kh, axis=1)
        vr = lax.dynamic_slice_in_dim(vg, rs, kh, axis=1)
        kb = kr[:, :, key_col]
        vb = vr[:, :, key_col]
        qb = qr.reshape(B, ncb, NA_COL_BLOCK, NA_HEADS, HEAD_DIM)
        s_loc = jnp.einsum('bmqhd,bimshd->bhmqis', qb, kb).astype(jnp.float32) * ATTN_SCALE
        ro = rs + jnp.arange(kh) - r + (NA_KH - 1)
        bias = rpb[:, ro[:, None, None, None], col_off[None]]
        bias = jnp.transpose(bias, (0, 2, 3, 1, 4)).astype(jnp.float32)
        s_loc = jnp.where(col_valid[:, :, None, :], s_loc + bias, -jnp.inf)
        s_loc = s_loc.reshape(B, NA_HEADS, ncb, NA_COL_BLOCK, n_loc)
        s_ctx = jnp.einsum('bmqhd,bshd->bhmqs', qb, kc).astype(jnp.float32) * ATTN_SCALE
        p = jax.nn.softmax(jnp.concatenate([s_loc, s_ctx], axis=-1), axis=-1)
        p_loc = p[..., :n_loc].reshape(B, NA_HEADS, ncb, NA_COL_BLOCK, kh, NA_COL_SPAN)
        o = (jnp.einsum('bhmqis,bimshd->bmqhd', p_loc.astype(v.dtype), vb)
             + jnp.einsum('bhmqs,bshd->bmqhd', p[..., n_loc:].astype(vc.dtype), vc))
        return o.reshape(B, GRID_W, NA_D)

    out = lax.map(row_step, jnp.arange(rows, dtype=jnp.int32))
    y_lat = jnp.moveaxis(out, 0, 1).reshape(B, L, NA_D) @ w_out
    y_ctx = None
    if ctx_out:
        qc = rmsnorm(heads(qc, NA_HEADS), q_g)
        s = jnp.einsum('bqhd,bshd->bhqs', qc, kc).astype(jnp.float32) * ATTN_SCALE
        p = jax.nn.softmax(s, axis=-1)
        oc = jnp.einsum('bhqs,bshd->bqhd', p.astype(vc.dtype), vc)
        y_ctx = oc.reshape(B, n_ctx, NA_D) @ w_out
    return y_lat, y_ctx


def setup_inputs(seed: int = 0) -> dict:
    key = jax.random.key(seed)
    ks = iter(jax.random.split(key, 32))

    def nrm(shape, scale):
        return jax.random.normal(next(ks), shape, jnp.float32) * scale

    D = D_MODEL
    a_init = jax.random.uniform(next(ks), (N_EVEN, 2, LRU_WIDTH), jnp.float32, minval=0.9, maxval=0.999)
    s_init = a_init ** (1.0 / LRU_C)
    lru_lambda = jnp.log(s_init) - jnp.log1p(-s_init)
    return {
        "x": nrm((BATCH, SEQ, D), 1.0),
        "c": nrm((BATCH, D), 1.0),
        "ctx": nrm((BATCH, CTX_LEN, D), 1.0),
        "c_ctx": nrm((D,), 1.0),
        "w_mod": nrm((DEPTH, D, N_MOD * D), 0.5 * D ** -0.5),
        "b_mod": nrm((DEPTH, N_MOD * D), 0.01),
        "norm_g": 1.0 + nrm((DEPTH, 3, D), 0.05),
        "ffn_w_in": nrm((DEPTH, 2, D, 2 * D_FF), D ** -0.5),
        "ffn_w_out": nrm((DEPTH, 2, D_FF, D), D_FF ** -0.5),
        "ab_w_in": nrm((N_EVEN, D, AB_IN), D ** -0.5),
        "lru_conv_w": nrm((N_EVEN, CONV_W, LRU_WIDTH), CONV_W ** -0.5),
        "lru_conv_b": nrm((N_EVEN, LRU_WIDTH), 0.01),
        "lru_w_a": nrm((N_EVEN, 2, LRU_BLOCKS, HEAD_DIM, HEAD_DIM), HEAD_DIM ** -0.5),
        "lru_b_a": nrm((N_EVEN, 2, LRU_WIDTH), 0.01),
        "lru_w_x": nrm((N_EVEN, 2, LRU_BLOCKS, HEAD_DIM, HEAD_DIM), HEAD_DIM ** -0.5),
        "lru_b_x": nrm((N_EVEN, 2, LRU_WIDTH), 0.01),
        "lru_lambda": lru_lambda,
        "attn_q_norm": 1.0 + nrm((N_EVEN, HEAD_DIM), 0.05),
        "attn_k_norm": 1.0 + nrm((N_EVEN, HEAD_DIM), 0.05),
        "attn_sink": nrm((N_EVEN, ATT_HEADS), 0.5),
        "ab_w_out": nrm((N_EVEN, LRU_WIDTH + ATT_HEADS * HEAD_DIM, D), (LRU_WIDTH + ATT_HEADS * HEAD_DIM) ** -0.5),
        "na_w_in": nrm((N_ODD, D, 3 * NA_D), D ** -0.5),
        "na_q_norm": 1.0 + nrm((N_ODD, HEAD_DIM), 0.05),
        "na_k_norm": 1.0 + nrm((N_ODD, HEAD_DIM), 0.05),
        "na_rpb": nrm((N_ODD, NA_HEADS, 2 * NA_KH - 1, 2 * NA_KW - 1), 0.1),
        "na_w_out": nrm((N_ODD, NA_D, D), NA_D ** -0.5),
    }


def reference(x, c, ctx, c_ctx, w_mod, b_mod, norm_g, ffn_w_in, ffn_w_out,
              ab_w_in, lru_conv_w, lru_conv_b, lru_w_a, lru_b_a, lru_w_x, lru_b_x, lru_lambda,
              attn_q_norm, attn_k_norm, attn_sink, ab_w_out,
              na_w_in, na_q_norm, na_k_norm, na_rpb, na_w_out):
    B, L, D = x.shape
    pos = jnp.arange(L, dtype=jnp.int32)
    pos_row, pos_col = pos // GRID_W, pos % GRID_W
    x_lat, x_ctx = x, ctx
    for l in range(DEPTH):
        ctx_out = l < DEPTH - 1
        m_lat = (jax.nn.silu(c) @ w_mod[l] + b_mod[l]).reshape(B, N_MOD, D)[:, :, None, :]
        m_ctx = (jax.nn.silu(c_ctx) @ w_mod[l] + b_mod[l]).reshape(N_MOD, D)
        x_lat = half_ffn(x_lat, norm_g[l, 0], m_lat[:, 0], m_lat[:, 1], m_lat[:, 2], ffn_w_in[l, 0], ffn_w_out[l, 0])
        x_ctx = half_ffn(x_ctx, norm_g[l, 0], m_ctx[0], m_ctx[1], m_ctx[2], ffn_w_in[l, 0], ffn_w_out[l, 0])
        h_lat = modulate(rmsnorm(x_lat, norm_g[l, 1]), m_lat[:, 3], m_lat[:, 4])
        h_ctx = modulate(rmsnorm(x_ctx, norm_g[l, 1]), m_ctx[3], m_ctx[4])
        if l % 2 == 0:
            e = l // 2
            y_lat, y_ctx = mixer_ab(h_lat, h_ctx, ab_w_in[e], lru_conv_w[e], lru_conv_b[e],
                                    lru_w_a[e], lru_b_a[e], lru_w_x[e], lru_b_x[e], lru_lambda[e],
                                    attn_q_norm[e], attn_k_norm[e], attn_sink[e], ab_w_out[e],
                                    pos_row, pos_col, ctx_out)
        else:
            o = l // 2
            y_lat, y_ctx = mixer_na(h_lat, h_ctx, na_w_in[o], na_q_norm[o], na_k_norm[o],
                                    na_rpb[o], na_w_out[o], ctx_out)
        x_lat = x_lat + m_lat[:, 5] * y_lat
        x_lat = half_ffn(x_lat, norm_g[l, 2], m_lat[:, 6], m_lat[:, 7], m_lat[:, 8], ffn_w_in[l, 1], ffn_w_out[l, 1])
        if ctx_out:
            x_ctx = x_ctx + m_ctx[5] * y_ctx
            x_ctx = half_ffn(x_ctx, norm_g[l, 2], m_ctx[6], m_ctx[7], m_ctx[8], ffn_w_in[l, 1], ffn_w_out[l, 1])
    return x_lat
```

```python
import functools

import numpy as np
import jax
import jax.numpy as jnp
from jax import lax
from jax.experimental import pallas as pl
from jax.experimental.pallas import tpu as pltpu

HEAD_DIM = 128
EPS = 1e-6
N_MOD = 9
GRID_W = 64
CONV_W = 4
LRU_C = 8.0
WINDOW = 128
ROPE_THETA = 10000.0
NA_KH = 8
NA_KW = 16
ATTN_SCALE = HEAD_DIM ** -0.5
GQA_GROUP = 4

LANES = 128
SUBLANES = 8
VMEM_LIMIT_BYTES = 56 * 1024 * 1024
MOD_ROWS = SUBLANES

NEG = -1e30
NA_ROWS_PER_BLOCK = 4

BF16 = jnp.bfloat16
F32 = jnp.float32


def _cparams(n_axes):
    return pltpu.CompilerParams(dimension_semantics=("arbitrary",) * n_axes,
                                vmem_limit_bytes=VMEM_LIMIT_BYTES)


def _sigmoid(x):
    return 1.0 / (1.0 + jnp.exp(-x))


def _norm_modulate(x, g, shift, scale):
    y = x * lax.rsqrt(jnp.mean(x * x, axis=-1, keepdims=True) + EPS)
    return (y * g) * (1.0 + scale) + shift


def _mod_kernel(c_ref, w_ref, b_ref, o_ref):
    c = c_ref[...]
    s = c * _sigmoid(c)
    o_ref[...] = jnp.dot(s.astype(BF16), w_ref[...].astype(BF16),
                         preferred_element_type=F32) + b_ref[...]


def _mod_table(cc, w_mod, b_mod, tn):
    depth, d, n = w_mod.shape
    return pl.pallas_call(
        _mod_kernel,
        out_shape=jax.ShapeDtypeStruct((depth, MOD_ROWS, n), F32),
        grid=(depth, n // tn),
        in_specs=[pl.BlockSpec((MOD_ROWS, d), lambda l, j: (0, 0)),
                  pl.BlockSpec((None, d, tn), lambda l, j: (l, 0, j)),
                  pl.BlockSpec((None, 1, tn), lambda l, j: (l, 0, j))],
        out_specs=pl.BlockSpec((None, MOD_ROWS, tn), lambda l, j: (l, 0, j)),
        compiler_params=_cparams(2),
        name="mod_table",
    )(cc, w_mod, b_mod.reshape(depth, 1, n))


def _mod_spec(d, layer, k, n_grid_axes):
    if n_grid_axes == 1:
        return pl.BlockSpec((None, MOD_ROWS, d), lambda i: (layer, 0, k))
    return pl.BlockSpec((None, MOD_ROWS, d), lambda i, j: (layer, 0, k))


def _mod_row(tile_idx, tm, rows_per_batch, n_batch):
    return jnp.minimum(tile_idx * tm // rows_per_batch, n_batch)


def _ffn_kernel(x_ref, g_ref, sh_ref, sc_ref, gt_ref, wg_ref, wu_ref, wo_ref, o_ref, h_scr,
                *, tm, rows_per_batch, n_batch):
    i, j = pl.program_id(0), pl.program_id(1)
    r = _mod_row(i, tm, rows_per_batch, n_batch)

    @pl.when(j == 0)
    def _():
        h = _norm_modulate(x_ref[...], g_ref[...], sh_ref[pl.ds(r, 1), :], sc_ref[pl.ds(r, 1), :])
        h_scr[...] = h.astype(BF16)
        o_ref[...] = jnp.zeros_like(o_ref)

    h = h_scr[...]
    a = jnp.dot(h, wg_ref[...], preferred_element_type=F32)
    u = jnp.dot(h, wu_ref[...], preferred_element_type=F32)
    act = (a * _sigmoid(a)) * u
    o_ref[...] += jnp.dot(act.astype(BF16), wo_ref[...], preferred_element_type=F32)

    @pl.when(j == pl.num_programs(1) - 1)
    def _():
        o_ref[...] = x_ref[...] + 0.5 * gt_ref[pl.ds(r, 1), :] * o_ref[...]


def _ffn(x_all, n_rows, mod, layer, k0, g, wgu, wo, *, tm, tf, rows_per_batch, n_batch):
    d = x_all.shape[1]
    nf = wo.shape[0] // tf
    kern = functools.partial(_ffn_kernel, tm=tm, rows_per_batch=rows_per_batch, n_batch=n_batch)
    return pl.pallas_call(
        kern,
        out_shape=jax.ShapeDtypeStruct((n_rows, d), F32),
        grid=(n_rows // tm, nf),
        in_specs=[pl.BlockSpec((tm, d), lambda i, j: (i, 0)),
                  pl.BlockSpec((1, d), lambda i, j: (0, 0)),
                  _mod_spec(d, layer, k0, 2), _mod_spec(d, layer, k0 + 1, 2),
                  _mod_spec(d, layer, k0 + 2, 2),
                  pl.BlockSpec((d, tf), lambda i, j: (0, j)),
                  pl.BlockSpec((d, tf), lambda i, j: (0, nf + j)),
                  pl.BlockSpec((tf, d), lambda i, j: (j, 0))],
        out_specs=pl.BlockSpec((tm, d), lambda i, j: (i, 0)),
        scratch_shapes=[pltpu.VMEM((tm, d), BF16)],
        compiler_params=_cparams(2),
        name="half_ffn",
    )(x_all, g.reshape(1, d), mod, mod, mod, wgu, wgu, wo)


def _inproj_kernel(x_ref, g_ref, sh_ref, sc_ref, w_ref, o_ref, h_scr, *, tm, rows_per_batch, n_batch):
    i, j = pl.program_id(0), pl.program_id(1)
    r = _mod_row(i, tm, rows_per_batch, n_batch)

    @pl.when(j == 0)
    def _():
        h = _norm_modulate(x_ref[...], g_ref[...], sh_ref[pl.ds(r, 1), :], sc_ref[pl.ds(r, 1), :])
        h_scr[...] = h.astype(BF16)

    o_ref[...] = jnp.dot(h_scr[...], w_ref[...], preferred_element_type=F32)


def _inproj(x_all, mod, layer, g, w, *, tm, tn, rows_per_batch, n_batch):
    ta, d = x_all.shape
    n = w.shape[1]
    while n % tn:
        tn -= LANES
    kern = functools.partial(_inproj_kernel, tm=tm, rows_per_batch=rows_per_batch, n_batch=n_batch)
    return pl.pallas_call(
        kern,
        out_shape=jax.ShapeDtypeStruct((ta, n), F32),
        grid=(ta // tm, n // tn),
        in_specs=[pl.BlockSpec((tm, d), lambda i, j: (i, 0)),
                  pl.BlockSpec((1, d), lambda i, j: (0, 0)),
                  _mod_spec(d, layer, 3, 2), _mod_spec(d, layer, 4, 2),
                  pl.BlockSpec((d, tn), lambda i, j: (0, j))],
        out_specs=pl.BlockSpec((tm, tn), lambda i, j: (i, j)),
        scratch_shapes=[pltpu.VMEM((tm, d), BF16)],
        compiler_params=_cparams(2),
        name="mixer_inproj",
    )(x_all, g.reshape(1, d), mod, mod, w)


def _outproj_kernel(a1_ref, a2_ref, w1_ref, w2_ref, x_ref, gt_ref, o_ref, *, tm, rows_per_batch, n_batch):
    r = _mod_row(pl.program_id(0), tm, rows_per_batch, n_batch)
    y = (jnp.dot(a1_ref[...], w1_ref[...], preferred_element_type=F32)
         + jnp.dot(a2_ref[...], w2_ref[...], preferred_element_type=F32))
    o_ref[...] = x_ref[...] + gt_ref[pl.ds(r, 1), :] * y


def _outproj(a1, a2, a2_col_block, w, x_all, n_rows, mod, layer, *, tm, rows_per_batch, n_batch):
    d = x_all.shape[1]
    kh = w.shape[0] // 2
    kern = functools.partial(_outproj_kernel, tm=tm, rows_per_batch=rows_per_batch, n_batch=n_batch)
    return pl.pallas_call(
        kern,
        out_shape=jax.ShapeDtypeStruct((n_rows, d), F32),
        grid=(n_rows // tm,),
        in_specs=[pl.BlockSpec((tm, kh), lambda i: (i, 0)),
                  pl.BlockSpec((tm, kh), lambda i: (i, a2_col_block)),
                  pl.BlockSpec((kh, d), lambda i: (0, 0)),
                  pl.BlockSpec((kh, d), lambda i: (1, 0)),
                  pl.BlockSpec((tm, d), lambda i: (i, 0)),
                  _mod_spec(d, layer, 5, 1)],
        out_specs=pl.BlockSpec((tm, d), lambda i: (i, 0)),
        compiler_params=_cparams(1),
        name="mixer_outproj",
    )(a1, a2, w, w, x_all, mod)


def _swap_halves_32(y):
    lane = lax.broadcasted_iota(jnp.int32, y.shape, 1)
    return jnp.where((lane & 32) == 0, pltpu.roll(y, LANES - 32, 1), pltpu.roll(y, 32, 1))


def _prep_kernel(*refs, segs, n_lat_tiles):
    n_seg = len(segs)
    seg_refs = refs[:n_seg]
    gains_ref, cos_ref, sin_ref, o_ref = refs[n_seg:]
    is_lat = pl.program_id(0) < n_lat_tiles
    col = 0
    for ref, (n_heads, gain_row, rope) in zip(seg_refs, segs):
        for h in range(n_heads):
            xh = ref[:, h * HEAD_DIM:(h + 1) * HEAD_DIM]
            if gain_row is not None:
                y = xh * lax.rsqrt(jnp.mean(xh * xh, axis=-1, keepdims=True) + EPS)
                y = y * gains_ref[gain_row:gain_row + 1, :]
                if rope:
                    y = jnp.where(is_lat, y * cos_ref[...] + _swap_halves_32(y) * sin_ref[...], y)
            else:
                y = xh
            o_ref[:, col:col + HEAD_DIM] = y.astype(BF16)
            col += HEAD_DIM


def _prep(p, seg_cols, segs, gains, cos_t, sin_t, *, tm, n_lat_rows):
    ta = p.shape[0]
    total = sum(s[0] for s in segs) * HEAD_DIM
    n_lat_tiles = n_lat_rows // tm
    n_pos_tiles = cos_t.shape[0] // tm
    in_specs = []
    for c0, (n_heads, _, _) in zip(seg_cols, segs):
        width = n_heads * HEAD_DIM
        in_specs.append(pl.BlockSpec((tm, width), functools.partial(lambda i, cb: (i, cb), cb=c0 // width)))
    in_specs += [pl.BlockSpec(gains.shape, lambda i: (0, 0)),
                 pl.BlockSpec((tm, HEAD_DIM), lambda i: (i % n_pos_tiles, 0)),
                 pl.BlockSpec((tm, HEAD_DIM), lambda i: (i % n_pos_tiles, 0))]
    kern = functools.partial(_prep_kernel, segs=tuple(segs), n_lat_tiles=n_lat_tiles)
    return pl.pallas_call(
        kern,
        out_shape=jax.ShapeDtypeStruct((ta, total), BF16),
        grid=(ta // tm,),
        in_specs=in_specs,
        out_specs=pl.BlockSpec((tm, total), lambda i: (i, 0)),
        compiler_params=_cparams(1),
        name="head_prep",
    )(*([p] * len(segs)), gains, cos_t, sin_t)


def _rope_tables(seq_len):
    half = HEAD_DIM // 2
    nf = half // 2
    inv = ROPE_THETA ** (-jnp.arange(nf, dtype=F32) / nf)
    pos = jnp.arange(seq_len, dtype=jnp.int32)
    ang_r = (pos // GRID_W).astype(F32)[:, None] * inv[None, :]
    ang_c = (pos % GRID_W).astype(F32)[:, None] * inv[None, :]
    cr, sr, cc, sc = jnp.cos(ang_r), jnp.sin(ang_r), jnp.cos(ang_c), jnp.sin(ang_c)
    return (jnp.concatenate([cr, cr, cc, cc], axis=-1),
            jnp.concatenate([-sr, sr, -sc, sc], axis=-1))


def _gelu_tanh(x):
    return x * (0.5 * (1.0 + jnp.tanh(np.float32(np.sqrt(2.0 / np.pi)) * (x + 0.044715 * (x * x * x)))))


def _lru_kernel(*refs, tt, reverse, final, n_lat_chunks):
    if final:
        (xp_ref, xc_ref, xn_ref, cw_ref, cb_ref, wa_ref, ba_ref, wx_ref, bx_ref, lam_ref,
         hf_ref, gl_ref, o_ref, xs_scr, a_scr, b_scr, carry_scr) = refs
    else:
        (xp_ref, xc_ref, xn_ref, cw_ref, cb_ref, wa_ref, ba_ref, wx_ref, bx_ref, lam_ref,
         o_ref, xs_scr, a_scr, b_scr, carry_scr) = refs
    s = pl.program_id(1)
    is_ctx = s == 0
    j = (n_lat_chunks - s) if reverse else (s - 1)
    has_prev = jnp.logical_and(jnp.logical_not(is_ctx), j > 0)
    has_next = jnp.logical_and(jnp.logical_not(is_ctx), j < n_lat_chunks - 1)

    @pl.when(is_ctx)
    def _():
        carry_scr[...] = jnp.zeros_like(carry_scr)

    halo = SUBLANES
    xs_scr[0:halo, :] = jnp.where(has_prev, xp_ref[tt - halo:tt, :], 0.0)
    xs_scr[halo:halo + tt, :] = xc_ref[...]
    xs_scr[halo + tt:2 * halo + tt, :] = jnp.where(has_next, xn_ref[0:halo, :], 0.0)

    width = xc_ref.shape[1]
    left = CONV_W // 2
    z = -lam_ref[...]
    softplus = jnp.maximum(z, 0.0) + jnp.log1p(jnp.exp(-jnp.abs(z)))
    for n in range(width // HEAD_DIM):
        cs = slice(n * HEAD_DIM, (n + 1) * HEAD_DIM)
        u = cb_ref[:, cs]
        for k in range(CONV_W):
            u = u + xs_scr[halo - left + k:halo - left + k + tt, cs] * cw_ref[k:k + 1, cs]
        ub = u.astype(BF16)
        gate_r = jnp.dot(ub, wa_ref[n], preferred_element_type=F32) + ba_ref[:, cs]
        gate_i = jnp.dot(ub, wx_ref[n], preferred_element_type=F32) + bx_ref[:, cs]
        log_a = (-LRU_C) * _sigmoid(gate_r) * softplus[:, cs]
        a = jnp.exp(log_a)
        a_scr[:, cs] = a
        b_scr[:, cs] = jnp.sqrt(-jnp.tanh(log_a) * (1.0 + a * a)) * (_sigmoid(gate_i) * u)

    n_groups = tt // SUBLANES
    row = lax.broadcasted_iota(jnp.int32, (SUBLANES, width), 0)

    def group(gi, carry):
        g = (n_groups - 1 - gi) if reverse else gi
        r0 = pl.multiple_of(g * SUBLANES, SUBLANES)
        a = a_scr[pl.ds(r0, SUBLANES), :]
        b = b_scr[pl.ds(r0, SUBLANES), :]
        for k in (1, 2, 4):
            if reverse:
                keep = row < SUBLANES - k
                shift = SUBLANES - k
            else:
                keep = row >= k
                shift = k
            a_sh = jnp.where(keep, pltpu.roll(a, shift, 0), 1.0)
            b_sh = jnp.where(keep, pltpu.roll(b, shift, 0), 0.0)
            b = a * b_sh + b
            a = a * a_sh
        h = b + a * carry
        b_scr[pl.ds(r0, SUBLANES), :] = h
        last = h[0:1, :] if reverse else h[SUBLANES - 1:SUBLANES, :]
        return jnp.broadcast_to(last, (SUBLANES, width))

    carry_scr[...] = lax.fori_loop(0, n_groups, group, carry_scr[...])

    if final:
        o_ref[...] = ((hf_ref[...] + b_scr[...]) * _gelu_tanh(gl_ref[...])).astype(o_ref.dtype)
    else:
        o_ref[...] = b_scr[...]


def _lru_pass(p, conv_w, conv_b, w_a, b_a, w_x, b_x, lam, hf, *, reverse, n_batch, seq_len, ctx_len, tt):
    ta = p.shape[0]
    width = conv_w.shape[1]
    assert ctx_len == tt and seq_len % tt == 0
    n_lat_chunks = seq_len // tt
    ctx_block0 = n_batch * n_lat_chunks
    final = hf is not None

    def chunk_index(b, s, off):
        j = (n_lat_chunks - s) if reverse else (s - 1)
        j = jnp.clip(j + off, 0, n_lat_chunks - 1)
        return jnp.where(s == 0, ctx_block0 + b, b * n_lat_chunks + j)

    def xspec(off):
        return pl.BlockSpec((tt, width), lambda b, s: (chunk_index(b, s, off), 0))

    row_spec = pl.BlockSpec((1, width), lambda b, s: (0, 0))
    w_spec = pl.BlockSpec(w_a.shape, lambda b, s: (0, 0, 0))
    in_specs = [xspec(-1), xspec(0), xspec(1),
                pl.BlockSpec((CONV_W, width), lambda b, s: (0, 0)), row_spec,
                w_spec, row_spec, w_spec, row_spec, row_spec]
    args = [p, p, p, conv_w, conv_b.reshape(1, width), w_a, b_a.reshape(1, width),
            w_x, b_x.reshape(1, width), lam.reshape(1, width)]
    if final:
        in_specs += [xspec(0), pl.BlockSpec((tt, width), lambda b, s: (chunk_index(b, s, 0), 1))]
        args += [hf, p]
    kern = functools.partial(_lru_kernel, tt=tt, reverse=reverse, final=final, n_lat_chunks=n_lat_chunks)
    return pl.pallas_call(
        kern,
        out_shape=jax.ShapeDtypeStruct((ta, width), BF16 if final else F32),
        grid=(n_batch, n_lat_chunks + 1),
        in_specs=in_specs,
        out_specs=xspec(0),
        scratch_shapes=[pltpu.VMEM((tt + 2 * SUBLANES, width), F32), pltpu.VMEM((tt, width), F32),
                        pltpu.VMEM((tt, width), F32), pltpu.VMEM((SUBLANES, width), F32)],
        compiler_params=_cparams(2),
        name="rglru_rev" if reverse else "rglru_fwd",
    )(*args)


def _softmax_rows(s, extra_logit=None):
    m = jnp.max(s, axis=-1, keepdims=True)
    if extra_logit is not None:
        m = jnp.maximum(m, extra_logit)
    e = jnp.exp(s - m)
    denom = jnp.sum(e, axis=-1, keepdims=True)
    if extra_logit is not None:
        denom = denom + jnp.exp(extra_logit - m)
    return e / denom


def _swa_kernel(sink_ref, q_ref, kp_ref, kc_ref, kn_ref, kx_ref, vp_ref, vc_ref, vn_ref, vx_ref, o_ref,
                *, n_lat_blocks):
    n, kh = pl.program_id(1), pl.program_id(2)
    blk = q_ref.shape[0]
    is_lat = n < n_lat_blocks
    q = jnp.concatenate([q_ref[:, h * HEAD_DIM:(h + 1) * HEAD_DIM] for h in range(GQA_GROUP)], axis=0)
    k = jnp.concatenate([kp_ref[...], kc_ref[...], kn_ref[...], kx_ref[...]], axis=0)
    v = jnp.concatenate([vp_ref[...], vc_ref[...], vn_ref[...], vx_ref[...]], axis=0)
    s = lax.dot_general(q, k, (((1,), (1,)), ((), ())), preferred_element_type=F32) * ATTN_SCALE

    n_local = 3 * blk
    qi = lax.broadcasted_iota(jnp.int32, s.shape, 0) & (blk - 1)
    col = lax.broadcasted_iota(jnp.int32, s.shape, 1)
    lo = jnp.where(is_lat, jnp.where(n > 0, 0, blk), 0)
    hi = jnp.where(is_lat, jnp.where(n < n_lat_blocks - 1, n_local, 2 * blk), 0)
    rel = col - blk - qi
    local_ok = (col >= lo) & (col < hi) & (rel >= -WINDOW) & (rel <= WINDOW)
    s = jnp.where(local_ok | (col >= n_local), s, NEG)

    head = lax.broadcasted_iota(jnp.int32, (GQA_GROUP * blk, 1), 0) // blk
    sink = jnp.zeros((GQA_GROUP * blk, 1), F32)
    for h in range(GQA_GROUP):
        sink = jnp.where(head == h, sink_ref[kh * GQA_GROUP + h], sink)
    p = _softmax_rows(s, sink)
    o = jnp.dot(p.astype(BF16), v, preferred_element_type=F32)
    for h in range(GQA_GROUP):
        o_ref[:, h * HEAD_DIM:(h + 1) * HEAD_DIM] = o[h * blk:(h + 1) * blk, :].astype(o_ref.dtype)


def _swa(qkv, sink, *, n_batch, seq_len, ctx_len, n_kv_heads, blk):
    ta = qkv.shape[0]
    assert blk == WINDOW and seq_len % blk == 0 and ctx_len % blk == 0
    n_heads = n_kv_heads * GQA_GROUP
    n_lat_blocks = seq_len // blk
    n_ctx_blocks = ctx_len // blk
    ctx_q0 = n_batch * n_lat_blocks
    ctx_kv0 = n_batch * seq_len // ctx_len
    k_col0, v_col0 = n_heads, n_heads + n_kv_heads

    def q_index(b, n, kh):
        return (jnp.where(n < n_lat_blocks, b * n_lat_blocks + n, ctx_q0 + b * n_ctx_blocks + n - n_lat_blocks), kh)

    def kv_spec(off, col0):
        return pl.BlockSpec((blk, HEAD_DIM),
                            lambda b, n, kh: (b * n_lat_blocks + jnp.clip(n + off, 0, n_lat_blocks - 1), col0 + kh))

    def ctx_spec(col0):
        return pl.BlockSpec((ctx_len, HEAD_DIM), lambda b, n, kh: (ctx_kv0 + b, col0 + kh))

    kern = functools.partial(_swa_kernel, n_lat_blocks=n_lat_blocks)
    return pl.pallas_call(
        kern,
        out_shape=jax.ShapeDtypeStruct((ta, n_heads * HEAD_DIM), BF16),
        grid=(n_batch, n_lat_blocks + n_ctx_blocks, n_kv_heads),
        in_specs=[pl.BlockSpec(memory_space=pltpu.SMEM),
                  pl.BlockSpec((blk, GQA_GROUP * HEAD_DIM), q_index),
                  kv_spec(-1, k_col0), kv_spec(0, k_col0), kv_spec(1, k_col0), ctx_spec(k_col0),
                  kv_spec(-1, v_col0), kv_spec(0, v_col0), kv_spec(1, v_col0), ctx_spec(v_col0)],
        out_specs=pl.BlockSpec((blk, GQA_GROUP * HEAD_DIM), q_index),
        compiler_params=_cparams(3),
        name="windowed_gqa",
    )(sink, *([qkv] * 9))


def _na_kernel(q_ref, kp_ref, kc_ref, kn_ref, kx_ref, vp_ref, vc_ref, vn_ref, vx_ref, bias_ref, o_ref,
               *, n_grid_rows):
    jb = pl.program_id(2)
    tq = q_ref.shape[0]
    k = jnp.concatenate([kp_ref[...], kc_ref[...], kn_ref[...], kx_ref[...]], axis=0)
    v = jnp.concatenate([vp_ref[...], vc_ref[...], vn_ref[...], vx_ref[...]], axis=0)
    s = lax.dot_general(q_ref[...], k, (((1,), (1,)), ((), ())), preferred_element_type=F32) * ATTN_SCALE
    s = s + bias_ref[...]
    r = NA_ROWS_PER_BLOCK * jb + lax.broadcasted_iota(jnp.int32, s.shape, 0) // GRID_W
    col = lax.broadcasted_iota(jnp.int32, s.shape, 1)
    kr = NA_ROWS_PER_BLOCK * (jb - 1) + col // GRID_W
    kh = min(NA_KH, n_grid_rows)
    rs = jnp.clip(r - kh // 2, 0, n_grid_rows - kh)
    row_ok = (kr >= rs) & (kr < rs + kh)
    s = jnp.where(row_ok | (col >= 3 * tq), s, NEG)
    p = _softmax_rows(s)
    o_ref[...] = jnp.dot(p.astype(BF16), v, preferred_element_type=F32).astype(o_ref.dtype)


def _na_bias_table(rpb, ctx_len):
    r_in = NA_ROWS_PER_BLOCK
    ri = np.arange(r_in)[:, None, None, None]
    qc = np.arange(GRID_W)[None, :, None, None]
    krj = np.arange(3 * r_in)[None, None, :, None]
    kc = np.arange(GRID_W)[None, None, None, :]
    shape = (r_in, GRID_W, 3 * r_in, GRID_W)
    ro = np.broadcast_to(np.clip(krj - ri + (NA_KH - 1) - r_in, 0, 2 * NA_KH - 2), shape)
    win_start = np.clip(qc - NA_KW // 2, 0, GRID_W - NA_KW)
    col_valid = np.broadcast_to((kc >= win_start) & (kc < win_start + NA_KW), shape)
    co = np.broadcast_to(np.clip(kc - qc + NA_KW - 1, 0, 2 * NA_KW - 2), shape)
    tab = jnp.where(col_valid[None], rpb[:, ro, co].astype(F32), NEG)
    tab = tab.reshape(rpb.shape[0], r_in * GRID_W, 3 * r_in * GRID_W)
    return jnp.pad(tab, ((0, 0), (0, 0), (0, ctx_len)))


def _na(qkv, bias, *, n_batch, seq_len, ctx_len, n_heads):
    tq = NA_ROWS_PER_BLOCK * GRID_W
    assert ctx_len == tq and seq_len % tq == 0 and NA_KH == 2 * NA_ROWS_PER_BLOCK
    n_blocks = seq_len // tq
    ctx0 = n_batch * n_blocks

    def kv_spec(off, col0):
        return pl.BlockSpec((tq, HEAD_DIM),
                            lambda b, h, jb: (b * n_blocks + jnp.clip(jb + off, 0, n_blocks - 1), col0 + h))

    def ctx_spec(col0):
        return pl.BlockSpec((ctx_len, HEAD_DIM), lambda b, h, jb: (ctx0 + b, col0 + h))

    kern = functools.partial(_na_kernel, n_grid_rows=seq_len // GRID_W)
    return pl.pallas_call(
        kern,
        out_shape=jax.ShapeDtypeStruct((n_batch * seq_len, n_heads * HEAD_DIM), BF16),
        grid=(n_batch, n_heads, n_blocks),
        in_specs=[kv_spec(0, 0),
                  kv_spec(-1, n_heads), kv_spec(0, n_heads), kv_spec(1, n_heads), ctx_spec(n_heads),
                  kv_spec(-1, 2 * n_heads), kv_spec(0, 2 * n_heads), kv_spec(1, 2 * n_heads),
                  ctx_spec(2 * n_heads),
                  pl.BlockSpec((None,) + bias.shape[1:], lambda b, h, jb: (h, 0, 0))],
        out_specs=pl.BlockSpec((tq, HEAD_DIM), lambda b, h, jb: (b * n_blocks + jb, h)),
        compiler_params=_cparams(3),
        name="neighbourhood_attn",
    )(*([qkv] * 9), bias)


def _round_up(n, m):
    return (n + m - 1) // m * m


def _tiles(d_ff):
    tf = 512
    return dict(tm=512, tf=tf, f_pad=_round_up(d_ff, tf), tn=512, tn_mod=1024)


def _ffn_weights(w_in, w_out, f_pad):
    f = w_out.shape[0]
    pad = f_pad - f
    wg = jnp.pad(w_in[:, :f].astype(BF16), ((0, 0), (0, pad)))
    wu = jnp.pad(w_in[:, f:].astype(BF16), ((0, 0), (0, pad)))
    return jnp.concatenate([wg, wu], axis=1), jnp.pad(w_out.astype(BF16), ((0, pad), (0, 0)))


def kernel(x, c, ctx, c_ctx, w_mod, b_mod, norm_g, ffn_w_in, ffn_w_out, ab_w_in, lru_conv_w, lru_conv_b,
           lru_w_a, lru_b_a, lru_w_x, lru_b_x, lru_lambda, attn_q_norm, attn_k_norm, attn_sink, ab_w_out,
           na_w_in, na_q_norm, na_k_norm, na_rpb, na_w_out):
    n_batch, seq_len, d = x.shape
    ctx_len = ctx.shape[1]
    depth = w_mod.shape[0]
    d_ff = ffn_w_out.shape[2]
    lru_w = lru_conv_w.shape[2]
    n_kv = attn_sink.shape[1] // GQA_GROUP
    n_att = n_kv * GQA_GROUP
    na_heads = na_rpb.shape[1]
    assert n_batch < MOD_ROWS and depth == 2
    t = _tiles(d_ff)
    tl, tc = n_batch * seq_len, n_batch * ctx_len
    geom = dict(rows_per_batch=seq_len, n_batch=n_batch)

    x_all = jnp.concatenate([x.reshape(tl, d), ctx.reshape(tc, d)], axis=0)
    cc = jnp.zeros((MOD_ROWS, d), F32).at[:n_batch].set(c).at[n_batch].set(c_ctx)
    mod = _mod_table(cc, w_mod, b_mod, t["tn_mod"])

    def ffn(x_in, n_rows, layer, which):
        wgu, wo = _ffn_weights(ffn_w_in[layer, which], ffn_w_out[layer, which], t["f_pad"])
        return _ffn(x_in, n_rows, mod, layer, 6 * which, norm_g[layer, 2 * which], wgu, wo,
                    tm=t["tm"], tf=t["tf"], **geom)

    cos_t, sin_t = _rope_tables(seq_len)

    x_all = ffn(x_all, tl + tc, 0, 0)
    p = _inproj(x_all, mod, 0, norm_g[0, 1], ab_w_in[0].astype(BF16), tm=t["tm"], tn=t["tn"], **geom)
    lru_args = lambda dr: (lru_conv_w[0], lru_conv_b[0], lru_w_a[0, dr].astype(BF16), lru_b_a[0, dr],
                           lru_w_x[0, dr].astype(BF16), lru_b_x[0, dr], lru_lambda[0, dr])
    scan_geom = dict(n_batch=n_batch, seq_len=seq_len, ctx_len=ctx_len, tt=ctx_len)
    h_fwd = _lru_pass(p, *lru_args(0), None, reverse=False, **scan_geom)
    lru = _lru_pass(p, *lru_args(1), h_fwd, reverse=True, **scan_geom)
    gains = jnp.stack([attn_q_norm[0], attn_k_norm[0]])
    q_col, k_col = 2 * lru_w, 2 * lru_w + n_att * HEAD_DIM
    qkv = _prep(p, (q_col, k_col, k_col + n_kv * HEAD_DIM),
                ((n_att, 0, True), (n_kv, 1, True), (n_kv, None, False)),
                gains, cos_t, sin_t, tm=ctx_len, n_lat_rows=tl)
    att = _swa(qkv, attn_sink[0], n_batch=n_batch, seq_len=seq_len, ctx_len=ctx_len, n_kv_heads=n_kv,
               blk=WINDOW)
    x_all = _outproj(lru, att, 0, ab_w_out[0].astype(BF16), x_all, tl + tc, mod, 0, tm=t["tm"], **geom)
    x_all = ffn(x_all, tl + tc, 0, 1)

    x_all = ffn(x_all, tl + tc, 1, 0)
    p = _inproj(x_all, mod, 1, norm_g[1, 1], na_w_in[0].astype(BF16), tm=t["tm"], tn=t["tn"], **geom)
    gains = jnp.stack([na_q_norm[0], na_k_norm[0]])
    qkv = _prep(p, (0, na_heads * HEAD_DIM, 2 * na_heads * HEAD_DIM),
                ((na_heads, 0, False), (na_heads, 1, False), (na_heads, None, False)),
                gains, cos_t, sin_t, tm=ctx_len, n_lat_rows=tl)
    o = _na(qkv, _na_bias_table(na_rpb[0], ctx_len), n_batch=n_batch, seq_len=seq_len, ctx_len=ctx_len,
            n_heads=na_heads)
    x_lat = _outproj(o, o, 1, na_w_out[0].astype(BF16), x_all, tl, mod, 1, tm=t["tm"], **geom)
    x_lat = ffn(x_lat, tl, 1, 1)
    return x_lat.reshape(n_batch, seq_len, d)
```

```python
import functools

import numpy as np
import jax
import jax.numpy as jnp
from jax import lax
from jax.experimental import pallas as pl
from jax.experimental.pallas import tpu as pltpu

HEAD_DIM = 128
EPS = 1e-6
N_MOD = 9
GRID_W = 64
CONV_W = 4
LRU_C = 8.0
WINDOW = 128
ROPE_THETA = 10000.0
NA_KH = 8
NA_KW = 16
ATTN_SCALE = HEAD_DIM ** -0.5
GQA_GROUP = 4

LANES = 128
SUBLANES = 8
VMEM_LIMIT_BYTES = 56 * 1024 * 1024
MOD_ROWS = SUBLANES

NEG = -1e30
NA_ROWS_PER_BLOCK = 4

BF16 = jnp.bfloat16
F32 = jnp.float32


def _cparams(n_axes):
    return pltpu.CompilerParams(dimension_semantics=("arbitrary",) * n_axes,
                                vmem_limit_bytes=VMEM_LIMIT_BYTES)


def _sigmoid(x):
    return 1.0 / (1.0 + jnp.exp(-x))


NORM_CHUNK_ROWS = 32


def _norm_modulate_into(h_scr, x_ref, g, shift, scale):
    gain = g * (1.0 + scale)

    def chunk(c, carry):
        r0 = pl.multiple_of(c * NORM_CHUNK_ROWS, NORM_CHUNK_ROWS)
        x = x_ref[pl.ds(r0, NORM_CHUNK_ROWS), :]
        inv = lax.rsqrt(jnp.mean(x * x, axis=-1, keepdims=True) + EPS)
        h_scr[pl.ds(r0, NORM_CHUNK_ROWS), :] = ((x * inv) * gain + shift).astype(h_scr.dtype)
        return carry

    lax.fori_loop(0, x_ref.shape[0] // NORM_CHUNK_ROWS, chunk, 0, unroll=2)


def _mod_kernel(c_ref, w_ref, b_ref, o_ref):
    c = c_ref[...]
    s = c * _sigmoid(c)
    o_ref[...] = jnp.dot(s.astype(BF16), w_ref[...].astype(BF16),
                         preferred_element_type=F32) + b_ref[...]


def _mod_table(cc, w_mod, b_mod, tn):
    depth, d, n = w_mod.shape
    return pl.pallas_call(
        _mod_kernel,
        out_shape=jax.ShapeDtypeStruct((depth, MOD_ROWS, n), F32),
        grid=(depth, n // tn),
        in_specs=[pl.BlockSpec((MOD_ROWS, d), lambda l, j: (0, 0)),
                  pl.BlockSpec((None, d, tn), lambda l, j: (l, 0, j)),
                  pl.BlockSpec((None, 1, tn), lambda l, j: (l, 0, j))],
        out_specs=pl.BlockSpec((None, MOD_ROWS, tn), lambda l, j: (l, 0, j)),
        compiler_params=_cparams(2),
        name="mod_table",
    )(cc, w_mod, b_mod.reshape(depth, 1, n))


def _mod_spec(d, layer, k, n_grid_axes):
    if n_grid_axes == 1:
        return pl.BlockSpec((None, MOD_ROWS, d), lambda i: (layer, 0, k))
    return pl.BlockSpec((None, MOD_ROWS, d), lambda i, j: (layer, 0, k))


def _mod_row(tile_idx, tm, rows_per_batch, n_batch):
    return jnp.minimum(tile_idx * tm // rows_per_batch, n_batch)


def _ffn_kernel(x_ref, g_ref, sh_ref, sc_ref, gt_ref, wg_ref, wu_ref, wo_ref, o_ref, h_scr,
                *, tm, rows_per_batch, n_batch):
    i, j = pl.program_id(0), pl.program_id(1)
    r = _mod_row(i, tm, rows_per_batch, n_batch)

    @pl.when(j == 0)
    def _():
        _norm_modulate_into(h_scr, x_ref, g_ref[...], sh_ref[pl.ds(r, 1), :], sc_ref[pl.ds(r, 1), :])
        o_ref[...] = jnp.zeros_like(o_ref)

    h = h_scr[...]
    a = jnp.dot(h, wg_ref[...], preferred_element_type=F32)
    u = jnp.dot(h, wu_ref[...], preferred_element_type=F32)
    act = (a * _sigmoid(a)) * u
    o_ref[...] += jnp.dot(act.astype(BF16), wo_ref[...], preferred_element_type=F32)

    @pl.when(j == pl.num_programs(1) - 1)
    def _():
        o_ref[...] = x_ref[...] + 0.5 * gt_ref[pl.ds(r, 1), :] * o_ref[...]


def _ffn(x_all, n_rows, mod, layer, k0, g, wgu, wo, *, tm, tf, rows_per_batch, n_batch):
    d = x_all.shape[1]
    nf = wo.shape[0] // tf
    kern = functools.partial(_ffn_kernel, tm=tm, rows_per_batch=rows_per_batch, n_batch=n_batch)
    return pl.pallas_call(
        kern,
        out_shape=jax.ShapeDtypeStruct((n_rows, d), F32),
        grid=(n_rows // tm, nf),
        in_specs=[pl.BlockSpec((tm, d), lambda i, j: (i, 0), pipeline_mode=pl.Buffered(1)),
                  pl.BlockSpec((1, d), lambda i, j: (0, 0)),
                  _mod_spec(d, layer, k0, 2), _mod_spec(d, layer, k0 + 1, 2),
                  _mod_spec(d, layer, k0 + 2, 2),
                  pl.BlockSpec((d, tf), lambda i, j: (0, j)),
                  pl.BlockSpec((d, tf), lambda i, j: (0, nf + j)),
                  pl.BlockSpec((tf, d), lambda i, j: (j, 0))],
        out_specs=pl.BlockSpec((tm, d), lambda i, j: (i, 0)),
        scratch_shapes=[pltpu.VMEM((tm, d), BF16)],
        compiler_params=_cparams(2),
        name="half_ffn",
    )(x_all, g.reshape(1, d), mod, mod, mod, wgu, wgu, wo)


def _inproj_kernel(x_ref, g_ref, sh_ref, sc_ref, w_ref, o_ref, h_scr, *, tm, rows_per_batch, n_batch):
    i, j = pl.program_id(0), pl.program_id(1)
    r = _mod_row(i, tm, rows_per_batch, n_batch)

    @pl.when(j == 0)
    def _():
        _norm_modulate_into(h_scr, x_ref, g_ref[...], sh_ref[pl.ds(r, 1), :], sc_ref[pl.ds(r, 1), :])

    o_ref[...] = jnp.dot(h_scr[...], w_ref[...], preferred_element_type=F32)


def _inproj(x_all, mod, layer, g, w, *, tm, tn, rows_per_batch, n_batch):
    ta, d = x_all.shape
    n = w.shape[1]
    while n % tn:
        tn -= LANES
    kern = functools.partial(_inproj_kernel, tm=tm, rows_per_batch=rows_per_batch, n_batch=n_batch)
    return pl.pallas_call(
        kern,
        out_shape=jax.ShapeDtypeStruct((ta, n), F32),
        grid=(ta // tm, n // tn),
        in_specs=[pl.BlockSpec((tm, d), lambda i, j: (i, 0), pipeline_mode=pl.Buffered(1)),
                  pl.BlockSpec((1, d), lambda i, j: (0, 0)),
                  _mod_spec(d, layer, 3, 2), _mod_spec(d, layer, 4, 2),
                  pl.BlockSpec((d, tn), lambda i, j: (0, j))],
        out_specs=pl.BlockSpec((tm, tn), lambda i, j: (i, j)),
        scratch_shapes=[pltpu.VMEM((tm, d), BF16)],
        compiler_params=_cparams(2),
        name="mixer_inproj",
    )(x_all, g.reshape(1, d), mod, mod, w)


def _outproj_kernel(a1_ref, a2_ref, w1_ref, w2_ref, x_ref, gt_ref, o_ref, *, tm, rows_per_batch, n_batch):
    r = _mod_row(pl.program_id(0), tm, rows_per_batch, n_batch)
    y = (jnp.dot(a1_ref[...], w1_ref[...], preferred_element_type=F32)
         + jnp.dot(a2_ref[...], w2_ref[...], preferred_element_type=F32))
    o_ref[...] = x_ref[...] + gt_ref[pl.ds(r, 1), :] * y


def _outproj(a1, a2, a2_col_block, w, x_all, n_rows, mod, layer, *, tm, rows_per_batch, n_batch):
    d = x_all.shape[1]
    kh = w.shape[0] // 2
    kern = functools.partial(_outproj_kernel, tm=tm, rows_per_batch=rows_per_batch, n_batch=n_batch)
    return pl.pallas_call(
        kern,
        out_shape=jax.ShapeDtypeStruct((n_rows, d), F32),
        grid=(n_rows // tm,),
        in_specs=[pl.BlockSpec((tm, kh), lambda i: (i, 0)),
                  pl.BlockSpec((tm, kh), lambda i: (i, a2_col_block)),
                  pl.BlockSpec((kh, d), lambda i: (0, 0)),
                  pl.BlockSpec((kh, d), lambda i: (1, 0)),
                  pl.BlockSpec((tm, d), lambda i: (i, 0)),
                  _mod_spec(d, layer, 5, 1)],
        out_specs=pl.BlockSpec((tm, d), lambda i: (i, 0)),
        compiler_params=_cparams(1),
        name="mixer_outproj",
    )(a1, a2, w, w, x_all, mod)


def _swap_halves_32(y):
    lane = lax.broadcasted_iota(jnp.int32, y.shape, 1)
    return jnp.where((lane & 32) == 0, pltpu.roll(y, LANES - 32, 1), pltpu.roll(y, 32, 1))


def _prep_kernel(*refs, segs, n_lat_tiles):
    n_seg = len(segs)
    seg_refs = refs[:n_seg]
    gains_ref, cos_ref, sin_ref, o_ref = refs[n_seg:]
    is_lat = pl.program_id(0) < n_lat_tiles
    col = 0
    for ref, (n_heads, gain_row, rope) in zip(seg_refs, segs):
        for h in range(n_heads):
            xh = ref[:, h * HEAD_DIM:(h + 1) * HEAD_DIM]
            if gain_row is not None:
                y = xh * lax.rsqrt(jnp.mean(xh * xh, axis=-1, keepdims=True) + EPS)
                y = y * gains_ref[gain_row:gain_row + 1, :]
                if rope:
                    y = jnp.where(is_lat, y * cos_ref[...] + _swap_halves_32(y) * sin_ref[...], y)
            else:
                y = xh
            o_ref[:, col:col + HEAD_DIM] = y.astype(BF16)
            col += HEAD_DIM


def _prep(p, seg_cols, segs, gains, cos_t, sin_t, *, tm, n_lat_rows):
    ta = p.shape[0]
    total = sum(s[0] for s in segs) * HEAD_DIM
    n_lat_tiles = n_lat_rows // tm
    n_pos_tiles = cos_t.shape[0] // tm
    in_specs = []
    for c0, (n_heads, _, _) in zip(seg_cols, segs):
        width = n_heads * HEAD_DIM
        in_specs.append(pl.BlockSpec((tm, width), functools.partial(lambda i, cb: (i, cb), cb=c0 // width)))
    in_specs += [pl.BlockSpec(gains.shape, lambda i: (0, 0)),
                 pl.BlockSpec((tm, HEAD_DIM), lambda i: (i % n_pos_tiles, 0)),
                 pl.BlockSpec((tm, HEAD_DIM), lambda i: (i % n_pos_tiles, 0))]
    kern = functools.partial(_prep_kernel, segs=tuple(segs), n_lat_tiles=n_lat_tiles)
    return pl.pallas_call(
        kern,
        out_shape=jax.ShapeDtypeStruct((ta, total), BF16),
        grid=(ta // tm,),
        in_specs=in_specs,
        out_specs=pl.BlockSpec((tm, total), lambda i: (i, 0)),
        compiler_params=_cparams(1),
        name="head_prep",
    )(*([p] * len(segs)), gains, cos_t, sin_t)


def _rope_tables(seq_len):
    half = HEAD_DIM // 2
    nf = half // 2
    inv = ROPE_THETA ** (-jnp.arange(nf, dtype=F32) / nf)
    pos = jnp.arange(seq_len, dtype=jnp.int32)
    ang_r = (pos // GRID_W).astype(F32)[:, None] * inv[None, :]
    ang_c = (pos % GRID_W).astype(F32)[:, None] * inv[None, :]
    cr, sr, cc, sc = jnp.cos(ang_r), jnp.sin(ang_r), jnp.cos(ang_c), jnp.sin(ang_c)
    return (jnp.concatenate([cr, cr, cc, cc], axis=-1),
            jnp.concatenate([-sr, sr, -sc, sc], axis=-1))


def _gelu_tanh(x):
    return x * (0.5 * (1.0 + jnp.tanh(np.float32(np.sqrt(2.0 / np.pi)) * (x + 0.044715 * (x * x * x)))))


def _lru_kernel(*refs, tt, reverse, final, n_lat_chunks):
    if final:
        (xp_ref, xc_ref, xn_ref, cw_ref, cb_ref, wa_ref, ba_ref, wx_ref, bx_ref, lam_ref,
         hf_ref, gl_ref, o_ref, xs_scr, a_scr, b_scr, carry_scr) = refs
    else:
        (xp_ref, xc_ref, xn_ref, cw_ref, cb_ref, wa_ref, ba_ref, wx_ref, bx_ref, lam_ref,
         o_ref, xs_scr, a_scr, b_scr, carry_scr) = refs
    s = pl.program_id(1)
    is_ctx = s == 0
    j = (n_lat_chunks - s) if reverse else (s - 1)
    has_prev = jnp.logical_and(jnp.logical_not(is_ctx), j > 0)
    has_next = jnp.logical_and(jnp.logical_not(is_ctx), j < n_lat_chunks - 1)

    @pl.when(is_ctx)
    def _():
        carry_scr[...] = jnp.zeros_like(carry_scr)

    halo = SUBLANES
    xs_scr[0:halo, :] = jnp.where(has_prev, xp_ref[tt - halo:tt, :], 0.0)
    xs_scr[halo:halo + tt, :] = xc_ref[...]
    xs_scr[halo + tt:2 * halo + tt, :] = jnp.where(has_next, xn_ref[0:halo, :], 0.0)

    width = xc_ref.shape[1]
    left = CONV_W // 2
    z = -lam_ref[...]
    softplus = jnp.maximum(z, 0.0) + jnp.log1p(jnp.exp(-jnp.abs(z)))
    for n in range(width // HEAD_DIM):
        cs = slice(n * HEAD_DIM, (n + 1) * HEAD_DIM)
        u = cb_ref[:, cs]
        for k in range(CONV_W):
            u = u + xs_scr[halo - left + k:halo - left + k + tt, cs] * cw_ref[k:k + 1, cs]
        ub = u.astype(BF16)
        gate_r = jnp.dot(ub, wa_ref[n], preferred_element_type=F32) + ba_ref[:, cs]
        gate_i = jnp.dot(ub, wx_ref[n], preferred_element_type=F32) + bx_ref[:, cs]
        log_a = (-LRU_C) * _sigmoid(gate_r) * softplus[:, cs]
        a = jnp.exp(log_a)
        a_scr[:, cs] = a
        b_scr[:, cs] = jnp.sqrt(-jnp.tanh(log_a) * (1.0 + a * a)) * (_sigmoid(gate_i) * u)

    n_groups = tt // SUBLANES
    row = lax.broadcasted_iota(jnp.int32, (SUBLANES, width), 0)

    def group(gi, carry):
        g = (n_groups - 1 - gi) if reverse else gi
        r0 = pl.multiple_of(g * SUBLANES, SUBLANES)
        a = a_scr[pl.ds(r0, SUBLANES), :]
        b = b_scr[pl.ds(r0, SUBLANES), :]
        for k in (1, 2, 4):
            if reverse:
                keep = row < SUBLANES - k
                shift = SUBLANES - k
            else:
                keep = row >= k
                shift = k
            a_sh = jnp.where(keep, pltpu.roll(a, shift, 0), 1.0)
            b_sh = jnp.where(keep, pltpu.roll(b, shift, 0), 0.0)
            b = a * b_sh + b
            a = a * a_sh
        h = b + a * carry
        b_scr[pl.ds(r0, SUBLANES), :] = h
        last = h[0:1, :] if reverse else h[SUBLANES - 1:SUBLANES, :]
        return jnp.broadcast_to(last, (SUBLANES, width))

    carry_scr[...] = lax.fori_loop(0, n_groups, group, carry_scr[...])

    if final:
        o_ref[...] = ((hf_ref[...] + b_scr[...]) * _gelu_tanh(gl_ref[...])).astype(o_ref.dtype)
    else:
        o_ref[...] = b_scr[...]


def _lru_pass(p, conv_w, conv_b, w_a, b_a, w_x, b_x, lam, hf, *, reverse, n_batch, seq_len, ctx_len, tt):
    ta = p.shape[0]
    width = conv_w.shape[1]
    assert ctx_len == tt and seq_len % tt == 0
    n_lat_chunks = seq_len // tt
    ctx_block0 = n_batch * n_lat_chunks
    final = hf is not None

    def chunk_index(b, s, off):
        j = (n_lat_chunks - s) if reverse else (s - 1)
        j = jnp.clip(j + off, 0, n_lat_chunks - 1)
        return jnp.where(s == 0, ctx_block0 + b, b * n_lat_chunks + j)

    def xspec(off):
        return pl.BlockSpec((tt, width), lambda b, s: (chunk_index(b, s, off), 0))

    row_spec = pl.BlockSpec((1, width), lambda b, s: (0, 0))
    w_spec = pl.BlockSpec(w_a.shape, lambda b, s: (0, 0, 0))
    in_specs = [xspec(-1), xspec(0), xspec(1),
                pl.BlockSpec((CONV_W, width), lambda b, s: (0, 0)), row_spec,
                w_spec, row_spec, w_spec, row_spec, row_spec]
    args = [p, p, p, conv_w, conv_b.reshape(1, width), w_a, b_a.reshape(1, width),
            w_x, b_x.reshape(1, width), lam.reshape(1, width)]
    if final:
        in_specs += [xspec(0), pl.BlockSpec((tt, width), lambda b, s: (chunk_index(b, s, 0), 1))]
        args += [hf, p]
    kern = functools.partial(_lru_kernel, tt=tt, reverse=reverse, final=final, n_lat_chunks=n_lat_chunks)
    return pl.pallas_call(
        kern,
        out_shape=jax.ShapeDtypeStruct((ta, width), BF16 if final else F32),
        grid=(n_batch, n_lat_chunks + 1),
        in_specs=in_specs,
        out_specs=xspec(0),
        scratch_shapes=[pltpu.VMEM((tt + 2 * SUBLANES, width), F32), pltpu.VMEM((tt, width), F32),
                        pltpu.VMEM((tt, width), F32), pltpu.VMEM((SUBLANES, width), F32)],
        compiler_params=_cparams(2),
        name="rglru_rev" if reverse else "rglru_fwd",
    )(*args)


def _softmax_rows(s, extra_logit=None):
    m = jnp.max(s, axis=-1, keepdims=True)
    if extra_logit is not None:
        m = jnp.maximum(m, extra_logit)
    e = jnp.exp(s - m)
    denom = jnp.sum(e, axis=-1, keepdims=True)
    if extra_logit is not None:
        denom = denom + jnp.exp(extra_logit - m)
    return e / denom


def _swa_kernel(sink_ref, q_ref, kp_ref, kc_ref, kn_ref, kx_ref, vp_ref, vc_ref, vn_ref, vx_ref, mask_ref, o_ref,
                *, n_kv_heads):
    blk = q_ref.shape[0]
    head = lax.broadcasted_iota(jnp.int32, (GQA_GROUP * blk, 1), 0) // blk
    for kh in range(n_kv_heads):
        ks = slice(kh * HEAD_DIM, (kh + 1) * HEAD_DIM)
        q0 = kh * GQA_GROUP
        q = jnp.concatenate([q_ref[:, (q0 + h) * HEAD_DIM:(q0 + h + 1) * HEAD_DIM] for h in range(GQA_GROUP)],
                            axis=0)
        k = jnp.concatenate([kp_ref[:, ks], kc_ref[:, ks], kn_ref[:, ks], kx_ref[:, ks]], axis=0)
        v = jnp.concatenate([vp_ref[:, ks], vc_ref[:, ks], vn_ref[:, ks], vx_ref[:, ks]], axis=0)
        s = lax.dot_general(q, k, (((1,), (1,)), ((), ())), preferred_element_type=F32) * ATTN_SCALE
        s = s + mask_ref[...]
        sink = jnp.zeros((GQA_GROUP * blk, 1), F32)
        for h in range(GQA_GROUP):
            sink = jnp.where(head == h, sink_ref[q0 + h], sink)
        p = _softmax_rows(s, sink)
        o = jnp.dot(p.astype(BF16), v, preferred_element_type=F32)
        for h in range(GQA_GROUP):
            o_ref[:, (q0 + h) * HEAD_DIM:(q0 + h + 1) * HEAD_DIM] = o[h * blk:(h + 1) * blk, :].astype(o_ref.dtype)


def _swa_mask_table(blk, ctx_len):
    shape = (4, GQA_GROUP * blk, 3 * blk + ctx_len)
    cls = lax.broadcasted_iota(jnp.int32, shape, 0)
    qi = lax.broadcasted_iota(jnp.int32, shape, 1) % blk
    col = lax.broadcasted_iota(jnp.int32, shape, 2)
    lo = jnp.where(cls == 0, blk, 0)
    hi = jnp.where(cls == 3, 0, jnp.where(cls == 2, 2 * blk, 3 * blk))
    rel = col - blk - qi
    ok = ((col >= lo) & (col < hi) & (rel >= -WINDOW) & (rel <= WINDOW)) | (col >= 3 * blk)
    return jnp.where(ok, 0.0, NEG).astype(F32)


def _swa(qkv, sink, *, n_batch, seq_len, ctx_len, n_kv_heads, blk):
    ta = qkv.shape[0]
    n_heads = n_kv_heads * GQA_GROUP
    n_lat_blocks = seq_len // blk
    n_ctx_blocks = ctx_len // blk
    assert blk == WINDOW and seq_len % blk == 0 and ctx_len % blk == 0 and n_lat_blocks >= 2
    ctx_q0 = n_batch * n_lat_blocks
    ctx_kv0 = n_batch * seq_len // ctx_len
    kv_w = n_kv_heads * HEAD_DIM
    k_col, v_col = n_heads * HEAD_DIM // kv_w, n_heads * HEAD_DIM // kv_w + 1

    def q_index(b, n):
        return (jnp.where(n < n_lat_blocks, b * n_lat_blocks + n, ctx_q0 + b * n_ctx_blocks + n - n_lat_blocks), 0)

    def kv_spec(off, col):
        return pl.BlockSpec((blk, kv_w),
                            lambda b, n: (b * n_lat_blocks + jnp.clip(n + off, 0, n_lat_blocks - 1), col))

    def ctx_spec(col):
        return pl.BlockSpec((ctx_len, kv_w), lambda b, n: (ctx_kv0 + b, col))

    def mask_index(b, n):
        interior = jnp.where(n == 0, 0, jnp.where(n == n_lat_blocks - 1, 2, 1))
        return (jnp.where(n < n_lat_blocks, interior, 3), 0, 0)

    mask = _swa_mask_table(blk, ctx_len)
    kern = functools.partial(_swa_kernel, n_kv_heads=n_kv_heads)
    return pl.pallas_call(
        kern,
        out_shape=jax.ShapeDtypeStruct((ta, n_heads * HEAD_DIM), BF16),
        grid=(n_batch, n_lat_blocks + n_ctx_blocks),
        in_specs=[pl.BlockSpec(memory_space=pltpu.SMEM),
                  pl.BlockSpec((blk, n_heads * HEAD_DIM), q_index),
                  kv_spec(-1, k_col), kv_spec(0, k_col), kv_spec(1, k_col), ctx_spec(k_col),
                  kv_spec(-1, v_col), kv_spec(0, v_col), kv_spec(1, v_col), ctx_spec(v_col),
                  pl.BlockSpec((None,) + mask.shape[1:], mask_index)],
        out_specs=pl.BlockSpec((blk, n_heads * HEAD_DIM), q_index),
        compiler_params=_cparams(2),
        name="windowed_gqa",
    )(sink, *([qkv] * 9), mask)


NA_HEADS_PER_STEP = 4


def _na_kernel(q_ref, kp_ref, kc_ref, kn_ref, kx_ref, vp_ref, vc_ref, vn_ref, vx_ref, bias_ref, o_ref):
    n_local = bias_ref.shape[-1]
    for h in range(NA_HEADS_PER_STEP):
        hs = slice(h * HEAD_DIM, (h + 1) * HEAD_DIM)
        k = jnp.concatenate([kp_ref[:, hs], kc_ref[:, hs], kn_ref[:, hs], kx_ref[:, hs]], axis=0)
        v = jnp.concatenate([vp_ref[:, hs], vc_ref[:, hs], vn_ref[:, hs], vx_ref[:, hs]], axis=0)
        s = lax.dot_general(q_ref[:, hs], k, (((1,), (1,)), ((), ())), preferred_element_type=F32) * ATTN_SCALE
        s = jnp.concatenate([s[:, :n_local] + bias_ref[h], s[:, n_local:]], axis=1)
        p = _softmax_rows(s)
        o_ref[:, hs] = jnp.dot(p.astype(BF16), v, preferred_element_type=F32).astype(o_ref.dtype)


def _na_bias_table(rpb, n_grid_rows):
    r_in, w = NA_ROWS_PER_BLOCK, GRID_W
    kh = min(NA_KH, n_grid_rows)
    n_blocks = n_grid_rows // r_in
    n_ro, n_co = 2 * NA_KH - 1, 2 * NA_KW - 1
    qc, kc = np.arange(w)[:, None], np.arange(w)[None, :]
    e_col = (kc - qc + NA_KW - 1 == np.arange(n_co)[:, None, None]).astype(np.float32)
    win_start = np.clip(qc - NA_KW // 2, 0, w - NA_KW)
    col_ok = (kc >= win_start) & (kc < win_start + NA_KW)
    ri, krj = np.arange(r_in)[:, None], np.arange(3 * r_in)[None, :]
    e_row = (krj - ri - r_in + NA_KH - 1 == np.arange(n_ro)[:, None, None]).astype(np.float32)
    row_ok = []
    for jb in (0, max(n_blocks // 2, 1) if n_blocks > 2 else 0, n_blocks - 1):
        r, kr = r_in * jb + ri, r_in * (jb - 1) + krj
        rs = np.clip(r - kh // 2, 0, n_grid_rows - kh)
        row_ok.append((kr >= rs) & (kr < rs + kh))
    ok = np.stack(row_ok)[:, :, None, :, None] & col_ok[None, None, :, None, :]
    hi = lax.Precision.HIGHEST
    tz = jnp.einsum('hab,bqc->haqc', rpb.astype(F32), e_col, precision=hi)
    full = jnp.einsum('haqc,aik->hiqkc', tz, e_row, precision=hi)
    tab = jnp.where(ok[None], full[:, None], NEG)
    return tab.reshape(rpb.shape[0], 3, r_in * w, 3 * r_in * w)


def _na(qkv, bias, *, n_batch, seq_len, ctx_len, n_heads):
    tq = NA_ROWS_PER_BLOCK * GRID_W
    n_blocks = seq_len // tq
    assert ctx_len == tq and seq_len % tq == 0 and NA_KH == 2 * NA_ROWS_PER_BLOCK and n_blocks >= 2
    assert n_heads % NA_HEADS_PER_STEP == 0
    n_groups = n_heads // NA_HEADS_PER_STEP
    gw = NA_HEADS_PER_STEP * HEAD_DIM
    ctx0 = n_batch * n_blocks

    def kv_spec(off, col0):
        return pl.BlockSpec((tq, gw),
                            lambda b, g, jb: (b * n_blocks + jnp.clip(jb + off, 0, n_blocks - 1), col0 + g))

    def ctx_spec(col0):
        return pl.BlockSpec((ctx_len, gw), lambda b, g, jb: (ctx0 + b, col0 + g))

    def bias_index(b, g, jb):
        return (g, jnp.where(jb == 0, 0, jnp.where(jb == n_blocks - 1, 2, 1)), 0, 0)

    return pl.pallas_call(
        _na_kernel,
        out_shape=jax.ShapeDtypeStruct((n_batch * seq_len, n_heads * HEAD_DIM), BF16),
        grid=(n_batch, n_groups, n_blocks),
        in_specs=[kv_spec(0, 0),
                  kv_spec(-1, n_groups), kv_spec(0, n_groups), kv_spec(1, n_groups), ctx_spec(n_groups),
                  kv_spec(-1, 2 * n_groups), kv_spec(0, 2 * n_groups), kv_spec(1, 2 * n_groups),
                  ctx_spec(2 * n_groups),
                  pl.BlockSpec((NA_HEADS_PER_STEP, None) + bias.shape[2:], bias_index)],
        out_specs=pl.BlockSpec((tq, gw), lambda b, g, jb: (b * n_blocks + jb, g)),
        compiler_params=_cparams(3),
        name="neighbourhood_attn",
    )(*([qkv] * 9), bias)


def _round_up(n, m):
    return (n + m - 1) // m * m


def _tiles(d_ff):
    tf = 512
    return dict(tm=1024, tm_out=512, tf=tf, f_pad=_round_up(d_ff, tf), tn=2048, tn_mod=1024)


def _ffn_weights(w_in, w_out, f_pad):
    f = w_out.shape[0]
    pad = f_pad - f
    wg = jnp.pad(w_in[:, :f].astype(BF16), ((0, 0), (0, pad)))
    wu = jnp.pad(w_in[:, f:].astype(BF16), ((0, 0), (0, pad)))
    return jnp.concatenate([wg, wu], axis=1), jnp.pad(w_out.astype(BF16), ((0, pad), (0, 0)))


def kernel(x, c, ctx, c_ctx, w_mod, b_mod, norm_g, ffn_w_in, ffn_w_out, ab_w_in, lru_conv_w, lru_conv_b,
           lru_w_a, lru_b_a, lru_w_x, lru_b_x, lru_lambda, attn_q_norm, attn_k_norm, attn_sink, ab_w_out,
           na_w_in, na_q_norm, na_k_norm, na_rpb, na_w_out):
    n_batch, seq_len, d = x.shape
    ctx_len = ctx.shape[1]
    depth = w_mod.shape[0]
    d_ff = ffn_w_out.shape[2]
    lru_w = lru_conv_w.shape[2]
    n_kv = attn_sink.shape[1] // GQA_GROUP
    n_att = n_kv * GQA_GROUP
    na_heads = na_rpb.shape[1]
    assert n_batch < MOD_ROWS and depth == 2
    t = _tiles(d_ff)
    tl, tc = n_batch * seq_len, n_batch * ctx_len
    geom = dict(rows_per_batch=seq_len, n_batch=n_batch)

    x_all = jnp.concatenate([x.reshape(tl, d), ctx.reshape(tc, d)], axis=0)
    cc = jnp.zeros((MOD_ROWS, d), F32).at[:n_batch].set(c).at[n_batch].set(c_ctx)
    mod = _mod_table(cc, w_mod, b_mod, t["tn_mod"])

    def ffn(x_in, n_rows, layer, which):
        wgu, wo = _ffn_weights(ffn_w_in[layer, which], ffn_w_out[layer, which], t["f_pad"])
        return _ffn(x_in, n_rows, mod, layer, 6 * which, norm_g[layer, 2 * which], wgu, wo,
                    tm=t["tm"], tf=t["tf"], **geom)

    cos_t, sin_t = _rope_tables(seq_len)

    x_all = ffn(x_all, tl + tc, 0, 0)
    p = _inproj(x_all, mod, 0, norm_g[0, 1], ab_w_in[0].astype(BF16), tm=t["tm"], tn=t["tn"], **geom)
    lru_args = lambda dr: (lru_conv_w[0], lru_conv_b[0], lru_w_a[0, dr].astype(BF16), lru_b_a[0, dr],
                           lru_w_x[0, dr].astype(BF16), lru_b_x[0, dr], lru_lambda[0, dr])
    scan_geom = dict(n_batch=n_batch, seq_len=seq_len, ctx_len=ctx_len, tt=ctx_len)
    h_fwd = _lru_pass(p, *lru_args(0), None, reverse=False, **scan_geom)
    lru = _lru_pass(p, *lru_args(1), h_fwd, reverse=True, **scan_geom)
    gains = jnp.stack([attn_q_norm[0], attn_k_norm[0]])
    q_col, k_col = 2 * lru_w, 2 * lru_w + n_att * HEAD_DIM
    qkv = _prep(p, (q_col, k_col, k_col + n_kv * HEAD_DIM),
                ((n_att, 0, True), (n_kv, 1, True), (n_kv, None, False)),
                gains, cos_t, sin_t, tm=ctx_len, n_lat_rows=tl)
    att = _swa(qkv, attn_sink[0], n_batch=n_batch, seq_len=seq_len, ctx_len=ctx_len, n_kv_heads=n_kv,
               blk=WINDOW)
    x_all = _outproj(lru, att, 0, ab_w_out[0].astype(BF16), x_all, tl + tc, mod, 0, tm=t["tm_out"], **geom)
    x_all = ffn(x_all, tl + tc, 0, 1)

    x_all = ffn(x_all, tl + tc, 1, 0)
    p = _inproj(x_all, mod, 1, norm_g[1, 1], na_w_in[0].astype(BF16), tm=t["tm"], tn=t["tn"], **geom)
    gains = jnp.stack([na_q_norm[0], na_k_norm[0]])
    qkv = _prep(p, (0, na_heads * HEAD_DIM, 2 * na_heads * HEAD_DIM),
                ((na_heads, 0, False), (na_heads, 1, False), (na_heads, None, False)),
                gains, cos_t, sin_t, tm=ctx_len, n_lat_rows=tl)
    o = _na(qkv, _na_bias_table(na_rpb[0], seq_len // GRID_W), n_batch=n_batch, seq_len=seq_len, ctx_len=ctx_len,
            n_heads=na_heads)
    x_lat = _outproj(o, o, 1, na_w_out[0].astype(BF16), x_all, tl, mod, 1, tm=t["tm_out"], **geom)
    x_lat = ffn(x_lat, tl, 1, 1)
    return x_lat.reshape(n_batch, seq_len, d)
```

```python
import functools

import numpy as np
import jax
import jax.numpy as jnp
from jax import lax
from jax.experimental import pallas as pl
from jax.experimental.pallas import tpu as pltpu

HEAD_DIM = 128
EPS = 1e-6
N_MOD = 9
GRID_W = 64
CONV_W = 4
LRU_C = 8.0
WINDOW = 128
ROPE_THETA = 10000.0
NA_KH = 8
NA_KW = 16
ATTN_SCALE = HEAD_DIM ** -0.5
GQA_GROUP = 4

LANES = 128
SUBLANES = 8
VMEM_LIMIT_BYTES = 56 * 1024 * 1024
MOD_ROWS = SUBLANES

NEG = -1e30
NA_ROWS_PER_BLOCK = 4

BF16 = jnp.bfloat16
F32 = jnp.float32


def _cparams(n_axes):
    return pltpu.CompilerParams(dimension_semantics=("arbitrary",) * n_axes,
                                vmem_limit_bytes=VMEM_LIMIT_BYTES)


def _sigmoid(x):
    return 1.0 / (1.0 + jnp.exp(-x))


NORM_CHUNK_ROWS = 32


def _norm_modulate_into(h_scr, x_ref, g, shift, scale):
    gain = g * (1.0 + scale)

    def chunk(c, carry):
        r0 = pl.multiple_of(c * NORM_CHUNK_ROWS, NORM_CHUNK_ROWS)
        x = x_ref[pl.ds(r0, NORM_CHUNK_ROWS), :]
        inv = lax.rsqrt(jnp.mean(x * x, axis=-1, keepdims=True) + EPS)
        h_scr[pl.ds(r0, NORM_CHUNK_ROWS), :] = ((x * inv) * gain + shift).astype(h_scr.dtype)
        return carry

    lax.fori_loop(0, x_ref.shape[0] // NORM_CHUNK_ROWS, chunk, 0, unroll=2)


def _mod_kernel(c_ref, w_ref, b_ref, o_ref):
    c = c_ref[...]
    s = c * _sigmoid(c)
    o_ref[...] = jnp.dot(s.astype(BF16), w_ref[...].astype(BF16),
                         preferred_element_type=F32) + b_ref[...]


def _mod_table(cc, w_mod, b_mod, tn):
    depth, d, n = w_mod.shape
    return pl.pallas_call(
        _mod_kernel,
        out_shape=jax.ShapeDtypeStruct((depth, MOD_ROWS, n), F32),
        grid=(depth, n // tn),
        in_specs=[pl.BlockSpec((MOD_ROWS, d), lambda l, j: (0, 0)),
                  pl.BlockSpec((None, d, tn), lambda l, j: (l, 0, j)),
                  pl.BlockSpec((None, 1, tn), lambda l, j: (l, 0, j))],
        out_specs=pl.BlockSpec((None, MOD_ROWS, tn), lambda l, j: (l, 0, j)),
        compiler_params=_cparams(2),
        name="mod_table",
    )(cc, w_mod, b_mod.reshape(depth, 1, n))


def _mod_spec(d, layer, k, n_grid_axes):
    if n_grid_axes == 1:
        return pl.BlockSpec((None, MOD_ROWS, d), lambda i: (layer, 0, k))
    return pl.BlockSpec((None, MOD_ROWS, d), lambda i, j: (layer, 0, k))


def _mod_row(tile_idx, tm, rows_per_batch, n_batch):
    return jnp.minimum(tile_idx * tm // rows_per_batch, n_batch)


def _ffn_kernel(x_ref, g_ref, sh_ref, sc_ref, gt_ref, wg_ref, wu_ref, wo_ref, o_ref, h_scr,
                *, tm, rows_per_batch, n_batch):
    i, j = pl.program_id(0), pl.program_id(1)
    r = _mod_row(i, tm, rows_per_batch, n_batch)

    @pl.when(j == 0)
    def _():
        _norm_modulate_into(h_scr, x_ref, g_ref[...], sh_ref[pl.ds(r, 1), :], sc_ref[pl.ds(r, 1), :])
        o_ref[...] = jnp.zeros_like(o_ref)

    h = h_scr[...]
    a = jnp.dot(h, wg_ref[...], preferred_element_type=F32)
    u = jnp.dot(h, wu_ref[...], preferred_element_type=F32)
    act = (a * _sigmoid(a)) * u
    o_ref[...] += jnp.dot(act.astype(BF16), wo_ref[...], preferred_element_type=F32)

    @pl.when(j == pl.num_programs(1) - 1)
    def _():
        o_ref[...] = x_ref[...] + 0.5 * gt_ref[pl.ds(r, 1), :] * o_ref[...]


def _ffn(x_all, n_rows, mod, layer, k0, g, wg, wu, wo, *, tm, tf, rows_per_batch, n_batch):
    d = x_all.shape[1]
    nf = wo.shape[0] // tf
    kern = functools.partial(_ffn_kernel, tm=tm, rows_per_batch=rows_per_batch, n_batch=n_batch)
    return pl.pallas_call(
        kern,
        out_shape=jax.ShapeDtypeStruct((n_rows, d), F32),
        grid=(n_rows // tm, nf),
        in_specs=[pl.BlockSpec((tm, d), lambda i, j: (i, 0)),
                  pl.BlockSpec((1, d), lambda i, j: (0, 0)),
                  _mod_spec(d, layer, k0, 2), _mod_spec(d, layer, k0 + 1, 2),
                  _mod_spec(d, layer, k0 + 2, 2),
                  pl.BlockSpec((d, tf), lambda i, j: (0, j)),
                  pl.BlockSpec((d, tf), lambda i, j: (0, j)),
                  pl.BlockSpec((tf, d), lambda i, j: (j, 0))],
        out_specs=pl.BlockSpec((tm, d), lambda i, j: (i, 0)),
        scratch_shapes=[pltpu.VMEM((tm, d), BF16)],
        compiler_params=_cparams(2),
        name="half_ffn",
    )(x_all, g.reshape(1, d), mod, mod, mod, wg, wu, wo)


PROJ_CHUNK = 4 * HEAD_DIM


def _swap_halves_32(y):
    lane = lax.broadcasted_iota(jnp.int32, y.shape, 1)
    return jnp.where((lane & 32) == 0, pltpu.roll(y, LANES - 32, 1), pltpu.roll(y, 32, 1))


def _inproj_kernel(x_ref, g_ref, sh_ref, sc_ref, w_ref, gains_ref, cos_ref, sin_ref, *rest,
                   plan, tm, rows_per_batch, n_batch, n_lat_tiles):
    out_refs, h_scr = rest[:-1], rest[-1]
    i = pl.program_id(0)
    r = _mod_row(i, tm, rows_per_batch, n_batch)
    is_lat = i < n_lat_tiles
    _norm_modulate_into(h_scr, x_ref, g_ref[...], sh_ref[pl.ds(r, 1), :], sc_ref[pl.ds(r, 1), :])
    for w_col, width, out_idx, out_col, ops in plan:
        y = jnp.dot(h_scr[...], w_ref[:, w_col:w_col + width], preferred_element_type=F32)
        o_ref = out_refs[out_idx]
        if ops is None:
            o_ref[:, out_col:out_col + width] = y.astype(o_ref.dtype)
            continue
        for hh, op in enumerate(ops):
            yh = y[:, hh * HEAD_DIM:(hh + 1) * HEAD_DIM]
            if op is not None:
                gain_row, rope = op
                yh = yh * lax.rsqrt(jnp.mean(yh * yh, axis=-1, keepdims=True) + EPS)
                yh = yh * gains_ref[gain_row:gain_row + 1, :]
                if rope:
                    yh = jnp.where(is_lat, yh * cos_ref[...] + _swap_halves_32(yh) * sin_ref[...], yh)
            o_ref[:, out_col + hh * HEAD_DIM:out_col + (hh + 1) * HEAD_DIM] = yh.astype(o_ref.dtype)


def _inproj(x_all, mod, layer, g, w, gains, cos_t, sin_t, plan, outs, *, tm, n_lat_rows, rows_per_batch, n_batch):
    ta, d = x_all.shape
    n_pos_tiles = cos_t.shape[0] // tm
    kern = functools.partial(_inproj_kernel, plan=tuple(plan), tm=tm, rows_per_batch=rows_per_batch,
                             n_batch=n_batch, n_lat_tiles=n_lat_rows // tm)
    return pl.pallas_call(
        kern,
        out_shape=[jax.ShapeDtypeStruct((ta, n), dt) for n, dt in outs],
        grid=(ta // tm,),
        in_specs=[pl.BlockSpec((tm, d), lambda i: (i, 0)),
                  pl.BlockSpec((1, d), lambda i: (0, 0)),
                  _mod_spec(d, layer, 3, 1), _mod_spec(d, layer, 4, 1),
                  pl.BlockSpec(w.shape, lambda i: (0, 0), pipeline_mode=pl.Buffered(1)),
                  pl.BlockSpec(gains.shape, lambda i: (0, 0)),
                  pl.BlockSpec((tm, HEAD_DIM), lambda i: (i % n_pos_tiles, 0)),
                  pl.BlockSpec((tm, HEAD_DIM), lambda i: (i % n_pos_tiles, 0))],
        out_specs=[pl.BlockSpec((tm, n), lambda i: (i, 0)) for n, _ in outs],
        scratch_shapes=[pltpu.VMEM((tm, d), BF16)],
        compiler_params=_cparams(1),
        name="mixer_inproj",
    )(x_all, g.reshape(1, d), mod, mod, w, gains, cos_t, sin_t)


def _outproj_kernel(a1_ref, a2_ref, w1_ref, w2_ref, x_ref, gt_ref, o_ref, *, tm, rows_per_batch, n_batch):
    r = _mod_row(pl.program_id(0), tm, rows_per_batch, n_batch)
    y = (jnp.dot(a1_ref[...], w1_ref[...], preferred_element_type=F32)
         + jnp.dot(a2_ref[...], w2_ref[...], preferred_element_type=F32))
    o_ref[...] = x_ref[...] + gt_ref[pl.ds(r, 1), :] * y


def _outproj(a1, a2, a2_col_block, w, x_all, n_rows, mod, layer, *, tm, rows_per_batch, n_batch):
    d = x_all.shape[1]
    kh = w.shape[0] // 2
    kern = functools.partial(_outproj_kernel, tm=tm, rows_per_batch=rows_per_batch, n_batch=n_batch)
    return pl.pallas_call(
        kern,
        out_shape=jax.ShapeDtypeStruct((n_rows, d), F32),
        grid=(n_rows // tm,),
        in_specs=[pl.BlockSpec((tm, kh), lambda i: (i, 0)),
                  pl.BlockSpec((tm, kh), lambda i: (i, a2_col_block)),
                  pl.BlockSpec((kh, d), lambda i: (0, 0)),
                  pl.BlockSpec((kh, d), lambda i: (1, 0)),
                  pl.BlockSpec((tm, d), lambda i: (i, 0)),
                  _mod_spec(d, layer, 5, 1)],
        out_specs=pl.BlockSpec((tm, d), lambda i: (i, 0)),
        compiler_params=_cparams(1),
        name="mixer_outproj",
    )(a1, a2, w, w, x_all, mod)


def _rope_tables(seq_len):
    half = HEAD_DIM // 2
    nf = half // 2
    inv = ROPE_THETA ** (-jnp.arange(nf, dtype=F32) / nf)
    pos = jnp.arange(seq_len, dtype=jnp.int32)
    ang_r = (pos // GRID_W).astype(F32)[:, None] * inv[None, :]
    ang_c = (pos % GRID_W).astype(F32)[:, None] * inv[None, :]
    cr, sr, cc, sc = jnp.cos(ang_r), jnp.sin(ang_r), jnp.cos(ang_c), jnp.sin(ang_c)
    return (jnp.concatenate([cr, cr, cc, cc], axis=-1),
            jnp.concatenate([-sr, sr, -sc, sc], axis=-1))


def _gelu_tanh(x):
    return x * (0.5 * (1.0 + jnp.tanh(np.float32(np.sqrt(2.0 / np.pi)) * (x + 0.044715 * (x * x * x)))))


def _lru_kernel(*refs, tt, reverse, final, n_lat_chunks):
    if final:
        (xp_ref, xc_ref, xn_ref, cw_ref, cb_ref, wa_ref, ba_ref, wx_ref, bx_ref, lam_ref,
         hf_ref, gl_ref, o_ref, xs_scr, a_scr, b_scr, carry_scr) = refs
    else:
        (xp_ref, xc_ref, xn_ref, cw_ref, cb_ref, wa_ref, ba_ref, wx_ref, bx_ref, lam_ref,
         o_ref, xs_scr, a_scr, b_scr, carry_scr) = refs
    s = pl.program_id(1)
    is_ctx = s == 0
    j = (n_lat_chunks - s) if reverse else (s - 1)
    has_prev = jnp.logical_and(jnp.logical_not(is_ctx), j > 0)
    has_next = jnp.logical_and(jnp.logical_not(is_ctx), j < n_lat_chunks - 1)

    @pl.when(is_ctx)
    def _():
        carry_scr[...] = jnp.zeros_like(carry_scr)

    halo = SUBLANES
    xs_scr[0:halo, :] = jnp.where(has_prev, xp_ref[tt - halo:tt, :], 0.0)
    xs_scr[halo:halo + tt, :] = xc_ref[...]
    xs_scr[halo + tt:2 * halo + tt, :] = jnp.where(has_next, xn_ref[0:halo, :], 0.0)

    width = xc_ref.shape[1]
    left = CONV_W // 2
    z = -lam_ref[...]
    softplus = jnp.maximum(z, 0.0) + jnp.log1p(jnp.exp(-jnp.abs(z)))
    for n in range(width // HEAD_DIM):
        cs = slice(n * HEAD_DIM, (n + 1) * HEAD_DIM)
        u = cb_ref[:, cs]
        for k in range(CONV_W):
            u = u + xs_scr[halo - left + k:halo - left + k + tt, cs] * cw_ref[k:k + 1, cs]
        ub = u.astype(BF16)
        gate_r = jnp.dot(ub, wa_ref[n], preferred_element_type=F32) + ba_ref[:, cs]
        gate_i = jnp.dot(ub, wx_ref[n], preferred_element_type=F32) + bx_ref[:, cs]
        log_a = (-LRU_C) * _sigmoid(gate_r) * softplus[:, cs]
        a = jnp.exp(log_a)
        a_scr[:, cs] = a
        b_scr[:, cs] = jnp.sqrt(-jnp.tanh(log_a) * (1.0 + a * a)) * (_sigmoid(gate_i) * u)

    n_groups = tt // SUBLANES
    row = lax.broadcasted_iota(jnp.int32, (SUBLANES, width), 0)

    def group(gi, carry):
        g = (n_groups - 1 - gi) if reverse else gi
        r0 = pl.multiple_of(g * SUBLANES, SUBLANES)
        a = a_scr[pl.ds(r0, SUBLANES), :]
        b = b_scr[pl.ds(r0, SUBLANES), :]
        for k in (1, 2, 4):
            if reverse:
                keep = row < SUBLANES - k
                shift = SUBLANES - k
            else:
                keep = row >= k
                shift = k
            a_sh = jnp.where(keep, pltpu.roll(a, shift, 0), 1.0)
            b_sh = jnp.where(keep, pltpu.roll(b, shift, 0), 0.0)
            b = a * b_sh + b
            a = a * a_sh
        h = b + a * carry
        b_scr[pl.ds(r0, SUBLANES), :] = h
        last = h[0:1, :] if reverse else h[SUBLANES - 1:SUBLANES, :]
        return jnp.broadcast_to(last, (SUBLANES, width))

    carry_scr[...] = lax.fori_loop(0, n_groups, group, carry_scr[...])

    if final:
        o_ref[...] = ((hf_ref[...] + b_scr[...]) * _gelu_tanh(gl_ref[...])).astype(o_ref.dtype)
    else:
        o_ref[...] = b_scr[...]


def _lru_pass(p, conv_w, conv_b, w_a, b_a, w_x, b_x, lam, hf, *, reverse, n_batch, seq_len, ctx_len, tt):
    ta = p.shape[0]
    width = conv_w.shape[1]
    assert ctx_len == tt and seq_len % tt == 0
    n_lat_chunks = seq_len // tt
    ctx_block0 = n_batch * n_lat_chunks
    final = hf is not None

    def chunk_index(b, s, off):
        j = (n_lat_chunks - s) if reverse else (s - 1)
        j = jnp.clip(j + off, 0, n_lat_chunks - 1)
        return jnp.where(s == 0, ctx_block0 + b, b * n_lat_chunks + j)

    def xspec(off):
        return pl.BlockSpec((tt, width), lambda b, s: (chunk_index(b, s, off), 0))

    row_spec = pl.BlockSpec((1, width), lambda b, s: (0, 0))
    w_spec = pl.BlockSpec(w_a.shape, lambda b, s: (0, 0, 0))
    in_specs = [xspec(-1), xspec(0), xspec(1),
                pl.BlockSpec((CONV_W, width), lambda b, s: (0, 0)), row_spec,
                w_spec, row_spec, w_spec, row_spec, row_spec]
    args = [p, p, p, conv_w, conv_b.reshape(1, width), w_a, b_a.reshape(1, width),
            w_x, b_x.reshape(1, width), lam.reshape(1, width)]
    if final:
        in_specs += [xspec(0), pl.BlockSpec((tt, width), lambda b, s: (chunk_index(b, s, 0), 1))]
        args += [hf, p]
    kern = functools.partial(_lru_kernel, tt=tt, reverse=reverse, final=final, n_lat_chunks=n_lat_chunks)
    return pl.pallas_call(
        kern,
        out_shape=jax.ShapeDtypeStruct((ta, width), BF16 if final else F32),
        grid=(n_batch, n_lat_chunks + 1),
        in_specs=in_specs,
        out_specs=xspec(0),
        scratch_shapes=[pltpu.VMEM((tt + 2 * SUBLANES, width), F32), pltpu.VMEM((tt, width), F32),
                        pltpu.VMEM((tt, width), F32), pltpu.VMEM((SUBLANES, width), F32)],
        compiler_params=_cparams(2),
        name="rglru_rev" if reverse else "rglru_fwd",
    )(*args)


LOG2E = float(np.log2(np.e))
LOGIT_SCALE = ATTN_SCALE * LOG2E


def _softmax_parts(s2, extra_logit2=None):
    m = jnp.max(s2, axis=-1, keepdims=True)
    if extra_logit2 is not None:
        m = jnp.maximum(m, extra_logit2)
    e = jnp.exp2(s2 - m)
    denom = jnp.sum(e, axis=-1, keepdims=True)
    if extra_logit2 is not None:
        denom = denom + jnp.exp2(extra_logit2 - m)
    return e, 1.0 / denom


def _swa_kernel(sink_ref, q_ref, kp_ref, kc_ref, kn_ref, kx_ref, vp_ref, vc_ref, vn_ref, vx_ref, mask_ref, o_ref,
                *, n_kv_heads):
    blk = q_ref.shape[0]
    head = lax.broadcasted_iota(jnp.int32, (GQA_GROUP * blk, 1), 0) // blk
    for kh in range(n_kv_heads):
        ks = slice(kh * HEAD_DIM, (kh + 1) * HEAD_DIM)
        q0 = kh * GQA_GROUP
        q = jnp.concatenate([q_ref[:, (q0 + h) * HEAD_DIM:(q0 + h + 1) * HEAD_DIM] for h in range(GQA_GROUP)],
                            axis=0)
        k = jnp.concatenate([kp_ref[:, ks], kc_ref[:, ks], kn_ref[:, ks], kx_ref[:, ks]], axis=0)
        v = jnp.concatenate([vp_ref[:, ks], vc_ref[:, ks], vn_ref[:, ks], vx_ref[:, ks]], axis=0)
        s = lax.dot_general(q, k, (((1,), (1,)), ((), ())), preferred_element_type=F32) * LOGIT_SCALE
        s = s + mask_ref[...]
        sink = jnp.zeros((GQA_GROUP * blk, 1), F32)
        for h in range(GQA_GROUP):
            sink = jnp.where(head == h, sink_ref[q0 + h] * LOG2E, sink)
        e, inv = _softmax_parts(s, sink)
        o = jnp.dot(e.astype(BF16), v, preferred_element_type=F32) * inv
        for h in range(GQA_GROUP):
            o_ref[:, (q0 + h) * HEAD_DIM:(q0 + h + 1) * HEAD_DIM] = o[h * blk:(h + 1) * blk, :].astype(o_ref.dtype)


def _swa_mask_table(blk, ctx_len):
    shape = (4, GQA_GROUP * blk, 3 * blk + ctx_len)
    cls = lax.broadcasted_iota(jnp.int32, shape, 0)
    qi = lax.broadcasted_iota(jnp.int32, shape, 1) % blk
    col = lax.broadcasted_iota(jnp.int32, shape, 2)
    lo = jnp.where(cls == 0, blk, 0)
    hi = jnp.where(cls == 3, 0, jnp.where(cls == 2, 2 * blk, 3 * blk))
    rel = col - blk - qi
    ok = ((col >= lo) & (col < hi) & (rel >= -WINDOW) & (rel <= WINDOW)) | (col >= 3 * blk)
    return jnp.where(ok, 0.0, NEG).astype(F32)


def _swa(qkv, sink, *, n_batch, seq_len, ctx_len, n_kv_heads, blk):
    ta = qkv.shape[0]
    n_heads = n_kv_heads * GQA_GROUP
    n_lat_blocks = seq_len // blk
    n_ctx_blocks = ctx_len // blk
    assert blk == WINDOW and seq_len % blk == 0 and ctx_len % blk == 0 and n_lat_blocks >= 2
    ctx_q0 = n_batch * n_lat_blocks
    ctx_kv0 = n_batch * seq_len // ctx_len
    kv_w = n_kv_heads * HEAD_DIM
    k_col, v_col = n_heads * HEAD_DIM // kv_w, n_heads * HEAD_DIM // kv_w + 1

    def q_index(b, n):
        return (jnp.where(n < n_lat_blocks, b * n_lat_blocks + n, ctx_q0 + b * n_ctx_blocks + n - n_lat_blocks), 0)

    def kv_spec(off, col):
        return pl.BlockSpec((blk, kv_w),
                            lambda b, n: (b * n_lat_blocks + jnp.clip(n + off, 0, n_lat_blocks - 1), col))

    def ctx_spec(col):
        return pl.BlockSpec((ctx_len, kv_w), lambda b, n: (ctx_kv0 + b, col))

    def mask_index(b, n):
        interior = jnp.where(n == 0, 0, jnp.where(n == n_lat_blocks - 1, 2, 1))
        return (jnp.where(n < n_lat_blocks, interior, 3), 0, 0)

    mask = _swa_mask_table(blk, ctx_len)
    kern = functools.partial(_swa_kernel, n_kv_heads=n_kv_heads)
    return pl.pallas_call(
        kern,
        out_shape=jax.ShapeDtypeStruct((ta, n_heads * HEAD_DIM), BF16),
        grid=(n_batch, n_lat_blocks + n_ctx_blocks),
        in_specs=[pl.BlockSpec(memory_space=pltpu.SMEM),
                  pl.BlockSpec((blk, n_heads * HEAD_DIM), q_index),
                  kv_spec(-1, k_col), kv_spec(0, k_col), kv_spec(1, k_col), ctx_spec(k_col),
                  kv_spec(-1, v_col), kv_spec(0, v_col), kv_spec(1, v_col), ctx_spec(v_col),
                  pl.BlockSpec((None,) + mask.shape[1:], mask_index)],
        out_specs=pl.BlockSpec((blk, n_heads * HEAD_DIM), q_index),
        compiler_params=_cparams(2),
        name="windowed_gqa",
    )(sink, *([qkv] * 9), mask)


NA_HEADS_PER_STEP = 4


def _na_kernel(q_ref, kvp_ref, kvc_ref, kvn_ref, kvx_ref, bias_ref, o_ref):
    n_local = bias_ref.shape[-1]
    gw = NA_HEADS_PER_STEP * HEAD_DIM
    kv_refs = (kvp_ref, kvc_ref, kvn_ref, kvx_ref)
    for h in range(NA_HEADS_PER_STEP):
        hs = slice(h * HEAD_DIM, (h + 1) * HEAD_DIM)
        vs = slice(gw + h * HEAD_DIM, gw + (h + 1) * HEAD_DIM)
        k = jnp.concatenate([ref[:, hs] for ref in kv_refs], axis=0)
        v = jnp.concatenate([ref[:, vs] for ref in kv_refs], axis=0)
        s = lax.dot_general(q_ref[:, hs], k, (((1,), (1,)), ((), ())), preferred_element_type=F32) * LOGIT_SCALE
        s = jnp.concatenate([s[:, :n_local] + bias_ref[h], s[:, n_local:]], axis=1)
        e, inv = _softmax_parts(s)
        o_ref[:, hs] = (jnp.dot(e.astype(BF16), v, preferred_element_type=F32) * inv).astype(o_ref.dtype)


def _na_bias_table(rpb, n_grid_rows):
    r_in, w = NA_ROWS_PER_BLOCK, GRID_W
    kh = min(NA_KH, n_grid_rows)
    n_blocks = n_grid_rows // r_in
    n_ro, n_co = 2 * NA_KH - 1, 2 * NA_KW - 1
    qc, kc = np.arange(w)[:, None], np.arange(w)[None, :]
    e_col = (kc - qc + NA_KW - 1 == np.arange(n_co)[:, None, None]).astype(np.float32)
    win_start = np.clip(qc - NA_KW // 2, 0, w - NA_KW)
    col_ok = (kc >= win_start) & (kc < win_start + NA_KW)
    ri, krj = np.arange(r_in)[:, None], np.arange(3 * r_in)[None, :]
    e_row = (krj - ri - r_in + NA_KH - 1 == np.arange(n_ro)[:, None, None]).astype(np.float32)
    row_ok = []
    for jb in (0, max(n_blocks // 2, 1) if n_blocks > 2 else 0, n_blocks - 1):
        r, kr = r_in * jb + ri, r_in * (jb - 1) + krj
        rs = np.clip(r - kh // 2, 0, n_grid_rows - kh)
        row_ok.append((kr >= rs) & (kr < rs + kh))
    ok = np.stack(row_ok)[:, :, None, :, None] & col_ok[None, None, :, None, :]
    hi = lax.Precision.HIGHEST
    tz = jnp.einsum('hab,bqc->haqc', rpb.astype(F32), e_col, precision=hi)
    full = jnp.einsum('haqc,aik->hiqkc', tz, e_row, precision=hi)
    tab = jnp.where(ok[None], full[:, None] * LOG2E, NEG)
    return tab.reshape(rpb.shape[0], 3, r_in * w, 3 * r_in * w)


def _na(qkv, bias, *, n_batch, seq_len, ctx_len, n_heads):
    tq = NA_ROWS_PER_BLOCK * GRID_W
    n_blocks = seq_len // tq
    assert ctx_len == tq and seq_len % tq == 0 and NA_KH == 2 * NA_ROWS_PER_BLOCK and n_blocks >= 2
    assert n_heads % NA_HEADS_PER_STEP == 0
    n_groups = n_heads // NA_HEADS_PER_STEP
    gw = NA_HEADS_PER_STEP * HEAD_DIM
    ctx0 = n_batch * n_blocks

    def kv_spec(off):
        return pl.BlockSpec((tq, 2 * gw),
                            lambda b, g, jb: (b * n_blocks + jnp.clip(jb + off, 0, n_blocks - 1), g))

    def bias_index(b, g, jb):
        return (g, jnp.where(jb == 0, 0, jnp.where(jb == n_blocks - 1, 2, 1)), 0, 0)

    return pl.pallas_call(
        _na_kernel,
        out_shape=jax.ShapeDtypeStruct((n_batch * seq_len, n_heads * HEAD_DIM), BF16),
        grid=(n_batch, n_groups, n_blocks),
        in_specs=[pl.BlockSpec((tq, gw), lambda b, g, jb: (b * n_blocks + jb, 2 * n_groups + g)),
                  kv_spec(-1), kv_spec(0), kv_spec(1),
                  pl.BlockSpec((ctx_len, 2 * gw), lambda b, g, jb: (ctx0 + b, g)),
                  pl.BlockSpec((NA_HEADS_PER_STEP, None) + bias.shape[2:], bias_index)],
        out_specs=pl.BlockSpec((tq, gw), lambda b, g, jb: (b * n_blocks + jb, g)),
        compiler_params=_cparams(3),
        name="neighbourhood_attn",
    )(*([qkv] * 5), bias)


def _round_up(n, m):
    return (n + m - 1) // m * m


def _tiles(d_ff):
    tf = 512
    return dict(tm=1024, tm_proj=512, tm_out=512, tf=tf, f_pad=_round_up(d_ff, tf), tn_mod=1024)


def _ffn_weights(w_in, w_out, f_pad):
    f = w_out.shape[0]
    pad = f_pad - f
    wg = jnp.pad(w_in[:, :f].astype(BF16), ((0, 0), (0, pad)))
    wu = jnp.pad(w_in[:, f:].astype(BF16), ((0, 0), (0, pad)))
    return wg, wu, jnp.pad(w_out.astype(BF16), ((0, pad), (0, 0)))


def _chunked(total, out_idx, out_col0=0, w_col0=0):
    return [(w_col0 + c, min(PROJ_CHUNK, total - c), out_idx, out_col0 + c, None)
            for c in range(0, total, PROJ_CHUNK)]


def _head_chunks(head_ops, w_col0, out_idx, out_col0):
    per = PROJ_CHUNK // HEAD_DIM
    return [(w_col0 + h0 * HEAD_DIM, len(head_ops[h0:h0 + per]) * HEAD_DIM, out_idx, out_col0 + h0 * HEAD_DIM,
             tuple(head_ops[h0:h0 + per])) for h0 in range(0, len(head_ops), per)]


def kernel(x, c, ctx, c_ctx, w_mod, b_mod, norm_g, ffn_w_in, ffn_w_out, ab_w_in, lru_conv_w, lru_conv_b,
           lru_w_a, lru_b_a, lru_w_x, lru_b_x, lru_lambda, attn_q_norm, attn_k_norm, attn_sink, ab_w_out,
           na_w_in, na_q_norm, na_k_norm, na_rpb, na_w_out):
    n_batch, seq_len, d = x.shape
    ctx_len = ctx.shape[1]
    depth = w_mod.shape[0]
    d_ff = ffn_w_out.shape[2]
    lru_w = lru_conv_w.shape[2]
    n_kv = attn_sink.shape[1] // GQA_GROUP
    n_att = n_kv * GQA_GROUP
    na_heads = na_rpb.shape[1]
    assert n_batch < MOD_ROWS and depth == 2
    t = _tiles(d_ff)
    tl, tc = n_batch * seq_len, n_batch * ctx_len
    geom = dict(rows_per_batch=seq_len, n_batch=n_batch)

    x_all = jnp.concatenate([x.reshape(tl, d), ctx.reshape(tc, d)], axis=0)
    cc = jnp.zeros((MOD_ROWS, d), F32).at[:n_batch].set(c).at[n_batch].set(c_ctx)
    mod = _mod_table(cc, w_mod, b_mod, t["tn_mod"])

    def ffn(x_in, n_rows, layer, which):
        wg, wu, wo = _ffn_weights(ffn_w_in[layer, which], ffn_w_out[layer, which], t["f_pad"])
        return _ffn(x_in, n_rows, mod, layer, 6 * which, norm_g[layer, 2 * which], wg, wu, wo,
                    tm=t["tm"], tf=t["tf"], **geom)

    cos_t, sin_t = _rope_tables(seq_len)
    proj_geom = dict(tm=t["tm_proj"], n_lat_rows=tl, **geom)

    x_all = ffn(x_all, tl + tc, 0, 0)
    gains = jnp.stack([attn_q_norm[0], attn_k_norm[0]])
    head_ops = [(0, True)] * n_att + [(1, True)] * n_kv + [None] * n_kv
    plan = _chunked(2 * lru_w, 0) + _head_chunks(head_ops, 2 * lru_w, 1, 0)
    p, qkv = _inproj(x_all, mod, 0, norm_g[0, 1], ab_w_in[0].astype(BF16), gains, cos_t, sin_t, plan,
                     ((2 * lru_w, F32), (len(head_ops) * HEAD_DIM, BF16)), **proj_geom)
    lru_args = lambda dr: (lru_conv_w[0], lru_conv_b[0], lru_w_a[0, dr].astype(BF16), lru_b_a[0, dr],
                           lru_w_x[0, dr].astype(BF16), lru_b_x[0, dr], lru_lambda[0, dr])
    scan_geom = dict(n_batch=n_batch, seq_len=seq_len, ctx_len=ctx_len, tt=ctx_len)
    h_fwd = _lru_pass(p, *lru_args(0), None, reverse=False, **scan_geom)
    lru = _lru_pass(p, *lru_args(1), h_fwd, reverse=True, **scan_geom)
    att = _swa(qkv, attn_sink[0], n_batch=n_batch, seq_len=seq_len, ctx_len=ctx_len, n_kv_heads=n_kv,
               blk=WINDOW)
    x_all = _outproj(lru, att, 0, ab_w_out[0].astype(BF16), x_all, tl + tc, mod, 0, tm=t["tm_out"], **geom)
    x_all = ffn(x_all, tl + tc, 0, 1)

    x_all = ffn(x_all, tl + tc, 1, 0)
    gains = jnp.stack([na_q_norm[0], na_k_norm[0]])
    na_d = na_heads * HEAD_DIM
    gw = NA_HEADS_PER_STEP * HEAD_DIM
    plan = []
    for g0 in range(0, na_d, gw):
        plan += _head_chunks([(0, False)] * NA_HEADS_PER_STEP, g0, 0, 2 * na_d + g0)
        plan += _head_chunks([(1, False)] * NA_HEADS_PER_STEP, na_d + g0, 0, 2 * g0)
        plan += _head_chunks([None] * NA_HEADS_PER_STEP, 2 * na_d + g0, 0, 2 * g0 + gw)
    (qkv,) = _inproj(x_all, mod, 1, norm_g[1, 1], na_w_in[0].astype(BF16), gains, cos_t, sin_t, plan,
                     ((3 * na_d, BF16),), **proj_geom)
    o = _na(qkv, _na_bias_table(na_rpb[0], seq_len // GRID_W), n_batch=n_batch, seq_len=seq_len, ctx_len=ctx_len,
            n_heads=na_heads)
    x_lat = _outproj(o, o, 1, na_w_out[0].astype(BF16), x_all, tl, mod, 1, tm=t["tm_out"], **geom)
    x_lat = ffn(x_lat, tl, 1, 1)
    return x_lat.reshape(n_batch, seq_len, d)
```

```python
import functools

import numpy as np
import jax
import jax.numpy as jnp
from jax import lax
from jax.experimental import pallas as pl
from jax.experimental.pallas import tpu as pltpu

HEAD_DIM = 128
EPS = 1e-6
N_MOD = 9
GRID_W = 64
CONV_W = 4
LRU_C = 8.0
WINDOW = 128
ROPE_THETA = 10000.0
NA_KH = 8
NA_KW = 16
ATTN_SCALE = HEAD_DIM ** -0.5
GQA_GROUP = 4

LANES = 128
SUBLANES = 8
VMEM_LIMIT_BYTES = 56 * 1024 * 1024
MOD_ROWS = SUBLANES

NEG = -1e30
NA_ROWS_PER_BLOCK = 4

BF16 = jnp.bfloat16
F32 = jnp.float32


def _cparams(n_axes):
    return pltpu.CompilerParams(dimension_semantics=("arbitrary",) * n_axes,
                                vmem_limit_bytes=VMEM_LIMIT_BYTES)


def _sigmoid(x):
    return 1.0 / (1.0 + jnp.exp(-x))


def _sigmoid_tanh(x):
    return 0.5 * jnp.tanh(0.5 * x) + 0.5


NORM_CHUNK_ROWS = 32


def _norm_modulate_into(h_scr, x_ref, g, shift, scale):
    gain = g * (1.0 + scale)

    def chunk(c, carry):
        r0 = pl.multiple_of(c * NORM_CHUNK_ROWS, NORM_CHUNK_ROWS)
        x = x_ref[pl.ds(r0, NORM_CHUNK_ROWS), :]
        inv = lax.rsqrt(jnp.mean(x * x, axis=-1, keepdims=True) + EPS)
        h_scr[pl.ds(r0, NORM_CHUNK_ROWS), :] = ((x * inv) * gain + shift).astype(h_scr.dtype)
        return carry

    lax.fori_loop(0, x_ref.shape[0] // NORM_CHUNK_ROWS, chunk, 0, unroll=2)


def _mod_kernel(c_ref, w_ref, b_ref, o_ref):
    c = c_ref[...]
    s = c * _sigmoid(c)
    o_ref[...] = jnp.dot(s.astype(BF16), w_ref[...].astype(BF16),
                         preferred_element_type=F32) + b_ref[...]


def _mod_table(cc, w_mod, b_mod, tn):
    depth, d, n = w_mod.shape
    return pl.pallas_call(
        _mod_kernel,
        out_shape=jax.ShapeDtypeStruct((depth, MOD_ROWS, n), F32),
        grid=(depth, n // tn),
        in_specs=[pl.BlockSpec((MOD_ROWS, d), lambda l, j: (0, 0)),
                  pl.BlockSpec((None, d, tn), lambda l, j: (l, 0, j)),
                  pl.BlockSpec((None, 1, tn), lambda l, j: (l, 0, j))],
        out_specs=pl.BlockSpec((None, MOD_ROWS, tn), lambda l, j: (l, 0, j)),
        compiler_params=_cparams(2),
        name="mod_table",
    )(cc, w_mod, b_mod.reshape(depth, 1, n))


def _mod_spec(d, layer, k, n_grid_axes):
    if n_grid_axes == 1:
        return pl.BlockSpec((None, MOD_ROWS, d), lambda i: (layer, 0, k))
    return pl.BlockSpec((None, MOD_ROWS, d), lambda i, j: (layer, 0, k))


def _mod_row(tile_idx, tm, rows_per_batch, n_batch):
    return jnp.minimum(tile_idx * tm // rows_per_batch, n_batch)


def _ffn_kernel(x_ref, g_ref, sh_ref, sc_ref, gt_ref, wg_ref, wu_ref, wo_ref, o_ref, h_scr,
                *, tm, rows_per_batch, n_batch, last_width):
    i, j = pl.program_id(0), pl.program_id(1)
    last = pl.num_programs(1) - 1
    r = _mod_row(i, tm, rows_per_batch, n_batch)

    @pl.when(j == 0)
    def _():
        _norm_modulate_into(h_scr, x_ref, g_ref[...], sh_ref[pl.ds(r, 1), :], sc_ref[pl.ds(r, 1), :])
        o_ref[...] = jnp.zeros_like(o_ref)

    def hidden_block(width):
        h = h_scr[...]
        a = jnp.dot(h, wg_ref[:, :width], preferred_element_type=F32)
        u = jnp.dot(h, wu_ref[:, :width], preferred_element_type=F32)
        act = (a * _sigmoid(a)) * u
        o_ref[...] += jnp.dot(act.astype(BF16), wo_ref[:width, :], preferred_element_type=F32)

    if last_width == wg_ref.shape[1]:
        hidden_block(last_width)
    else:
        pl.when(j < last)(lambda: hidden_block(wg_ref.shape[1]))
        pl.when(j == last)(lambda: hidden_block(last_width))

    @pl.when(j == last)
    def _():
        o_ref[...] = x_ref[...] + 0.5 * gt_ref[pl.ds(r, 1), :] * o_ref[...]


def _ffn(x_all, n_rows, mod, layer, k0, g, wg, wu, wo, *, tm, tf, rows_per_batch, n_batch):
    d = x_all.shape[1]
    d_ff = wo.shape[0]
    nf = pl.cdiv(d_ff, tf)
    last_width = d_ff - (nf - 1) * tf
    assert last_width % LANES == 0
    kern = functools.partial(_ffn_kernel, tm=tm, rows_per_batch=rows_per_batch, n_batch=n_batch,
                             last_width=last_width)
    return pl.pallas_call(
        kern,
        out_shape=jax.ShapeDtypeStruct((n_rows, d), F32),
        grid=(n_rows // tm, nf),
        in_specs=[pl.BlockSpec((tm, d), lambda i, j: (i, 0)),
                  pl.BlockSpec((1, d), lambda i, j: (0, 0)),
                  _mod_spec(d, layer, k0, 2), _mod_spec(d, layer, k0 + 1, 2),
                  _mod_spec(d, layer, k0 + 2, 2),
                  pl.BlockSpec((d, tf), lambda i, j: (0, j)),
                  pl.BlockSpec((d, tf), lambda i, j: (0, j)),
                  pl.BlockSpec((tf, d), lambda i, j: (j, 0))],
        out_specs=pl.BlockSpec((tm, d), lambda i, j: (i, 0)),
        scratch_shapes=[pltpu.VMEM((tm, d), BF16)],
        compiler_params=_cparams(2),
        name="half_ffn",
    )(x_all, g.reshape(1, d), mod, mod, mod, wg, wu, wo)


PROJ_CHUNK = 4 * HEAD_DIM


def _swap_halves_32(y):
    lane = lax.broadcasted_iota(jnp.int32, y.shape, 1)
    return jnp.where((lane & 32) == 0, pltpu.roll(y, LANES - 32, 1), pltpu.roll(y, 32, 1))


def _inproj_kernel(x_ref, g_ref, sh_ref, sc_ref, w_ref, gains_ref, cos_ref, sin_ref, *rest,
                   plan, tm, rows_per_batch, n_batch, n_lat_tiles):
    out_refs, h_scr = rest[:-1], rest[-1]
    i = pl.program_id(0)
    r = _mod_row(i, tm, rows_per_batch, n_batch)
    is_lat = i < n_lat_tiles
    _norm_modulate_into(h_scr, x_ref, g_ref[...], sh_ref[pl.ds(r, 1), :], sc_ref[pl.ds(r, 1), :])
    for w_col, width, out_idx, out_col, ops in plan:
        y = jnp.dot(h_scr[...], w_ref[:, w_col:w_col + width], preferred_element_type=F32)
        o_ref = out_refs[out_idx]
        if ops is None:
            o_ref[:, out_col:out_col + width] = y.astype(o_ref.dtype)
            continue
        for hh, op in enumerate(ops):
            yh = y[:, hh * HEAD_DIM:(hh + 1) * HEAD_DIM]
            if op is not None:
                gain_row, rope = op
                yh = yh * lax.rsqrt(jnp.mean(yh * yh, axis=-1, keepdims=True) + EPS)
                yh = yh * gains_ref[gain_row:gain_row + 1, :]
                if rope:
                    yh = jnp.where(is_lat, yh * cos_ref[...] + _swap_halves_32(yh) * sin_ref[...], yh)
            o_ref[:, out_col + hh * HEAD_DIM:out_col + (hh + 1) * HEAD_DIM] = yh.astype(o_ref.dtype)


def _inproj(x_all, mod, layer, g, w, gains, cos_t, sin_t, plan, outs, *, tm, n_lat_rows, rows_per_batch, n_batch):
    ta, d = x_all.shape
    n_pos_tiles = cos_t.shape[0] // tm
    kern = functools.partial(_inproj_kernel, plan=tuple(plan), tm=tm, rows_per_batch=rows_per_batch,
                             n_batch=n_batch, n_lat_tiles=n_lat_rows // tm)
    return pl.pallas_call(
        kern,
        out_shape=[jax.ShapeDtypeStruct((ta, n), dt) for n, dt in outs],
        grid=(ta // tm,),
        in_specs=[pl.BlockSpec((tm, d), lambda i: (i, 0)),
                  pl.BlockSpec((1, d), lambda i: (0, 0)),
                  _mod_spec(d, layer, 3, 1), _mod_spec(d, layer, 4, 1),
                  pl.BlockSpec(w.shape, lambda i: (0, 0), pipeline_mode=pl.Buffered(1)),
                  pl.BlockSpec(gains.shape, lambda i: (0, 0)),
                  pl.BlockSpec((tm, HEAD_DIM), lambda i: (i % n_pos_tiles, 0)),
                  pl.BlockSpec((tm, HEAD_DIM), lambda i: (i % n_pos_tiles, 0))],
        out_specs=[pl.BlockSpec((tm, n), lambda i: (i, 0)) for n, _ in outs],
        scratch_shapes=[pltpu.VMEM((tm, d), BF16)],
        compiler_params=_cparams(1),
        name="mixer_inproj",
    )(x_all, g.reshape(1, d), mod, mod, w, gains, cos_t, sin_t)


def _outproj_kernel(a1_ref, a2_ref, w1_ref, w2_ref, x_ref, gt_ref, o_ref, *, tm, rows_per_batch, n_batch):
    r = _mod_row(pl.program_id(0), tm, rows_per_batch, n_batch)
    y = (jnp.dot(a1_ref[...], w1_ref[...], preferred_element_type=F32)
         + jnp.dot(a2_ref[...], w2_ref[...], preferred_element_type=F32))
    o_ref[...] = x_ref[...] + gt_ref[pl.ds(r, 1), :] * y


def _outproj(a1, a2, a2_col_block, w, x_all, n_rows, mod, layer, *, tm, rows_per_batch, n_batch):
    d = x_all.shape[1]
    kh = w.shape[0] // 2
    kern = functools.partial(_outproj_kernel, tm=tm, rows_per_batch=rows_per_batch, n_batch=n_batch)
    return pl.pallas_call(
        kern,
        out_shape=jax.ShapeDtypeStruct((n_rows, d), F32),
        grid=(n_rows // tm,),
        in_specs=[pl.BlockSpec((tm, kh), lambda i: (i, 0)),
                  pl.BlockSpec((tm, kh), lambda i: (i, a2_col_block)),
                  pl.BlockSpec((kh, d), lambda i: (0, 0)),
                  pl.BlockSpec((kh, d), lambda i: (1, 0)),
                  pl.BlockSpec((tm, d), lambda i: (i, 0)),
                  _mod_spec(d, layer, 5, 1)],
        out_specs=pl.BlockSpec((tm, d), lambda i: (i, 0)),
        compiler_params=_cparams(1),
        name="mixer_outproj",
    )(a1, a2, w, w, x_all, mod)


def _rope_tables(seq_len):
    half = HEAD_DIM // 2
    nf = half // 2
    inv = ROPE_THETA ** (-jnp.arange(nf, dtype=F32) / nf)
    pos = jnp.arange(seq_len, dtype=jnp.int32)
    ang_r = (pos // GRID_W).astype(F32)[:, None] * inv[None, :]
    ang_c = (pos % GRID_W).astype(F32)[:, None] * inv[None, :]
    cr, sr, cc, sc = jnp.cos(ang_r), jnp.sin(ang_r), jnp.cos(ang_c), jnp.sin(ang_c)
    return (jnp.concatenate([cr, cr, cc, cc], axis=-1),
            jnp.concatenate([-sr, sr, -sc, sc], axis=-1))


def _gelu_tanh(x):
    return x * (0.5 * (1.0 + jnp.tanh(np.float32(np.sqrt(2.0 / np.pi)) * (x + 0.044715 * (x * x * x)))))


def _lru_kernel(*refs, tt, reverse, final, n_lat_chunks):
    if final:
        (xp_ref, xc_ref, xn_ref, cw_ref, cb_ref, wa_ref, ba_ref, wx_ref, bx_ref, lam_ref,
         hf_ref, gl_ref, o_ref, xs_scr, a_scr, b_scr, carry_scr) = refs
    else:
        (xp_ref, xc_ref, xn_ref, cw_ref, cb_ref, wa_ref, ba_ref, wx_ref, bx_ref, lam_ref,
         o_ref, xs_scr, a_scr, b_scr, carry_scr) = refs
    s = pl.program_id(1)
    is_ctx = s == 0
    j = (n_lat_chunks - s) if reverse else (s - 1)
    has_prev = jnp.logical_and(jnp.logical_not(is_ctx), j > 0)
    has_next = jnp.logical_and(jnp.logical_not(is_ctx), j < n_lat_chunks - 1)

    @pl.when(is_ctx)
    def _():
        carry_scr[...] = jnp.zeros_like(carry_scr)

    halo = SUBLANES
    xs_scr[0:halo, :] = jnp.where(has_prev, xp_ref[tt - halo:tt, :], 0.0)
    xs_scr[halo:halo + tt, :] = xc_ref[...]
    xs_scr[halo + tt:2 * halo + tt, :] = jnp.where(has_next, xn_ref[0:halo, :], 0.0)

    width = xc_ref.shape[1]
    left = CONV_W // 2
    z = -lam_ref[...]
    softplus = jnp.maximum(z, 0.0) + jnp.log1p(jnp.exp(-jnp.abs(z)))
    for n in range(width // HEAD_DIM):
        cs = slice(n * HEAD_DIM, (n + 1) * HEAD_DIM)
        u = cb_ref[:, cs]
        for k in range(CONV_W):
            u = u + xs_scr[halo - left + k:halo - left + k + tt, cs] * cw_ref[k:k + 1, cs]
        ub = u.astype(BF16)
        gate_r = jnp.dot(ub, wa_ref[n], preferred_element_type=F32) + ba_ref[:, cs]
        gate_i = jnp.dot(ub, wx_ref[n], preferred_element_type=F32) + bx_ref[:, cs]
        log_a = (-LRU_C) * _sigmoid_tanh(gate_r) * softplus[:, cs]
        a = jnp.exp(log_a)
        a_scr[:, cs] = a
        b_scr[:, cs] = jnp.sqrt(-jnp.tanh(log_a) * (1.0 + a * a)) * (_sigmoid_tanh(gate_i) * u)

    n_groups = tt // SUBLANES
    row = lax.broadcasted_iota(jnp.int32, (SUBLANES, width), 0)

    def group(gi, carry):
        g = (n_groups - 1 - gi) if reverse else gi
        r0 = pl.multiple_of(g * SUBLANES, SUBLANES)
        a = a_scr[pl.ds(r0, SUBLANES), :]
        b = b_scr[pl.ds(r0, SUBLANES), :]
        for k in (1, 2, 4):
            if reverse:
                keep = row < SUBLANES - k
                shift = SUBLANES - k
            else:
                keep = row >= k
                shift = k
            a_sh = jnp.where(keep, pltpu.roll(a, shift, 0), 1.0)
            b_sh = jnp.where(keep, pltpu.roll(b, shift, 0), 0.0)
            b = a * b_sh + b
            a = a * a_sh
        h = b + a * carry
        b_scr[pl.ds(r0, SUBLANES), :] = h
        last = h[0:1, :] if reverse else h[SUBLANES - 1:SUBLANES, :]
        return jnp.broadcast_to(last, (SUBLANES, width))

    carry_scr[...] = lax.fori_loop(0, n_groups, group, carry_scr[...])

    if final:
        o_ref[...] = ((hf_ref[...] + b_scr[...]) * _gelu_tanh(gl_ref[...])).astype(o_ref.dtype)
    else:
        o_ref[...] = b_scr[...]


def _lru_pass(p, conv_w, conv_b, w_a, b_a, w_x, b_x, lam, hf, *, reverse, n_batch, seq_len, ctx_len, tt):
    ta = p.shape[0]
    width = conv_w.shape[1]
    assert ctx_len == tt and seq_len % tt == 0
    n_lat_chunks = seq_len // tt
    ctx_block0 = n_batch * n_lat_chunks
    final = hf is not None

    def chunk_index(b, s, off):
        j = (n_lat_chunks - s) if reverse else (s - 1)
        j = jnp.clip(j + off, 0, n_lat_chunks - 1)
        return jnp.where(s == 0, ctx_block0 + b, b * n_lat_chunks + j)

    def xspec(off):
        return pl.BlockSpec((tt, width), lambda b, s: (chunk_index(b, s, off), 0))

    row_spec = pl.BlockSpec((1, width), lambda b, s: (0, 0))
    w_spec = pl.BlockSpec(w_a.shape, lambda b, s: (0, 0, 0))
    in_specs = [xspec(-1), xspec(0), xspec(1),
                pl.BlockSpec((CONV_W, width), lambda b, s: (0, 0)), row_spec,
                w_spec, row_spec, w_spec, row_spec, row_spec]
    args = [p, p, p, conv_w, conv_b.reshape(1, width), w_a, b_a.reshape(1, width),
            w_x, b_x.reshape(1, width), lam.reshape(1, width)]
    if final:
        in_specs += [xspec(0), pl.BlockSpec((tt, width), lambda b, s: (chunk_index(b, s, 0), 1))]
        args += [hf, p]
    kern = functools.partial(_lru_kernel, tt=tt, reverse=reverse, final=final, n_lat_chunks=n_lat_chunks)
    return pl.pallas_call(
        kern,
        out_shape=jax.ShapeDtypeStruct((ta, width), BF16 if final else F32),
        grid=(n_batch, n_lat_chunks + 1),
        in_specs=in_specs,
        out_specs=xspec(0),
        scratch_shapes=[pltpu.VMEM((tt + 2 * SUBLANES, width), F32), pltpu.VMEM((tt, width), F32),
                        pltpu.VMEM((tt, width), F32), pltpu.VMEM((SUBLANES, width), F32)],
        compiler_params=_cparams(2),
        name="rglru_rev" if reverse else "rglru_fwd",
    )(*args)


LOG2E = float(np.log2(np.e))
LOGIT_SCALE = ATTN_SCALE * LOG2E


def _softmax_parts(s2, extra_logit2=None):
    m = jnp.max(s2, axis=-1, keepdims=True)
    if extra_logit2 is not None:
        m = jnp.maximum(m, extra_logit2)
    e = jnp.exp2(s2 - m)
    denom = jnp.sum(e, axis=-1, keepdims=True)
    if extra_logit2 is not None:
        denom = denom + jnp.exp2(extra_logit2 - m)
    return e, 1.0 / denom


def _swa_kernel(sink_ref, q_ref, kp_ref, kc_ref, kn_ref, kx_ref, vp_ref, vc_ref, vn_ref, vx_ref, mask_ref, o_ref,
                *, n_kv_heads):
    blk = q_ref.shape[0]
    head = lax.broadcasted_iota(jnp.int32, (GQA_GROUP * blk, 1), 0) // blk
    logits = []
    for kh in range(n_kv_heads):
        ks = slice(kh * HEAD_DIM, (kh + 1) * HEAD_DIM)
        q0 = kh * GQA_GROUP
        q = jnp.concatenate([q_ref[:, (q0 + h) * HEAD_DIM:(q0 + h + 1) * HEAD_DIM] for h in range(GQA_GROUP)],
                            axis=0)
        k = jnp.concatenate([kp_ref[:, ks], kc_ref[:, ks], kn_ref[:, ks], kx_ref[:, ks]], axis=0)
        logits.append(lax.dot_general(q, k, (((1,), (1,)), ((), ())), preferred_element_type=F32))
    for kh in range(n_kv_heads):
        ks = slice(kh * HEAD_DIM, (kh + 1) * HEAD_DIM)
        q0 = kh * GQA_GROUP
        v = jnp.concatenate([vp_ref[:, ks], vc_ref[:, ks], vn_ref[:, ks], vx_ref[:, ks]], axis=0)
        s = logits[kh] * LOGIT_SCALE + mask_ref[...]
        sink = jnp.zeros((GQA_GROUP * blk, 1), F32)
        for h in range(GQA_GROUP):
            sink = jnp.where(head == h, sink_ref[q0 + h] * LOG2E, sink)
        e, inv = _softmax_parts(s, sink)
        o = jnp.dot(e.astype(BF16), v, preferred_element_type=F32) * inv
        for h in range(GQA_GROUP):
            o_ref[:, (q0 + h) * HEAD_DIM:(q0 + h + 1) * HEAD_DIM] = o[h * blk:(h + 1) * blk, :].astype(o_ref.dtype)


def _swa_mask_table(blk, ctx_len):
    shape = (4, GQA_GROUP * blk, 3 * blk + ctx_len)
    cls = lax.broadcasted_iota(jnp.int32, shape, 0)
    qi = lax.broadcasted_iota(jnp.int32, shape, 1) % blk
    col = lax.broadcasted_iota(jnp.int32, shape, 2)
    lo = jnp.where(cls == 0, blk, 0)
    hi = jnp.where(cls == 3, 0, jnp.where(cls == 2, 2 * blk, 3 * blk))
    rel = col - blk - qi
    ok = ((col >= lo) & (col < hi) & (rel >= -WINDOW) & (rel <= WINDOW)) | (col >= 3 * blk)
    return jnp.where(ok, 0.0, NEG).astype(F32)


def _swa(qkv, sink, *, n_batch, seq_len, ctx_len, n_kv_heads, blk):
    ta = qkv.shape[0]
    n_heads = n_kv_heads * GQA_GROUP
    n_lat_blocks = seq_len // blk
    n_ctx_blocks = ctx_len // blk
    assert blk == WINDOW and seq_len % blk == 0 and ctx_len % blk == 0 and n_lat_blocks >= 2
    ctx_q0 = n_batch * n_lat_blocks
    ctx_kv0 = n_batch * seq_len // ctx_len
    kv_w = n_kv_heads * HEAD_DIM
    k_col, v_col = n_heads * HEAD_DIM // kv_w, n_heads * HEAD_DIM // kv_w + 1

    def q_index(b, n):
        return (jnp.where(n < n_lat_blocks, b * n_lat_blocks + n, ctx_q0 + b * n_ctx_blocks + n - n_lat_blocks), 0)

    def kv_spec(off, col):
        return pl.BlockSpec((blk, kv_w),
                            lambda b, n: (b * n_lat_blocks + jnp.clip(n + off, 0, n_lat_blocks - 1), col))

    def ctx_spec(col):
        return pl.BlockSpec((ctx_len, kv_w), lambda b, n: (ctx_kv0 + b, col))

    def mask_index(b, n):
        interior = jnp.where(n == 0, 0, jnp.where(n == n_lat_blocks - 1, 2, 1))
        return (jnp.where(n < n_lat_blocks, interior, 3), 0, 0)

    mask = _swa_mask_table(blk, ctx_len)
    kern = functools.partial(_swa_kernel, n_kv_heads=n_kv_heads)
    return pl.pallas_call(
        kern,
        out_shape=jax.ShapeDtypeStruct((ta, n_heads * HEAD_DIM), BF16),
        grid=(n_batch, n_lat_blocks + n_ctx_blocks),
        in_specs=[pl.BlockSpec(memory_space=pltpu.SMEM),
                  pl.BlockSpec((blk, n_heads * HEAD_DIM), q_index),
                  kv_spec(-1, k_col), kv_spec(0, k_col), kv_spec(1, k_col), ctx_spec(k_col),
                  kv_spec(-1, v_col), kv_spec(0, v_col), kv_spec(1, v_col), ctx_spec(v_col),
                  pl.BlockSpec((None,) + mask.shape[1:], mask_index)],
        out_specs=pl.BlockSpec((blk, n_heads * HEAD_DIM), q_index),
        compiler_params=_cparams(2),
        name="windowed_gqa",
    )(sink, *([qkv] * 9), mask)


NA_HEADS_PER_STEP = 8


def _na_kernel(q_ref, kvp_ref, kvc_ref, kvn_ref, kvx_ref, bias_ref, o_ref):
    n_local = bias_ref.shape[-1]
    gw = NA_HEADS_PER_STEP * HEAD_DIM
    kv_refs = (kvp_ref, kvc_ref, kvn_ref, kvx_ref)
    heads = [slice(h * HEAD_DIM, (h + 1) * HEAD_DIM) for h in range(NA_HEADS_PER_STEP)]
    logits = []
    for hs in heads:
        k = jnp.concatenate([ref[:, hs] for ref in kv_refs], axis=0)
        logits.append(lax.dot_general(q_ref[:, hs], k, (((1,), (1,)), ((), ())), preferred_element_type=F32))
    weights = []
    for h, hs in enumerate(heads):
        s = logits[h] * LOGIT_SCALE
        s = jnp.concatenate([s[:, :n_local] + bias_ref[h], s[:, n_local:]], axis=1)
        e, inv = _softmax_parts(s)
        weights.append((e.astype(BF16), inv))
    for (e, inv), hs in zip(weights, heads):
        vs = slice(gw + hs.start, gw + hs.stop)
        v = jnp.concatenate([ref[:, vs] for ref in kv_refs], axis=0)
        o_ref[:, hs] = (jnp.dot(e, v, preferred_element_type=F32) * inv).astype(o_ref.dtype)


def _na_bias_table(rpb, n_grid_rows):
    r_in, w = NA_ROWS_PER_BLOCK, GRID_W
    kh = min(NA_KH, n_grid_rows)
    n_blocks = n_grid_rows // r_in
    n_ro, n_co = 2 * NA_KH - 1, 2 * NA_KW - 1
    qc, kc = np.arange(w)[:, None], np.arange(w)[None, :]
    e_col = (kc - qc + NA_KW - 1 == np.arange(n_co)[:, None, None]).astype(np.float32)
    win_start = np.clip(qc - NA_KW // 2, 0, w - NA_KW)
    col_ok = (kc >= win_start) & (kc < win_start + NA_KW)
    ri, krj = np.arange(r_in)[:, None], np.arange(3 * r_in)[None, :]
    e_row = (krj - ri - r_in + NA_KH - 1 == np.arange(n_ro)[:, None, None]).astype(np.float32)
    row_ok = []
    for jb in (0, max(n_blocks // 2, 1) if n_blocks > 2 else 0, n_blocks - 1):
        r, kr = r_in * jb + ri, r_in * (jb - 1) + krj
        rs = np.clip(r - kh // 2, 0, n_grid_rows - kh)
        row_ok.append((kr >= rs) & (kr < rs + kh))
    ok = np.stack(row_ok)[:, :, None, :, None] & col_ok[None, None, :, None, :]
    hi = lax.Precision.HIGHEST
    tz = jnp.einsum('hab,bqc->haqc', rpb.astype(F32), e_col, precision=hi)
    full = jnp.einsum('haqc,aik->hiqkc', tz, e_row, precision=hi)
    tab = jnp.where(ok[None], full[:, None] * LOG2E, NEG)
    return tab.reshape(rpb.shape[0], 3, r_in * w, 3 * r_in * w)


def _na(qkv, bias, *, n_batch, seq_len, ctx_len, n_heads):
    tq = NA_ROWS_PER_BLOCK * GRID_W
    n_blocks = seq_len // tq
    assert ctx_len == tq and seq_len % tq == 0 and NA_KH == 2 * NA_ROWS_PER_BLOCK and n_blocks >= 2
    assert n_heads % NA_HEADS_PER_STEP == 0
    n_groups = n_heads // NA_HEADS_PER_STEP
    gw = NA_HEADS_PER_STEP * HEAD_DIM
    ctx0 = n_batch * n_blocks

    def kv_spec(off):
        return pl.BlockSpec((tq, 2 * gw),
                            lambda b, g, jb: (b * n_blocks + jnp.clip(jb + off, 0, n_blocks - 1), g))

    def bias_index(b, g, jb):
        return (g, jnp.where(jb == 0, 0, jnp.where(jb == n_blocks - 1, 2, 1)), 0, 0)

    return pl.pallas_call(
        _na_kernel,
        out_shape=jax.ShapeDtypeStruct((n_batch * seq_len, n_heads * HEAD_DIM), BF16),
        grid=(n_batch, n_groups, n_blocks),
        in_specs=[pl.BlockSpec((tq, gw), lambda b, g, jb: (b * n_blocks + jb, 2 * n_groups + g)),
                  kv_spec(-1), kv_spec(0), kv_spec(1),
                  pl.BlockSpec((ctx_len, 2 * gw), lambda b, g, jb: (ctx0 + b, g)),
                  pl.BlockSpec((NA_HEADS_PER_STEP, None) + bias.shape[2:], bias_index)],
        out_specs=pl.BlockSpec((tq, gw), lambda b, g, jb: (b * n_blocks + jb, g)),
        compiler_params=_cparams(3),
        name="neighbourhood_attn",
    )(*([qkv] * 5), bias)


def _tiles():
    return dict(tm=1024, tm_proj=512, tm_out=512, tf=512, tn_mod=1024)


def _ffn_weights(w_in, w_out):
    f = w_out.shape[0]
    return w_in[:, :f].astype(BF16), w_in[:, f:].astype(BF16), w_out.astype(BF16)


def _chunked(total, out_idx, out_col0=0, w_col0=0):
    return [(w_col0 + c, min(PROJ_CHUNK, total - c), out_idx, out_col0 + c, None)
            for c in range(0, total, PROJ_CHUNK)]


def _head_chunks(head_ops, w_col0, out_idx, out_col0):
    per = PROJ_CHUNK // HEAD_DIM
    return [(w_col0 + h0 * HEAD_DIM, len(head_ops[h0:h0 + per]) * HEAD_DIM, out_idx, out_col0 + h0 * HEAD_DIM,
             tuple(head_ops[h0:h0 + per])) for h0 in range(0, len(head_ops), per)]


def kernel(x, c, ctx, c_ctx, w_mod, b_mod, norm_g, ffn_w_in, ffn_w_out, ab_w_in, lru_conv_w, lru_conv_b,
           lru_w_a, lru_b_a, lru_w_x, lru_b_x, lru_lambda, attn_q_norm, attn_k_norm, attn_sink, ab_w_out,
           na_w_in, na_q_norm, na_k_norm, na_rpb, na_w_out):
    n_batch, seq_len, d = x.shape
    ctx_len = ctx.shape[1]
    depth = w_mod.shape[0]
    lru_w = lru_conv_w.shape[2]
    n_kv = attn_sink.shape[1] // GQA_GROUP
    n_att = n_kv * GQA_GROUP
    na_heads = na_rpb.shape[1]
    assert n_batch < MOD_ROWS and depth == 2
    t = _tiles()
    tl, tc = n_batch * seq_len, n_batch * ctx_len
    geom = dict(rows_per_batch=seq_len, n_batch=n_batch)

    x_all = jnp.concatenate([x.reshape(tl, d), ctx.reshape(tc, d)], axis=0)
    cc = jnp.zeros((MOD_ROWS, d), F32).at[:n_batch].set(c).at[n_batch].set(c_ctx)
    mod = _mod_table(cc, w_mod, b_mod, t["tn_mod"])

    def ffn(x_in, n_rows, layer, which):
        wg, wu, wo = _ffn_weights(ffn_w_in[layer, which], ffn_w_out[layer, which])
        return _ffn(x_in, n_rows, mod, layer, 6 * which, norm_g[layer, 2 * which], wg, wu, wo,
                    tm=t["tm"], tf=t["tf"], **geom)

    cos_t, sin_t = _rope_tables(seq_len)
    proj_geom = dict(tm=t["tm_proj"], n_lat_rows=tl, **geom)

    x_all = ffn(x_all, tl + tc, 0, 0)
    gains = jnp.stack([attn_q_norm[0], attn_k_norm[0]])
    head_ops = [(0, True)] * n_att + [(1, True)] * n_kv + [None] * n_kv
    plan = _chunked(2 * lru_w, 0) + _head_chunks(head_ops, 2 * lru_w, 1, 0)
    p, qkv = _inproj(x_all, mod, 0, norm_g[0, 1], ab_w_in[0].astype(BF16), gains, cos_t, sin_t, plan,
                     ((2 * lru_w, F32), (len(head_ops) * HEAD_DIM, BF16)), **proj_geom)
    lru_args = lambda dr: (lru_conv_w[0], lru_conv_b[0], lru_w_a[0, dr].astype(BF16), lru_b_a[0, dr],
                           lru_w_x[0, dr].astype(BF16), lru_b_x[0, dr], lru_lambda[0, dr])
    scan_geom = dict(n_batch=n_batch, seq_len=seq_len, ctx_len=ctx_len, tt=ctx_len)
    h_fwd = _lru_pass(p, *lru_args(0), None, reverse=False, **scan_geom)
    lru = _lru_pass(p, *lru_args(1), h_fwd, reverse=True, **scan_geom)
    att = _swa(qkv, attn_sink[0], n_batch=n_batch, seq_len=seq_len, ctx_len=ctx_len, n_kv_heads=n_kv,
               blk=WINDOW)
    x_all = _outproj(lru, att, 0, ab_w_out[0].astype(BF16), x_all, tl + tc, mod, 0, tm=t["tm_out"], **geom)
    x_all = ffn(x_all, tl + tc, 0, 1)

    x_all = ffn(x_all, tl + tc, 1, 0)
    gains = jnp.stack([na_q_norm[0], na_k_norm[0]])
    na_d = na_heads * HEAD_DIM
    gw = NA_HEADS_PER_STEP * HEAD_DIM
    plan = []
    for g0 in range(0, na_d, gw):
        plan += _head_chunks([(0, False)] * NA_HEADS_PER_STEP, g0, 0, 2 * na_d + g0)
        plan += _head_chunks([(1, False)] * NA_HEADS_PER_STEP, na_d + g0, 0, 2 * g0)
        plan += _head_chunks([None] * NA_HEADS_PER_STEP, 2 * na_d + g0, 0, 2 * g0 + gw)
    (qkv,) = _inproj(x_all, mod, 1, norm_g[1, 1], na_w_in[0].astype(BF16), gains, cos_t, sin_t, plan,
                     ((3 * na_d, BF16),), **proj_geom)
    o = _na(qkv, _na_bias_table(na_rpb[0], seq_len // GRID_W), n_batch=n_batch, seq_len=seq_len, ctx_len=ctx_len,
            n_heads=na_heads)
    x_lat = _outproj(o, o, 1, na_w_out[0].astype(BF16), x_all, tl, mod, 1, tm=t["tm_out"], **geom)
    x_lat = ffn(x_lat, tl, 1, 1)
    return x_lat.reshape(n_batch, seq_len, d)
```

```python
import functools

import numpy as np
import jax
import jax.numpy as jnp
from jax import lax
from jax.experimental import pallas as pl
from jax.experimental.pallas import tpu as pltpu

HEAD_DIM = 128
EPS = 1e-6
N_MOD = 9
GRID_W = 64
CONV_W = 4
LRU_C = 8.0
WINDOW = 128
ROPE_THETA = 10000.0
NA_KH = 8
NA_KW = 16
ATTN_SCALE = HEAD_DIM ** -0.5
GQA_GROUP = 4

LANES = 128
SUBLANES = 8
VMEM_LIMIT_BYTES = 56 * 1024 * 1024
MOD_ROWS = SUBLANES

NEG = -1e30
NA_ROWS_PER_BLOCK = 4

BF16 = jnp.bfloat16
F32 = jnp.float32


def _cparams(n_axes):
    return pltpu.CompilerParams(dimension_semantics=("arbitrary",) * n_axes,
                                vmem_limit_bytes=VMEM_LIMIT_BYTES)


def _sigmoid(x):
    return 1.0 / (1.0 + jnp.exp(-x))


def _sigmoid_tanh(x):
    return 0.5 * jnp.tanh(0.5 * x) + 0.5


NORM_CHUNK_ROWS = 16


def _norm_modulate_into(h_scr, x_ref, g, shift, scale):
    gain = g * (1.0 + scale)

    def chunk(c, carry):
        r0 = pl.multiple_of(c * NORM_CHUNK_ROWS, NORM_CHUNK_ROWS)
        x = x_ref[pl.ds(r0, NORM_CHUNK_ROWS), :]
        inv = lax.rsqrt(jnp.mean(x * x, axis=-1, keepdims=True) + EPS)
        h_scr[pl.ds(r0, NORM_CHUNK_ROWS), :] = ((x * inv) * gain + shift).astype(h_scr.dtype)
        return carry

    lax.fori_loop(0, x_ref.shape[0] // NORM_CHUNK_ROWS, chunk, 0, unroll=4)


def _mod_kernel(c_ref, w_ref, b_ref, o_ref):
    c = c_ref[...]
    s = c * _sigmoid(c)
    o_ref[...] = jnp.dot(s.astype(BF16), w_ref[...].astype(BF16),
                         preferred_element_type=F32) + b_ref[...]


def _mod_table(cc, w_mod, b_mod, tn):
    depth, d, n = w_mod.shape
    return pl.pallas_call(
        _mod_kernel,
        out_shape=jax.ShapeDtypeStruct((depth, MOD_ROWS, n), F32),
        grid=(depth, n // tn),
        in_specs=[pl.BlockSpec((MOD_ROWS, d), lambda l, j: (0, 0)),
                  pl.BlockSpec((None, d, tn), lambda l, j: (l, 0, j)),
                  pl.BlockSpec((None, 1, tn), lambda l, j: (l, 0, j))],
        out_specs=pl.BlockSpec((None, MOD_ROWS, tn), lambda l, j: (l, 0, j)),
        compiler_params=_cparams(2),
        name="mod_table",
    )(cc, w_mod, b_mod.reshape(depth, 1, n))


def _mod_spec(d, layer, k, n_grid_axes):
    if n_grid_axes == 1:
        return pl.BlockSpec((None, MOD_ROWS, d), lambda i: (layer, 0, k))
    return pl.BlockSpec((None, MOD_ROWS, d), lambda i, j: (layer, 0, k))


def _mod_row(tile_idx, tm, rows_per_batch, n_batch):
    return jnp.minimum(tile_idx * tm // rows_per_batch, n_batch)


def _ffn_kernel(x_ref, g_ref, sh_ref, sc_ref, gt_ref, wg_ref, wu_ref, wo_ref, *rest,
                tm, tile0, rows_per_batch, n_batch, last_width):
    o_ref, h_scr = rest[-2:]
    i, j = pl.program_id(0), pl.program_id(1)
    last = pl.num_programs(1) - 1
    r = _mod_row(i + tile0, tm, rows_per_batch, n_batch)

    @pl.when(j == 0)
    def _():
        _norm_modulate_into(h_scr, x_ref, g_ref[...], sh_ref[pl.ds(r, 1), :], sc_ref[pl.ds(r, 1), :])
        o_ref[...] = jnp.zeros_like(o_ref)

    def hidden_block(width):
        h = h_scr[...]
        a = jnp.dot(h, wg_ref[:, :width], preferred_element_type=F32)
        u = jnp.dot(h, wu_ref[:, :width], preferred_element_type=F32)
        act = (a * _sigmoid(a)) * u
        o_ref[...] += jnp.dot(act.astype(BF16), wo_ref[:width, :], preferred_element_type=F32)

    if last_width == wg_ref.shape[1]:
        hidden_block(last_width)
    else:
        pl.when(j < last)(lambda: hidden_block(wg_ref.shape[1]))
        pl.when(j == last)(lambda: hidden_block(last_width))

    @pl.when(j == last)
    def _():
        o_ref[...] = x_ref[...] + 0.5 * gt_ref[pl.ds(r, 1), :] * o_ref[...]


def _ffn(x_src, n_tiles, out_rows, tile0, dst, mod, layer, which, g, wg, wu, wo, *, tm, tf, rows_per_batch, n_batch):
    d = x_src.shape[1]
    d_ff = wo.shape[2]
    nf = pl.cdiv(d_ff, tf)
    last_width = d_ff - (nf - 1) * tf
    assert last_width % LANES == 0
    k0 = 6 * which
    kern = functools.partial(_ffn_kernel, tm=tm, tile0=tile0, rows_per_batch=rows_per_batch, n_batch=n_batch,
                             last_width=last_width)
    in_specs = [pl.BlockSpec((tm, d), lambda i, j: (i, 0)),
                pl.BlockSpec((1, d), lambda i, j: (0, 0)),
                _mod_spec(d, layer, k0, 2), _mod_spec(d, layer, k0 + 1, 2), _mod_spec(d, layer, k0 + 2, 2),
                pl.BlockSpec((None, None, d, tf), lambda i, j: (layer, which, 0, j)),
                pl.BlockSpec((None, None, d, tf), lambda i, j: (layer, which, 0, j)),
                pl.BlockSpec((None, None, tf, d), lambda i, j: (layer, which, j, 0))]
    args = [x_src, g.reshape(1, d), mod, mod, mod, wg, wu, wo]
    aliases = {}
    if dst is not None:
        in_specs.append(pl.BlockSpec(memory_space=pl.ANY))
        aliases = {len(args): 0}
        args.append(dst)
    return pl.pallas_call(
        kern,
        out_shape=jax.ShapeDtypeStruct((out_rows, d), F32),
        grid=(n_tiles, nf),
        in_specs=in_specs,
        out_specs=pl.BlockSpec((tm, d), lambda i, j: (i + tile0, 0)),
        scratch_shapes=[pltpu.VMEM((tm, d), BF16)],
        input_output_aliases=aliases,
        compiler_params=_cparams(2),
        name="half_ffn",
    )(*args)


def _cast_kernel(*refs):
    n = len(refs) // 2
    for src, dst in zip(refs[:n], refs[n:]):
        dst[...] = src[...].astype(dst.dtype)


def _largest_tile(n, multiple, cap):
    return max(t for t in range(multiple, min(n, cap) + 1, multiple) if n % t == 0)


def _cast_ffn_weights(w_in, w_out):
    depth, n_ffn, d, f2 = w_in.shape
    f = f2 // 2
    assert f % LANES == 0
    rows = _largest_tile(d, 2 * SUBLANES, 256)
    half = (None, None, rows, f)
    wg, wu = pl.pallas_call(
        _cast_kernel,
        out_shape=[jax.ShapeDtypeStruct((depth, n_ffn, d, f), BF16)] * 2,
        grid=(depth, n_ffn, d // rows),
        in_specs=[pl.BlockSpec(half, lambda l, w, r: (l, w, r, 0)), pl.BlockSpec(half, lambda l, w, r: (l, w, r, 1))],
        out_specs=[pl.BlockSpec(half, lambda l, w, r: (l, w, r, 0))] * 2,
        compiler_params=_cparams(3),
        name="cast_ffn_in",
    )(w_in, w_in)
    rows = _largest_tile(f, 2 * SUBLANES, 1024)
    blk = (None, None, rows, d)
    wo = pl.pallas_call(
        _cast_kernel,
        out_shape=jax.ShapeDtypeStruct(w_out.shape, BF16),
        grid=(depth, n_ffn, f // rows),
        in_specs=[pl.BlockSpec(blk, lambda l, w, r: (l, w, r, 0))],
        out_specs=pl.BlockSpec(blk, lambda l, w, r: (l, w, r, 0)),
        compiler_params=_cparams(3),
        name="cast_ffn_out",
    )(w_out)
    return wg, wu, wo


PROJ_CHUNK = 4 * HEAD_DIM


def _swap_halves_32(y):
    lane = lax.broadcasted_iota(jnp.int32, y.shape, 1)
    return jnp.where((lane & 32) == 0, pltpu.roll(y, LANES - 32, 1), pltpu.roll(y, 32, 1))


def _inproj_kernel(x_ref, g_ref, sh_ref, sc_ref, w_ref, gains_ref, cos_ref, sin_ref, *rest,
                   plan, tm, rows_per_batch, n_batch, n_lat_tiles):
    out_refs, h_scr = rest[:-1], rest[-1]
    i = pl.program_id(0)
    r = _mod_row(i, tm, rows_per_batch, n_batch)
    is_lat = i < n_lat_tiles
    _norm_modulate_into(h_scr, x_ref, g_ref[...], sh_ref[pl.ds(r, 1), :], sc_ref[pl.ds(r, 1), :])
    for w_col, width, out_idx, out_col, ops in plan:
        y = jnp.dot(h_scr[...], w_ref[:, w_col:w_col + width], preferred_element_type=F32)
        o_ref = out_refs[out_idx]
        if ops is None:
            o_ref[:, out_col:out_col + width] = y.astype(o_ref.dtype)
            continue
        for hh, op in enumerate(ops):
            yh = y[:, hh * HEAD_DIM:(hh + 1) * HEAD_DIM]
            if op is not None:
                gain_row, rope = op
                yh = yh * lax.rsqrt(jnp.mean(yh * yh, axis=-1, keepdims=True) + EPS)
                yh = yh * gains_ref[gain_row:gain_row + 1, :]
                if rope:
                    yh = jnp.where(is_lat, yh * cos_ref[...] + _swap_halves_32(yh) * sin_ref[...], yh)
            o_ref[:, out_col + hh * HEAD_DIM:out_col + (hh + 1) * HEAD_DIM] = yh.astype(o_ref.dtype)


def _inproj(x_all, mod, layer, g, w, gains, cos_t, sin_t, plan, outs, *, tm, n_lat_rows, rows_per_batch, n_batch):
    ta, d = x_all.shape
    n_pos_tiles = cos_t.shape[0] // tm
    kern = functools.partial(_inproj_kernel, plan=tuple(plan), tm=tm, rows_per_batch=rows_per_batch,
                             n_batch=n_batch, n_lat_tiles=n_lat_rows // tm)
    return pl.pallas_call(
        kern,
        out_shape=[jax.ShapeDtypeStruct((ta, n), dt) for n, dt in outs],
        grid=(ta // tm,),
        in_specs=[pl.BlockSpec((tm, d), lambda i: (i, 0)),
                  pl.BlockSpec((1, d), lambda i: (0, 0)),
                  _mod_spec(d, layer, 3, 1), _mod_spec(d, layer, 4, 1),
                  pl.BlockSpec(w.shape, lambda i: (0, 0), pipeline_mode=pl.Buffered(1)),
                  pl.BlockSpec(gains.shape, lambda i: (0, 0)),
                  pl.BlockSpec((tm, HEAD_DIM), lambda i: (i % n_pos_tiles, 0)),
                  pl.BlockSpec((tm, HEAD_DIM), lambda i: (i % n_pos_tiles, 0))],
        out_specs=[pl.BlockSpec((tm, n), lambda i: (i, 0)) for n, _ in outs],
        scratch_shapes=[pltpu.VMEM((tm, d), BF16)],
        compiler_params=_cparams(1),
        name="mixer_inproj",
    )(x_all, g.reshape(1, d), mod, mod, w, gains, cos_t, sin_t)


def _outproj_kernel(a1_ref, a2_ref, w1_ref, w2_ref, x_ref, gt_ref, o_ref, *, tm, rows_per_batch, n_batch):
    r = _mod_row(pl.program_id(0), tm, rows_per_batch, n_batch)
    y = (jnp.dot(a1_ref[...], w1_ref[...], preferred_element_type=F32)
         + jnp.dot(a2_ref[...], w2_ref[...], preferred_element_type=F32))
    o_ref[...] = x_ref[...] + gt_ref[pl.ds(r, 1), :] * y


def _outproj(a1, a2, a2_col_block, w, x_all, n_rows, mod, layer, *, tm, rows_per_batch, n_batch):
    d = x_all.shape[1]
    kh = w.shape[0] // 2
    kern = functools.partial(_outproj_kernel, tm=tm, rows_per_batch=rows_per_batch, n_batch=n_batch)
    return pl.pallas_call(
        kern,
        out_shape=jax.ShapeDtypeStruct((n_rows, d), F32),
        grid=(n_rows // tm,),
        in_specs=[pl.BlockSpec((tm, kh), lambda i: (i, 0)),
                  pl.BlockSpec((tm, kh), lambda i: (i, a2_col_block)),
                  pl.BlockSpec((kh, d), lambda i: (0, 0)),
                  pl.BlockSpec((kh, d), lambda i: (1, 0)),
                  pl.BlockSpec((tm, d), lambda i: (i, 0)),
                  _mod_spec(d, layer, 5, 1)],
        out_specs=pl.BlockSpec((tm, d), lambda i: (i, 0)),
        compiler_params=_cparams(1),
        name="mixer_outproj",
    )(a1, a2, w, w, x_all, mod)


def _rope_tables(seq_len):
    half = HEAD_DIM // 2
    nf = half // 2
    inv = ROPE_THETA ** (-jnp.arange(nf, dtype=F32) / nf)
    pos = jnp.arange(seq_len, dtype=jnp.int32)
    ang_r = (pos // GRID_W).astype(F32)[:, None] * inv[None, :]
    ang_c = (pos % GRID_W).astype(F32)[:, None] * inv[None, :]
    cr, sr, cc, sc = jnp.cos(ang_r), jnp.sin(ang_r), jnp.cos(ang_c), jnp.sin(ang_c)
    return (jnp.concatenate([cr, cr, cc, cc], axis=-1),
            jnp.concatenate([-sr, sr, -sc, sc], axis=-1))


def _gelu_tanh(x):
    return x * (0.5 * (1.0 + jnp.tanh(np.float32(np.sqrt(2.0 / np.pi)) * (x + 0.044715 * (x * x * x)))))


def _lru_kernel(*refs, tt, reverse, final, n_lat_chunks):
    if final:
        (xp_ref, xc_ref, xn_ref, cw_ref, cb_ref, wa_ref, ba_ref, wx_ref, bx_ref, lam_ref,
         hf_ref, gl_ref, o_ref, xs_scr, a_scr, b_scr, carry_scr) = refs
    else:
        (xp_ref, xc_ref, xn_ref, cw_ref, cb_ref, wa_ref, ba_ref, wx_ref, bx_ref, lam_ref,
         o_ref, xs_scr, a_scr, b_scr, carry_scr) = refs
    s = pl.program_id(1)
    is_ctx = s == 0
    j = (n_lat_chunks - s) if reverse else (s - 1)
    has_prev = jnp.logical_and(jnp.logical_not(is_ctx), j > 0)
    has_next = jnp.logical_and(jnp.logical_not(is_ctx), j < n_lat_chunks - 1)

    @pl.when(is_ctx)
    def _():
        carry_scr[...] = jnp.zeros_like(carry_scr)

    halo = SUBLANES
    xs_scr[0:halo, :] = jnp.where(has_prev, xp_ref[tt - halo:tt, :], 0.0)
    xs_scr[halo:halo + tt, :] = xc_ref[...]
    xs_scr[halo + tt:2 * halo + tt, :] = jnp.where(has_next, xn_ref[0:halo, :], 0.0)

    width = xc_ref.shape[1]
    left = CONV_W // 2
    z = -lam_ref[...]
    softplus = jnp.maximum(z, 0.0) + jnp.log1p(jnp.exp(-jnp.abs(z)))
    for n in range(width // HEAD_DIM):
        cs = slice(n * HEAD_DIM, (n + 1) * HEAD_DIM)
        u = cb_ref[:, cs]
        for k in range(CONV_W):
            u = u + xs_scr[halo - left + k:halo - left + k + tt, cs] * cw_ref[k:k + 1, cs]
        ub = u.astype(BF16)
        gate_r = jnp.dot(ub, wa_ref[n], preferred_element_type=F32) + ba_ref[:, cs]
        gate_i = jnp.dot(ub, wx_ref[n], preferred_element_type=F32) + bx_ref[:, cs]
        log_a = (-LRU_C) * _sigmoid_tanh(gate_r) * softplus[:, cs]
        a = jnp.exp(log_a)
        a_scr[:, cs] = a
        b_scr[:, cs] = jnp.sqrt(-jnp.tanh(log_a) * (1.0 + a * a)) * (_sigmoid_tanh(gate_i) * u)

    n_groups = tt // SUBLANES
    row = lax.broadcasted_iota(jnp.int32, (SUBLANES, width), 0)

    def group(gi, carry):
        g = (n_groups - 1 - gi) if reverse else gi
        r0 = pl.multiple_of(g * SUBLANES, SUBLANES)
        a = a_scr[pl.ds(r0, SUBLANES), :]
        b = b_scr[pl.ds(r0, SUBLANES), :]
        for k in (1, 2, 4):
            if reverse:
                keep = row < SUBLANES - k
                shift = SUBLANES - k
            else:
                keep = row >= k
                shift = k
            a_sh = jnp.where(keep, pltpu.roll(a, shift, 0), 1.0)
            b_sh = jnp.where(keep, pltpu.roll(b, shift, 0), 0.0)
            b = a * b_sh + b
            a = a * a_sh
        h = b + a * carry
        b_scr[pl.ds(r0, SUBLANES), :] = h
        last = h[0:1, :] if reverse else h[SUBLANES - 1:SUBLANES, :]
        return jnp.broadcast_to(last, (SUBLANES, width))

    carry_scr[...] = lax.fori_loop(0, n_groups, group, carry_scr[...])

    if final:
        o_ref[...] = ((hf_ref[...] + b_scr[...]) * _gelu_tanh(gl_ref[...])).astype(o_ref.dtype)
    else:
        o_ref[...] = b_scr[...]


def _lru_pass(p, conv_w, conv_b, w_a, b_a, w_x, b_x, lam, hf, *, reverse, n_batch, seq_len, ctx_len, tt):
    ta = p.shape[0]
    width = conv_w.shape[1]
    assert ctx_len == tt and seq_len % tt == 0
    n_lat_chunks = seq_len // tt
    ctx_block0 = n_batch * n_lat_chunks
    final = hf is not None

    def chunk_index(b, s, off):
        j = (n_lat_chunks - s) if reverse else (s - 1)
        j = jnp.clip(j + off, 0, n_lat_chunks - 1)
        return jnp.where(s == 0, ctx_block0 + b, b * n_lat_chunks + j)

    def xspec(off):
        return pl.BlockSpec((tt, width), lambda b, s: (chunk_index(b, s, off), 0))

    row_spec = pl.BlockSpec((1, width), lambda b, s: (0, 0))
    w_spec = pl.BlockSpec(w_a.shape, lambda b, s: (0, 0, 0))
    in_specs = [xspec(-1), xspec(0), xspec(1),
                pl.BlockSpec((CONV_W, width), lambda b, s: (0, 0)), row_spec,
                w_spec, row_spec, w_spec, row_spec, row_spec]
    args = [p, p, p, conv_w, conv_b.reshape(1, width), w_a, b_a.reshape(1, width),
            w_x, b_x.reshape(1, width), lam.reshape(1, width)]
    if final:
        in_specs += [xspec(0), pl.BlockSpec((tt, width), lambda b, s: (chunk_index(b, s, 0), 1))]
        args += [hf, p]
    kern = functools.partial(_lru_kernel, tt=tt, reverse=reverse, final=final, n_lat_chunks=n_lat_chunks)
    return pl.pallas_call(
        kern,
        out_shape=jax.ShapeDtypeStruct((ta, width), BF16 if final else F32),
        grid=(n_batch, n_lat_chunks + 1),
        in_specs=in_specs,
        out_specs=xspec(0),
        scratch_shapes=[pltpu.VMEM((tt + 2 * SUBLANES, width), F32), pltpu.VMEM((tt, width), F32),
                        pltpu.VMEM((tt, width), F32), pltpu.VMEM((SUBLANES, width), F32)],
        compiler_params=_cparams(2),
        name="rglru_rev" if reverse else "rglru_fwd",
    )(*args)


LOG2E = float(np.log2(np.e))
LOGIT_SCALE = ATTN_SCALE * LOG2E


def _softmax_parts(s2, extra_logit2=None):
    m = jnp.max(s2, axis=-1, keepdims=True)
    if extra_logit2 is not None:
        m = jnp.maximum(m, extra_logit2)
    e = jnp.exp2(s2 - m)
    denom = jnp.sum(e, axis=-1, keepdims=True)
    if extra_logit2 is not None:
        denom = denom + jnp.exp2(extra_logit2 - m)
    return e, 1.0 / denom


def _swa_kernel(sink_ref, q_ref, kp_ref, kc_ref, kn_ref, kx_ref, vp_ref, vc_ref, vn_ref, vx_ref, mask_ref, o_ref,
                *, n_kv_heads):
    blk = q_ref.shape[0]
    head = lax.broadcasted_iota(jnp.int32, (GQA_GROUP * blk, 1), 0) // blk
    logits = []
    for kh in range(n_kv_heads):
        ks = slice(kh * HEAD_DIM, (kh + 1) * HEAD_DIM)
        q0 = kh * GQA_GROUP
        q = jnp.concatenate([q_ref[:, (q0 + h) * HEAD_DIM:(q0 + h + 1) * HEAD_DIM] for h in range(GQA_GROUP)],
                            axis=0)
        k = jnp.concatenate([kp_ref[:, ks], kc_ref[:, ks], kn_ref[:, ks], kx_ref[:, ks]], axis=0)
        logits.append(lax.dot_general(q, k, (((1,), (1,)), ((), ())), preferred_element_type=F32))
    for kh in range(n_kv_heads):
        ks = slice(kh * HEAD_DIM, (kh + 1) * HEAD_DIM)
        q0 = kh * GQA_GROUP
        v = jnp.concatenate([vp_ref[:, ks], vc_ref[:, ks], vn_ref[:, ks], vx_ref[:, ks]], axis=0)
        s = logits[kh] * LOGIT_SCALE + mask_ref[...]
        sink = jnp.zeros((GQA_GROUP * blk, 1), F32)
        for h in range(GQA_GROUP):
            sink = jnp.where(head == h, sink_ref[q0 + h] * LOG2E, sink)
        e, inv = _softmax_parts(s, sink)
        o = jnp.dot(e.astype(BF16), v, preferred_element_type=F32) * inv
        for h in range(GQA_GROUP):
            o_ref[:, (q0 + h) * HEAD_DIM:(q0 + h + 1) * HEAD_DIM] = o[h * blk:(h + 1) * blk, :].astype(o_ref.dtype)


def _swa_mask_table(blk, ctx_len):
    shape = (4, GQA_GROUP * blk, 3 * blk + ctx_len)
    cls = lax.broadcasted_iota(jnp.int32, shape, 0)
    qi = lax.broadcasted_iota(jnp.int32, shape, 1) % blk
    col = lax.broadcasted_iota(jnp.int32, shape, 2)
    lo = jnp.where(cls == 0, blk, 0)
    hi = jnp.where(cls == 3, 0, jnp.where(cls == 2, 2 * blk, 3 * blk))
    rel = col - blk - qi
    ok = ((col >= lo) & (col < hi) & (rel >= -WINDOW) & (rel <= WINDOW)) | (col >= 3 * blk)
    return jnp.where(ok, 0.0, NEG).astype(F32)


def _swa(qkv, sink, *, n_batch, seq_len, ctx_len, n_kv_heads, blk):
    ta = qkv.shape[0]
    n_heads = n_kv_heads * GQA_GROUP
    n_lat_blocks = seq_len // blk
    n_ctx_blocks = ctx_len // blk
    assert blk == WINDOW and seq_len % blk == 0 and ctx_len % blk == 0 and n_lat_blocks >= 2
    ctx_q0 = n_batch * n_lat_blocks
    ctx_kv0 = n_batch * seq_len // ctx_len
    kv_w = n_kv_heads * HEAD_DIM
    k_col, v_col = n_heads * HEAD_DIM // kv_w, n_heads * HEAD_DIM // kv_w + 1

    def q_index(b, n):
        return (jnp.where(n < n_lat_blocks, b * n_lat_blocks + n, ctx_q0 + b * n_ctx_blocks + n - n_lat_blocks), 0)

    def kv_spec(off, col):
        return pl.BlockSpec((blk, kv_w),
                            lambda b, n: (b * n_lat_blocks + jnp.clip(n + off, 0, n_lat_blocks - 1), col))

    def ctx_spec(col):
        return pl.BlockSpec((ctx_len, kv_w), lambda b, n: (ctx_kv0 + b, col))

    def mask_index(b, n):
        interior = jnp.where(n == 0, 0, jnp.where(n == n_lat_blocks - 1, 2, 1))
        return (jnp.where(n < n_lat_blocks, interior, 3), 0, 0)

    mask = _swa_mask_table(blk, ctx_len)
    kern = functools.partial(_swa_kernel, n_kv_heads=n_kv_heads)
    return pl.pallas_call(
        kern,
        out_shape=jax.ShapeDtypeStruct((ta, n_heads * HEAD_DIM), BF16),
        grid=(n_batch, n_lat_blocks + n_ctx_blocks),
        in_specs=[pl.BlockSpec(memory_space=pltpu.SMEM),
                  pl.BlockSpec((blk, n_heads * HEAD_DIM), q_index),
                  kv_spec(-1, k_col), kv_spec(0, k_col), kv_spec(1, k_col), ctx_spec(k_col),
                  kv_spec(-1, v_col), kv_spec(0, v_col), kv_spec(1, v_col), ctx_spec(v_col),
                  pl.BlockSpec((None,) + mask.shape[1:], mask_index)],
        out_specs=pl.BlockSpec((blk, n_heads * HEAD_DIM), q_index),
        compiler_params=_cparams(2),
        name="windowed_gqa",
    )(sink, *([qkv] * 9), mask)


NA_HEADS_PER_STEP = 8


def _na_kernel(q_ref, kvp_ref, kvc_ref, kvn_ref, kvx_ref, bias_ref, o_ref):
    n_local = bias_ref.shape[-1]
    gw = NA_HEADS_PER_STEP * HEAD_DIM
    kv_refs = (kvp_ref, kvc_ref, kvn_ref, kvx_ref)
    heads = [slice(h * HEAD_DIM, (h + 1) * HEAD_DIM) for h in range(NA_HEADS_PER_STEP)]
    logits = []
    for hs in heads:
        k = jnp.concatenate([ref[:, hs] for ref in kv_refs], axis=0)
        logits.append(lax.dot_general(q_ref[:, hs], k, (((1,), (1,)), ((), ())), preferred_element_type=F32))
    weights = []
    for h, hs in enumerate(heads):
        s = logits[h] * LOGIT_SCALE
        s = jnp.concatenate([s[:, :n_local] + bias_ref[h], s[:, n_local:]], axis=1)
        e, inv = _softmax_parts(s)
        weights.append((e.astype(BF16), inv))
    for (e, inv), hs in zip(weights, heads):
        vs = slice(gw + hs.start, gw + hs.stop)
        v = jnp.concatenate([ref[:, vs] for ref in kv_refs], axis=0)
        o_ref[:, hs] = (jnp.dot(e, v, preferred_element_type=F32) * inv).astype(o_ref.dtype)


def _na_bias_table(rpb, n_grid_rows):
    r_in, w = NA_ROWS_PER_BLOCK, GRID_W
    kh = min(NA_KH, n_grid_rows)
    n_blocks = n_grid_rows // r_in
    n_ro, n_co = 2 * NA_KH - 1, 2 * NA_KW - 1
    qc, kc = np.arange(w)[:, None], np.arange(w)[None, :]
    e_col = (kc - qc + NA_KW - 1 == np.arange(n_co)[:, None, None]).astype(np.float32)
    win_start = np.clip(qc - NA_KW // 2, 0, w - NA_KW)
    col_ok = (kc >= win_start) & (kc < win_start + NA_KW)
    ri, krj = np.arange(r_in)[:, None], np.arange(3 * r_in)[None, :]
    e_row = (krj - ri - r_in + NA_KH - 1 == np.arange(n_ro)[:, None, None]).astype(np.float32)
    row_ok = []
    for jb in (0, max(n_blocks // 2, 1) if n_blocks > 2 else 0, n_blocks - 1):
        r, kr = r_in * jb + ri, r_in * (jb - 1) + krj
        rs = np.clip(r - kh // 2, 0, n_grid_rows - kh)
        row_ok.append((kr >= rs) & (kr < rs + kh))
    ok = np.stack(row_ok)[:, :, None, :, None] & col_ok[None, None, :, None, :]
    hi = lax.Precision.HIGHEST
    tz = jnp.einsum('hab,bqc->haqc', rpb.astype(F32), e_col, precision=hi)
    full = jnp.einsum('haqc,aik->hiqkc', tz, e_row, precision=hi)
    tab = jnp.where(ok[None], full[:, None] * LOG2E, NEG)
    return tab.reshape(rpb.shape[0], 3, r_in * w, 3 * r_in * w)


def _na(qkv, bias, *, n_batch, seq_len, ctx_len, n_heads):
    tq = NA_ROWS_PER_BLOCK * GRID_W
    n_blocks = seq_len // tq
    assert ctx_len == tq and seq_len % tq == 0 and NA_KH == 2 * NA_ROWS_PER_BLOCK and n_blocks >= 2
    assert n_heads % NA_HEADS_PER_STEP == 0
    n_groups = n_heads // NA_HEADS_PER_STEP
    gw = NA_HEADS_PER_STEP * HEAD_DIM
    ctx0 = n_batch * n_blocks

    def kv_spec(off):
        return pl.BlockSpec((tq, 2 * gw),
                            lambda b, g, jb: (b * n_blocks + jnp.clip(jb + off, 0, n_blocks - 1), g))

    def bias_index(b, g, jb):
        return (g, jnp.where(jb == 0, 0, jnp.where(jb == n_blocks - 1, 2, 1)), 0, 0)

    return pl.pallas_call(
        _na_kernel,
        out_shape=jax.ShapeDtypeStruct((n_batch * seq_len, n_heads * HEAD_DIM), BF16),
        grid=(n_batch, n_groups, n_blocks),
        in_specs=[pl.BlockSpec((tq, gw), lambda b, g, jb: (b * n_blocks + jb, 2 * n_groups + g)),
                  kv_spec(-1), kv_spec(0), kv_spec(1),
                  pl.BlockSpec((ctx_len, 2 * gw), lambda b, g, jb: (ctx0 + b, g)),
                  pl.BlockSpec((NA_HEADS_PER_STEP, None) + bias.shape[2:], bias_index)],
        out_specs=pl.BlockSpec((tq, gw), lambda b, g, jb: (b * n_blocks + jb, g)),
        compiler_params=_cparams(3),
        name="neighbourhood_attn",
    )(*([qkv] * 5), bias)


def _tiles():
    return dict(tm=1024, tm_proj=512, tm_out=512, tf=512, tn_mod=1024)


def _chunked(total, out_idx, out_col0=0, w_col0=0):
    return [(w_col0 + c, min(PROJ_CHUNK, total - c), out_idx, out_col0 + c, None)
            for c in range(0, total, PROJ_CHUNK)]


def _head_chunks(head_ops, w_col0, out_idx, out_col0):
    per = PROJ_CHUNK // HEAD_DIM
    return [(w_col0 + h0 * HEAD_DIM, len(head_ops[h0:h0 + per]) * HEAD_DIM, out_idx, out_col0 + h0 * HEAD_DIM,
             tuple(head_ops[h0:h0 + per])) for h0 in range(0, len(head_ops), per)]


def kernel(x, c, ctx, c_ctx, w_mod, b_mod, norm_g, ffn_w_in, ffn_w_out, ab_w_in, lru_conv_w, lru_conv_b,
           lru_w_a, lru_b_a, lru_w_x, lru_b_x, lru_lambda, attn_q_norm, attn_k_norm, attn_sink, ab_w_out,
           na_w_in, na_q_norm, na_k_norm, na_rpb, na_w_out):
    n_batch, seq_len, d = x.shape
    ctx_len = ctx.shape[1]
    depth = w_mod.shape[0]
    lru_w = lru_conv_w.shape[2]
    n_kv = attn_sink.shape[1] // GQA_GROUP
    n_att = n_kv * GQA_GROUP
    na_heads = na_rpb.shape[1]
    assert n_batch < MOD_ROWS and depth == 2
    t = _tiles()
    tl, tc = n_batch * seq_len, n_batch * ctx_len
    geom = dict(rows_per_batch=seq_len, n_batch=n_batch)

    cc = jnp.zeros((MOD_ROWS, d), F32).at[:n_batch].set(c).at[n_batch].set(c_ctx)
    mod = _mod_table(cc, w_mod, b_mod, t["tn_mod"])
    wg, wu, wo = _cast_ffn_weights(ffn_w_in, ffn_w_out)
    n_lat_tiles, n_ctx_tiles = tl // t["tm"], tc // t["tm"]

    def ffn(x_src, n_tiles, out_rows, layer, which, tile0=0, dst=None):
        return _ffn(x_src, n_tiles, out_rows, tile0, dst, mod, layer, which, norm_g[layer, 2 * which], wg, wu, wo,
                    tm=t["tm"], tf=t["tf"], **geom)

    cos_t, sin_t = _rope_tables(seq_len)
    proj_geom = dict(tm=t["tm_proj"], n_lat_rows=tl, **geom)

    x_all = ffn(x.reshape(tl, d), n_lat_tiles, tl + tc, 0, 0)
    x_all = ffn(ctx.reshape(tc, d), n_ctx_tiles, tl + tc, 0, 0, tile0=n_lat_tiles, dst=x_all)
    gains = jnp.stack([attn_q_norm[0], attn_k_norm[0]])
    head_ops = [(0, True)] * n_att + [(1, True)] * n_kv + [None] * n_kv
    plan = _chunked(2 * lru_w, 0) + _head_chunks(head_ops, 2 * lru_w, 1, 0)
    p, qkv = _inproj(x_all, mod, 0, norm_g[0, 1], ab_w_in[0].astype(BF16), gains, cos_t, sin_t, plan,
                     ((2 * lru_w, F32), (len(head_ops) * HEAD_DIM, BF16)), **proj_geom)
    lru_args = lambda dr: (lru_conv_w[0], lru_conv_b[0], lru_w_a[0, dr].astype(BF16), lru_b_a[0, dr],
                           lru_w_x[0, dr].astype(BF16), lru_b_x[0, dr], lru_lambda[0, dr])
    scan_geom = dict(n_batch=n_batch, seq_len=seq_len, ctx_len=ctx_len, tt=ctx_len)
    h_fwd = _lru_pass(p, *lru_args(0), None, reverse=False, **scan_geom)
    lru = _lru_pass(p, *lru_args(1), h_fwd, reverse=True, **scan_geom)
    att = _swa(qkv, attn_sink[0], n_batch=n_batch, seq_len=seq_len, ctx_len=ctx_len, n_kv_heads=n_kv,
               blk=WINDOW)
    x_all = _outproj(lru, att, 0, ab_w_out[0].astype(BF16), x_all, tl + tc, mod, 0, tm=t["tm_out"], **geom)
    x_all = ffn(x_all, n_lat_tiles + n_ctx_tiles, tl + tc, 0, 1)

    x_all = ffn(x_all, n_lat_tiles + n_ctx_tiles, tl + tc, 1, 0)
    gains = jnp.stack([na_q_norm[0], na_k_norm[0]])
    na_d = na_heads * HEAD_DIM
    gw = NA_HEADS_PER_STEP * HEAD_DIM
    plan = []
    for g0 in range(0, na_d, gw):
        plan += _head_chunks([(0, False)] * NA_HEADS_PER_STEP, g0, 0, 2 * na_d + g0)
        plan += _head_chunks([(1, False)] * NA_HEADS_PER_STEP, na_d + g0, 0, 2 * g0)
        plan += _head_chunks([None] * NA_HEADS_PER_STEP, 2 * na_d + g0, 0, 2 * g0 + gw)
    (qkv,) = _inproj(x_all, mod, 1, norm_g[1, 1], na_w_in[0].astype(BF16), gains, cos_t, sin_t, plan,
                     ((3 * na_d, BF16),), **proj_geom)
    o = _na(qkv, _na_bias_table(na_rpb[0], seq_len // GRID_W), n_batch=n_batch, seq_len=seq_len, ctx_len=ctx_len,
            n_heads=na_heads)
    x_lat = _outproj(o, o, 1, na_w_out[0].astype(BF16), x_all, tl, mod, 1, tm=t["tm_out"], **geom)
    x_lat = ffn(x_lat, n_lat_tiles, tl, 1, 1)
    return x_lat.reshape(n_batch, seq_len, d)
```

```python
import functools

import numpy as np
import jax
import jax.numpy as jnp
from jax import lax
from jax.experimental import pallas as pl
from jax.experimental.pallas import tpu as pltpu

HEAD_DIM = 128
EPS = 1e-6
N_MOD = 9
GRID_W = 64
CONV_W = 4
LRU_C = 8.0
WINDOW = 128
ROPE_THETA = 10000.0
NA_KH = 8
NA_KW = 16
ATTN_SCALE = HEAD_DIM ** -0.5
GQA_GROUP = 4

LANES = 128
SUBLANES = 8
VMEM_LIMIT_BYTES = 56 * 1024 * 1024
MOD_ROWS = SUBLANES

NEG = -1e30
NA_ROWS_PER_BLOCK = 4

BF16 = jnp.bfloat16
F32 = jnp.float32


def _cparams(n_axes):
    return pltpu.CompilerParams(dimension_semantics=("arbitrary",) * n_axes,
                                vmem_limit_bytes=VMEM_LIMIT_BYTES)


def _sigmoid(x):
    return 1.0 / (1.0 + jnp.exp(-x))


def _sigmoid_tanh(x):
    return 0.5 * jnp.tanh(0.5 * x) + 0.5


NORM_CHUNK_ROWS = 16


def _norm_modulate_into(h_scr, x_ref, g, shift, scale):
    gain = g * (1.0 + scale)

    def chunk(c, carry):
        r0 = pl.multiple_of(c * NORM_CHUNK_ROWS, NORM_CHUNK_ROWS)
        x = x_ref[pl.ds(r0, NORM_CHUNK_ROWS), :]
        inv = lax.rsqrt(jnp.mean(x * x, axis=-1, keepdims=True) + EPS)
        h_scr[pl.ds(r0, NORM_CHUNK_ROWS), :] = ((x * inv) * gain + shift).astype(h_scr.dtype)
        return carry

    lax.fori_loop(0, x_ref.shape[0] // NORM_CHUNK_ROWS, chunk, 0, unroll=4)


def _mod_kernel(c_ref, w_ref, b_ref, o_ref):
    c = c_ref[...]
    s = c * _sigmoid(c)
    o_ref[...] = jnp.dot(s.astype(BF16), w_ref[...].astype(BF16),
                         preferred_element_type=F32) + b_ref[...]


def _mod_table(cc, w_mod, b_mod, tn):
    depth, d, n = w_mod.shape
    return pl.pallas_call(
        _mod_kernel,
        out_shape=jax.ShapeDtypeStruct((depth, MOD_ROWS, n), F32),
        grid=(depth, n // tn),
        in_specs=[pl.BlockSpec((MOD_ROWS, d), lambda l, j: (0, 0)),
                  pl.BlockSpec((None, d, tn), lambda l, j: (l, 0, j)),
                  pl.BlockSpec((None, 1, tn), lambda l, j: (l, 0, j))],
        out_specs=pl.BlockSpec((None, MOD_ROWS, tn), lambda l, j: (l, 0, j)),
        compiler_params=_cparams(2),
        name="mod_table",
    )(cc, w_mod, b_mod.reshape(depth, 1, n))


def _mod_spec(d, layer, k, n_grid_axes):
    if n_grid_axes == 1:
        return pl.BlockSpec((None, MOD_ROWS, d), lambda i: (layer, 0, k))
    return pl.BlockSpec((None, MOD_ROWS, d), lambda i, j: (layer, 0, k))


def _mod_row(tile_idx, tm, rows_per_batch, n_batch):
    return jnp.minimum(tile_idx * tm // rows_per_batch, n_batch)


def _ffn_kernel(x_ref, g_ref, sh_ref, sc_ref, gt_ref, wg_ref, wu_ref, wo_ref, *rest,
                tm, tile0, rows_per_batch, n_batch, last_width):
    o_ref, h_scr = rest[-2:]
    i, j = pl.program_id(0), pl.program_id(1)
    last = pl.num_programs(1) - 1
    r = _mod_row(i + tile0, tm, rows_per_batch, n_batch)

    @pl.when(j == 0)
    def _():
        _norm_modulate_into(h_scr, x_ref, g_ref[...], sh_ref[pl.ds(r, 1), :], sc_ref[pl.ds(r, 1), :])
        o_ref[...] = jnp.zeros_like(o_ref)

    def hidden_block(width):
        h = h_scr[...]
        a = jnp.dot(h, wg_ref[:, :width], preferred_element_type=F32)
        u = jnp.dot(h, wu_ref[:, :width], preferred_element_type=F32)
        act = (a * _sigmoid(a)) * u
        o_ref[...] += jnp.dot(act.astype(BF16), wo_ref[:width, :], preferred_element_type=F32)

    if last_width == wg_ref.shape[1]:
        hidden_block(last_width)
    else:
        pl.when(j < last)(lambda: hidden_block(wg_ref.shape[1]))
        pl.when(j == last)(lambda: hidden_block(last_width))

    @pl.when(j == last)
    def _():
        o_ref[...] = x_ref[...] + 0.5 * gt_ref[pl.ds(r, 1), :] * o_ref[...]


def _ffn(x_src, n_tiles, out_rows, tile0, dst, mod, layer, which, g, wg, wu, wo, *, tm, tf, rows_per_batch, n_batch):
    d = x_src.shape[1]
    d_ff = wo.shape[2]
    nf = pl.cdiv(d_ff, tf)
    last_width = d_ff - (nf - 1) * tf
    assert last_width % LANES == 0
    k0 = 6 * which
    kern = functools.partial(_ffn_kernel, tm=tm, tile0=tile0, rows_per_batch=rows_per_batch, n_batch=n_batch,
                             last_width=last_width)
    in_specs = [pl.BlockSpec((tm, d), lambda i, j: (i, 0)),
                pl.BlockSpec((1, d), lambda i, j: (0, 0)),
                _mod_spec(d, layer, k0, 2), _mod_spec(d, layer, k0 + 1, 2), _mod_spec(d, layer, k0 + 2, 2),
                pl.BlockSpec((None, None, d, tf), lambda i, j: (layer, which, 0, j)),
                pl.BlockSpec((None, None, d, tf), lambda i, j: (layer, which, 0, j)),
                pl.BlockSpec((None, None, tf, d), lambda i, j: (layer, which, j, 0))]
    args = [x_src, g.reshape(1, d), mod, mod, mod, wg, wu, wo]
    aliases = {}
    if dst is not None:
        in_specs.append(pl.BlockSpec(memory_space=pl.ANY))
        aliases = {len(args): 0}
        args.append(dst)
    return pl.pallas_call(
        kern,
        out_shape=jax.ShapeDtypeStruct((out_rows, d), F32),
        grid=(n_tiles, nf),
        in_specs=in_specs,
        out_specs=pl.BlockSpec((tm, d), lambda i, j: (i + tile0, 0)),
        scratch_shapes=[pltpu.VMEM((tm, d), BF16)],
        input_output_aliases=aliases,
        compiler_params=_cparams(2),
        name="half_ffn",
    )(*args)


def _cast_kernel(*refs):
    n = len(refs) // 2
    for src, dst in zip(refs[:n], refs[n:]):
        dst[...] = src[...].astype(dst.dtype)


def _largest_tile(n, multiple, cap):
    return max(t for t in range(multiple, min(n, cap) + 1, multiple) if n % t == 0)


def _cast_ffn_weights(w_in, w_out):
    depth, n_ffn, d, f2 = w_in.shape
    f = f2 // 2
    assert f % LANES == 0
    rows = _largest_tile(d, 2 * SUBLANES, 256)
    half = (None, None, rows, f)
    wg, wu = pl.pallas_call(
        _cast_kernel,
        out_shape=[jax.ShapeDtypeStruct((depth, n_ffn, d, f), BF16)] * 2,
        grid=(depth, n_ffn, d // rows),
        in_specs=[pl.BlockSpec(half, lambda l, w, r: (l, w, r, 0)), pl.BlockSpec(half, lambda l, w, r: (l, w, r, 1))],
        out_specs=[pl.BlockSpec(half, lambda l, w, r: (l, w, r, 0))] * 2,
        compiler_params=_cparams(3),
        name="cast_ffn_in",
    )(w_in, w_in)
    rows = _largest_tile(f, 2 * SUBLANES, 1024)
    blk = (None, None, rows, d)
    wo = pl.pallas_call(
        _cast_kernel,
        out_shape=jax.ShapeDtypeStruct(w_out.shape, BF16),
        grid=(depth, n_ffn, f // rows),
        in_specs=[pl.BlockSpec(blk, lambda l, w, r: (l, w, r, 0))],
        out_specs=pl.BlockSpec(blk, lambda l, w, r: (l, w, r, 0)),
        compiler_params=_cparams(3),
        name="cast_ffn_out",
    )(w_out)
    return wg, wu, wo


PROJ_CHUNK = 4 * HEAD_DIM


def _swap_halves_32(y):
    lane = lax.broadcasted_iota(jnp.int32, y.shape, 1)
    return jnp.where((lane & 32) == 0, pltpu.roll(y, LANES - 32, 1), pltpu.roll(y, 32, 1))


def _inproj_kernel(x_ref, g_ref, sh_ref, sc_ref, w_ref, gains_ref, cos_ref, sin_ref, *rest,
                   plan, tm, rows_per_batch, n_batch, n_lat_tiles):
    out_refs, h_scr = rest[:-1], rest[-1]
    i = pl.program_id(0)
    r = _mod_row(i, tm, rows_per_batch, n_batch)
    is_lat = i < n_lat_tiles
    _norm_modulate_into(h_scr, x_ref, g_ref[...], sh_ref[pl.ds(r, 1), :], sc_ref[pl.ds(r, 1), :])
    for w_col, width, out_idx, out_col, ops in plan:
        y = jnp.dot(h_scr[...], w_ref[:, w_col:w_col + width], preferred_element_type=F32)
        o_ref = out_refs[out_idx]
        if ops is None:
            o_ref[:, out_col:out_col + width] = y.astype(o_ref.dtype)
            continue
        for hh, op in enumerate(ops):
            yh = y[:, hh * HEAD_DIM:(hh + 1) * HEAD_DIM]
            if op is not None:
                gain_row, rope = op
                yh = yh * lax.rsqrt(jnp.mean(yh * yh, axis=-1, keepdims=True) + EPS)
                yh = yh * gains_ref[gain_row:gain_row + 1, :]
                if rope:
                    yh = jnp.where(is_lat, yh * cos_ref[...] + _swap_halves_32(yh) * sin_ref[...], yh)
            o_ref[:, out_col + hh * HEAD_DIM:out_col + (hh + 1) * HEAD_DIM] = yh.astype(o_ref.dtype)


def _inproj(x_all, mod, layer, g, w, gains, cos_t, sin_t, plan, outs, *, tm, n_lat_rows, rows_per_batch, n_batch):
    ta, d = x_all.shape
    n_pos_tiles = cos_t.shape[0] // tm
    kern = functools.partial(_inproj_kernel, plan=tuple(plan), tm=tm, rows_per_batch=rows_per_batch,
                             n_batch=n_batch, n_lat_tiles=n_lat_rows // tm)
    return pl.pallas_call(
        kern,
        out_shape=[jax.ShapeDtypeStruct((ta, n), dt) for n, dt in outs],
        grid=(ta // tm,),
        in_specs=[pl.BlockSpec((tm, d), lambda i: (i, 0)),
                  pl.BlockSpec((1, d), lambda i: (0, 0)),
                  _mod_spec(d, layer, 3, 1), _mod_spec(d, layer, 4, 1),
                  pl.BlockSpec(w.shape, lambda i: (0, 0), pipeline_mode=pl.Buffered(1)),
                  pl.BlockSpec(gains.shape, lambda i: (0, 0)),
                  pl.BlockSpec((tm, HEAD_DIM), lambda i: (i % n_pos_tiles, 0)),
                  pl.BlockSpec((tm, HEAD_DIM), lambda i: (i % n_pos_tiles, 0))],
        out_specs=[pl.BlockSpec((tm, n), lambda i: (i, 0)) for n, _ in outs],
        scratch_shapes=[pltpu.VMEM((tm, d), BF16)],
        compiler_params=_cparams(1),
        name="mixer_inproj",
    )(x_all, g.reshape(1, d), mod, mod, w, gains, cos_t, sin_t)


def _outproj_kernel(a1_ref, a2_ref, w1_ref, w2_ref, x_ref, gt_ref, o_ref, *, tm, rows_per_batch, n_batch):
    r = _mod_row(pl.program_id(0), tm, rows_per_batch, n_batch)
    d = o_ref.shape[1]
    for c0 in range(0, d, PROJ_CHUNK):
        cs = slice(c0, min(c0 + PROJ_CHUNK, d))
        y = (jnp.dot(a1_ref[...], w1_ref[:, cs], preferred_element_type=F32)
             + jnp.dot(a2_ref[...], w2_ref[:, cs], preferred_element_type=F32))
        o_ref[:, cs] = x_ref[:, cs] + gt_ref[pl.ds(r, 1), cs] * y


def _outproj(a1, a2, a2_col_block, w, x_all, n_rows, mod, layer, *, tm, rows_per_batch, n_batch):
    d = x_all.shape[1]
    kh = w.shape[0] // 2
    kern = functools.partial(_outproj_kernel, tm=tm, rows_per_batch=rows_per_batch, n_batch=n_batch)
    return pl.pallas_call(
        kern,
        out_shape=jax.ShapeDtypeStruct((n_rows, d), F32),
        grid=(n_rows // tm,),
        in_specs=[pl.BlockSpec((tm, kh), lambda i: (i, 0)),
                  pl.BlockSpec((tm, kh), lambda i: (i, a2_col_block)),
                  pl.BlockSpec((kh, d), lambda i: (0, 0)),
                  pl.BlockSpec((kh, d), lambda i: (1, 0)),
                  pl.BlockSpec((tm, d), lambda i: (i, 0)),
                  _mod_spec(d, layer, 5, 1)],
        out_specs=pl.BlockSpec((tm, d), lambda i: (i, 0)),
        compiler_params=_cparams(1),
        name="mixer_outproj",
    )(a1, a2, w, w, x_all, mod)


def _rope_tables(seq_len):
    half = HEAD_DIM // 2
    nf = half // 2
    inv = ROPE_THETA ** (-jnp.arange(nf, dtype=F32) / nf)
    pos = jnp.arange(seq_len, dtype=jnp.int32)
    ang_r = (pos // GRID_W).astype(F32)[:, None] * inv[None, :]
    ang_c = (pos % GRID_W).astype(F32)[:, None] * inv[None, :]
    cr, sr, cc, sc = jnp.cos(ang_r), jnp.sin(ang_r), jnp.cos(ang_c), jnp.sin(ang_c)
    return (jnp.concatenate([cr, cr, cc, cc], axis=-1),
            jnp.concatenate([-sr, sr, -sc, sc], axis=-1))


def _gelu_tanh(x):
    return x * (0.5 * (1.0 + jnp.tanh(np.float32(np.sqrt(2.0 / np.pi)) * (x + 0.044715 * (x * x * x)))))


def _lru_kernel(*refs, tt, reverse, final, n_lat_chunks):
    if final:
        (xp_ref, xc_ref, xn_ref, cw_ref, cb_ref, wa_ref, ba_ref, wx_ref, bx_ref, lam_ref,
         hf_ref, gl_ref, o_ref, xs_scr, a_scr, b_scr, carry_scr) = refs
    else:
        (xp_ref, xc_ref, xn_ref, cw_ref, cb_ref, wa_ref, ba_ref, wx_ref, bx_ref, lam_ref,
         o_ref, xs_scr, a_scr, b_scr, carry_scr) = refs
    s = pl.program_id(1)
    is_ctx = s == 0
    j = (n_lat_chunks - s) if reverse else (s - 1)
    has_prev = jnp.logical_and(jnp.logical_not(is_ctx), j > 0)
    has_next = jnp.logical_and(jnp.logical_not(is_ctx), j < n_lat_chunks - 1)

    @pl.when(is_ctx)
    def _():
        carry_scr[...] = jnp.zeros_like(carry_scr)

    halo = SUBLANES
    xs_scr[0:halo, :] = jnp.where(has_prev, xp_ref[tt - halo:tt, :], 0.0)
    xs_scr[halo:halo + tt, :] = xc_ref[...]
    xs_scr[halo + tt:2 * halo + tt, :] = jnp.where(has_next, xn_ref[0:halo, :], 0.0)

    width = xc_ref.shape[1]
    left = CONV_W // 2
    z = -lam_ref[...]
    softplus = jnp.maximum(z, 0.0) + jnp.log1p(jnp.exp(-jnp.abs(z)))
    for n in range(width // HEAD_DIM):
        cs = slice(n * HEAD_DIM, (n + 1) * HEAD_DIM)
        u = cb_ref[:, cs]
        xa = xs_scr[:, cs]
        n_rows = xa.shape[0]
        for k in range(CONV_W):
            xk = xa if k == left else pltpu.roll(xa, (left - k) % n_rows, 0)
            u = u + xk[halo:halo + tt, :] * cw_ref[k:k + 1, cs]
        ub = u.astype(BF16)
        gate_r = jnp.dot(ub, wa_ref[n], preferred_element_type=F32) + ba_ref[:, cs]
        gate_i = jnp.dot(ub, wx_ref[n], preferred_element_type=F32) + bx_ref[:, cs]
        log_a = (-LRU_C) * _sigmoid_tanh(gate_r) * softplus[:, cs]
        a = jnp.exp(log_a)
        a_scr[:, cs] = a
        b_scr[:, cs] = jnp.sqrt(-jnp.tanh(log_a) * (1.0 + a * a)) * (_sigmoid_tanh(gate_i) * u)

    n_groups = tt // SUBLANES
    row = lax.broadcasted_iota(jnp.int32, (SUBLANES, width), 0)

    def group(gi, carry):
        g = (n_groups - 1 - gi) if reverse else gi
        r0 = pl.multiple_of(g * SUBLANES, SUBLANES)
        a = a_scr[pl.ds(r0, SUBLANES), :]
        b = b_scr[pl.ds(r0, SUBLANES), :]
        for k in (1, 2, 4):
            if reverse:
                keep = row < SUBLANES - k
                shift = SUBLANES - k
            else:
                keep = row >= k
                shift = k
            a_sh = jnp.where(keep, pltpu.roll(a, shift, 0), 1.0)
            b_sh = jnp.where(keep, pltpu.roll(b, shift, 0), 0.0)
            b = a * b_sh + b
            a = a * a_sh
        h = b + a * carry
        b_scr[pl.ds(r0, SUBLANES), :] = h
        last = h[0:1, :] if reverse else h[SUBLANES - 1:SUBLANES, :]
        return jnp.broadcast_to(last, (SUBLANES, width))

    carry_scr[...] = lax.fori_loop(0, n_groups, group, carry_scr[...])

    if final:
        o_ref[...] = ((hf_ref[...] + b_scr[...]) * _gelu_tanh(gl_ref[...])).astype(o_ref.dtype)
    else:
        o_ref[...] = b_scr[...]


def _lru_pass(p, conv_w, conv_b, w_a, b_a, w_x, b_x, lam, hf, *, reverse, n_batch, seq_len, ctx_len, tt):
    ta = p.shape[0]
    width = conv_w.shape[1]
    assert ctx_len == tt and seq_len % tt == 0
    n_lat_chunks = seq_len // tt
    ctx_block0 = n_batch * n_lat_chunks
    final = hf is not None

    def chunk_index(b, s, off):
        j = (n_lat_chunks - s) if reverse else (s - 1)
        j = jnp.clip(j + off, 0, n_lat_chunks - 1)
        return jnp.where(s == 0, ctx_block0 + b, b * n_lat_chunks + j)

    def xspec(off):
        return pl.BlockSpec((tt, width), lambda b, s: (chunk_index(b, s, off), 0))

    row_spec = pl.BlockSpec((1, width), lambda b, s: (0, 0))
    w_spec = pl.BlockSpec(w_a.shape, lambda b, s: (0, 0, 0))
    in_specs = [xspec(-1), xspec(0), xspec(1),
                pl.BlockSpec((CONV_W, width), lambda b, s: (0, 0)), row_spec,
                w_spec, row_spec, w_spec, row_spec, row_spec]
    args = [p, p, p, conv_w, conv_b.reshape(1, width), w_a, b_a.reshape(1, width),
            w_x, b_x.reshape(1, width), lam.reshape(1, width)]
    if final:
        in_specs += [xspec(0), pl.BlockSpec((tt, width), lambda b, s: (chunk_index(b, s, 0), 1))]
        args += [hf, p]
    kern = functools.partial(_lru_kernel, tt=tt, reverse=reverse, final=final, n_lat_chunks=n_lat_chunks)
    return pl.pallas_call(
        kern,
        out_shape=jax.ShapeDtypeStruct((ta, width), BF16 if final else F32),
        grid=(n_batch, n_lat_chunks + 1),
        in_specs=in_specs,
        out_specs=xspec(0),
        scratch_shapes=[pltpu.VMEM((tt + 2 * SUBLANES, width), F32), pltpu.VMEM((tt, width), F32),
                        pltpu.VMEM((tt, width), F32), pltpu.VMEM((SUBLANES, width), F32)],
        compiler_params=_cparams(2),
        name="rglru_rev" if reverse else "rglru_fwd",
    )(*args)


LOG2E = float(np.log2(np.e))
LOGIT_SCALE = ATTN_SCALE * LOG2E


def _softmax_parts(s2, extra_logit2=None):
    m = jnp.max(s2, axis=-1, keepdims=True)
    if extra_logit2 is not None:
        m = jnp.maximum(m, extra_logit2)
    e = jnp.exp2(s2 - m)
    denom = jnp.sum(e, axis=-1, keepdims=True)
    if extra_logit2 is not None:
        denom = denom + jnp.exp2(extra_logit2 - m)
    return e, 1.0 / denom


def _swa_kernel(sink_ref, q_ref, kp_ref, kc_ref, kn_ref, kx_ref, vp_ref, vc_ref, vn_ref, vx_ref, mask_ref, o_ref,
                *, n_kv_heads):
    blk = q_ref.shape[0]
    head = lax.broadcasted_iota(jnp.int32, (GQA_GROUP * blk, 1), 0) // blk
    logits = []
    for kh in range(n_kv_heads):
        ks = slice(kh * HEAD_DIM, (kh + 1) * HEAD_DIM)
        q0 = kh * GQA_GROUP
        q = jnp.concatenate([q_ref[:, (q0 + h) * HEAD_DIM:(q0 + h + 1) * HEAD_DIM] for h in range(GQA_GROUP)],
                            axis=0)
        k = jnp.concatenate([kp_ref[:, ks], kc_ref[:, ks], kn_ref[:, ks], kx_ref[:, ks]], axis=0)
        logits.append(lax.dot_general(q, k, (((1,), (1,)), ((), ())), preferred_element_type=F32))
    for kh in range(n_kv_heads):
        ks = slice(kh * HEAD_DIM, (kh + 1) * HEAD_DIM)
        q0 = kh * GQA_GROUP
        v = jnp.concatenate([vp_ref[:, ks], vc_ref[:, ks], vn_ref[:, ks], vx_ref[:, ks]], axis=0)
        s = logits[kh] * LOGIT_SCALE + mask_ref[...]
        sink = jnp.zeros((GQA_GROUP * blk, 1), F32)
        for h in range(GQA_GROUP):
            sink = jnp.where(head == h, sink_ref[q0 + h] * LOG2E, sink)
        e, inv = _softmax_parts(s, sink)
        o = jnp.dot(e.astype(BF16), v, preferred_element_type=F32) * inv
        for h in range(GQA_GROUP):
            o_ref[:, (q0 + h) * HEAD_DIM:(q0 + h + 1) * HEAD_DIM] = o[h * blk:(h + 1) * blk, :].astype(o_ref.dtype)


def _swa_mask_table(blk, ctx_len):
    shape = (4, GQA_GROUP * blk, 3 * blk + ctx_len)
    cls = lax.broadcasted_iota(jnp.int32, shape, 0)
    qi = lax.broadcasted_iota(jnp.int32, shape, 1) % blk
    col = lax.broadcasted_iota(jnp.int32, shape, 2)
    lo = jnp.where(cls == 0, blk, 0)
    hi = jnp.where(cls == 3, 0, jnp.where(cls == 2, 2 * blk, 3 * blk))
    rel = col - blk - qi
    ok = ((col >= lo) & (col < hi) & (rel >= -WINDOW) & (rel <= WINDOW)) | (col >= 3 * blk)
    return jnp.where(ok, 0.0, NEG).astype(F32)


def _swa(qkv, sink, *, n_batch, seq_len, ctx_len, n_kv_heads, blk):
    ta = qkv.shape[0]
    n_heads = n_kv_heads * GQA_GROUP
    n_lat_blocks = seq_len // blk
    n_ctx_blocks = ctx_len // blk
    assert blk == WINDOW and seq_len % blk == 0 and ctx_len % blk == 0 and n_lat_blocks >= 2
    ctx_q0 = n_batch * n_lat_blocks
    ctx_kv0 = n_batch * seq_len // ctx_len
    kv_w = n_kv_heads * HEAD_DIM
    k_col, v_col = n_heads * HEAD_DIM // kv_w, n_heads * HEAD_DIM // kv_w + 1

    def q_index(b, n):
        return (jnp.where(n < n_lat_blocks, b * n_lat_blocks + n, ctx_q0 + b * n_ctx_blocks + n - n_lat_blocks), 0)

    def kv_spec(off, col):
        return pl.BlockSpec((blk, kv_w),
                            lambda b, n: (b * n_lat_blocks + jnp.clip(n + off, 0, n_lat_blocks - 1), col))

    def ctx_spec(col):
        return pl.BlockSpec((ctx_len, kv_w), lambda b, n: (ctx_kv0 + b, col))

    def mask_index(b, n):
        interior = jnp.where(n == 0, 0, jnp.where(n == n_lat_blocks - 1, 2, 1))
        return (jnp.where(n < n_lat_blocks, interior, 3), 0, 0)

    mask = _swa_mask_table(blk, ctx_len)
    kern = functools.partial(_swa_kernel, n_kv_heads=n_kv_heads)
    return pl.pallas_call(
        kern,
        out_shape=jax.ShapeDtypeStruct((ta, n_heads * HEAD_DIM), BF16),
        grid=(n_batch, n_lat_blocks + n_ctx_blocks),
        in_specs=[pl.BlockSpec(memory_space=pltpu.SMEM),
                  pl.BlockSpec((blk, n_heads * HEAD_DIM), q_index),
                  kv_spec(-1, k_col), kv_spec(0, k_col), kv_spec(1, k_col), ctx_spec(k_col),
                  kv_spec(-1, v_col), kv_spec(0, v_col), kv_spec(1, v_col), ctx_spec(v_col),
                  pl.BlockSpec((None,) + mask.shape[1:], mask_index)],
        out_specs=pl.BlockSpec((blk, n_heads * HEAD_DIM), q_index),
        compiler_params=_cparams(2),
        name="windowed_gqa",
    )(sink, *([qkv] * 9), mask)


NA_HEADS_PER_STEP = 8


def _na_kernel(q_ref, kvp_ref, kvc_ref, kvn_ref, kvx_ref, bias_ref, o_ref):
    n_local = bias_ref.shape[-1]
    gw = NA_HEADS_PER_STEP * HEAD_DIM
    kv_refs = (kvp_ref, kvc_ref, kvn_ref, kvx_ref)
    heads = [slice(h * HEAD_DIM, (h + 1) * HEAD_DIM) for h in range(NA_HEADS_PER_STEP)]
    logits = []
    for hs in heads:
        k = jnp.concatenate([ref[:, hs] for ref in kv_refs], axis=0)
        logits.append(lax.dot_general(q_ref[:, hs], k, (((1,), (1,)), ((), ())), preferred_element_type=F32))
    weights = []
    for h, hs in enumerate(heads):
        s = logits[h] * LOGIT_SCALE
        s = jnp.concatenate([s[:, :n_local] + bias_ref[h], s[:, n_local:]], axis=1)
        e, inv = _softmax_parts(s)
        weights.append((e.astype(BF16), inv))
    for (e, inv), hs in zip(weights, heads):
        vs = slice(gw + hs.start, gw + hs.stop)
        v = jnp.concatenate([ref[:, vs] for ref in kv_refs], axis=0)
        o_ref[:, hs] = (jnp.dot(e, v, preferred_element_type=F32) * inv).astype(o_ref.dtype)


def _na_bias_table(rpb, n_grid_rows):
    r_in, w = NA_ROWS_PER_BLOCK, GRID_W
    kh = min(NA_KH, n_grid_rows)
    n_blocks = n_grid_rows // r_in
    n_ro, n_co = 2 * NA_KH - 1, 2 * NA_KW - 1
    qc, kc = np.arange(w)[:, None], np.arange(w)[None, :]
    e_col = (kc - qc + NA_KW - 1 == np.arange(n_co)[:, None, None]).astype(np.float32)
    win_start = np.clip(qc - NA_KW // 2, 0, w - NA_KW)
    col_ok = (kc >= win_start) & (kc < win_start + NA_KW)
    ri, krj = np.arange(r_in)[:, None], np.arange(3 * r_in)[None, :]
    row_ok = []
    for jb in (0, max(n_blocks // 2, 1) if n_blocks > 2 else 0, n_blocks - 1):
        r, kr = r_in * jb + ri, r_in * (jb - 1) + krj
        rs = np.clip(r - kh // 2, 0, n_grid_rows - kh)
        row_ok.append((kr >= rs) & (kr < rs + kh))
    ok = np.stack(row_ok)[:, :, None, :, None] & col_ok[None, None, :, None, :]
    ok = ok.reshape(3, r_in * w, 3 * r_in * w)
    tz = jnp.einsum('hab,bqc->haqc', rpb.astype(F32), e_col, precision=lax.Precision.HIGHEST) * LOG2E
    assert r_in - 1 <= NA_KH - 1 - r_in and 3 * r_in - 1 - r_in + NA_KH - 1 < n_ro
    full = jnp.concatenate(
        [jnp.concatenate([tz[:, k - i - r_in + NA_KH - 1] for k in range(3 * r_in)], axis=-1) for i in range(r_in)],
        axis=-2)
    return jnp.where(ok[None], full[:, None], NEG)


def _na(qkv, bias, *, n_batch, seq_len, ctx_len, n_heads):
    tq = NA_ROWS_PER_BLOCK * GRID_W
    n_blocks = seq_len // tq
    assert ctx_len == tq and seq_len % tq == 0 and NA_KH == 2 * NA_ROWS_PER_BLOCK and n_blocks >= 2
    assert n_heads % NA_HEADS_PER_STEP == 0
    n_groups = n_heads // NA_HEADS_PER_STEP
    gw = NA_HEADS_PER_STEP * HEAD_DIM
    ctx0 = n_batch * n_blocks

    def kv_spec(off):
        return pl.BlockSpec((tq, 2 * gw),
                            lambda b, g, jb: (b * n_blocks + jnp.clip(jb + off, 0, n_blocks - 1), g))

    def bias_index(b, g, jb):
        return (g, jnp.where(jb == 0, 0, jnp.where(jb == n_blocks - 1, 2, 1)), 0, 0)

    return pl.pallas_call(
        _na_kernel,
        out_shape=jax.ShapeDtypeStruct((n_batch * seq_len, n_heads * HEAD_DIM), BF16),
        grid=(n_batch, n_groups, n_blocks),
        in_specs=[pl.BlockSpec((tq, gw), lambda b, g, jb: (b * n_blocks + jb, 2 * n_groups + g)),
                  kv_spec(-1), kv_spec(0), kv_spec(1),
                  pl.BlockSpec((ctx_len, 2 * gw), lambda b, g, jb: (ctx0 + b, g)),
                  pl.BlockSpec((NA_HEADS_PER_STEP, None) + bias.shape[2:], bias_index)],
        out_specs=pl.BlockSpec((tq, gw), lambda b, g, jb: (b * n_blocks + jb, g)),
        compiler_params=_cparams(3),
        name="neighbourhood_attn",
    )(*([qkv] * 5), bias)


def _tiles():
    return dict(tm=1024, tm_proj=512, tm_out=512, tf=512, tn_mod=1024)


def _chunked(total, out_idx, out_col0=0, w_col0=0):
    return [(w_col0 + c, min(PROJ_CHUNK, total - c), out_idx, out_col0 + c, None)
            for c in range(0, total, PROJ_CHUNK)]


def _head_chunks(head_ops, w_col0, out_idx, out_col0):
    per = PROJ_CHUNK // HEAD_DIM
    return [(w_col0 + h0 * HEAD_DIM, len(head_ops[h0:h0 + per]) * HEAD_DIM, out_idx, out_col0 + h0 * HEAD_DIM,
             tuple(head_ops[h0:h0 + per])) for h0 in range(0, len(head_ops), per)]


def kernel(x, c, ctx, c_ctx, w_mod, b_mod, norm_g, ffn_w_in, ffn_w_out, ab_w_in, lru_conv_w, lru_conv_b,
           lru_w_a, lru_b_a, lru_w_x, lru_b_x, lru_lambda, attn_q_norm, attn_k_norm, attn_sink, ab_w_out,
           na_w_in, na_q_norm, na_k_norm, na_rpb, na_w_out):
    n_batch, seq_len, d = x.shape
    ctx_len = ctx.shape[1]
    depth = w_mod.shape[0]
    lru_w = lru_conv_w.shape[2]
    n_kv = attn_sink.shape[1] // GQA_GROUP
    n_att = n_kv * GQA_GROUP
    na_heads = na_rpb.shape[1]
    assert n_batch < MOD_ROWS and depth == 2
    t = _tiles()
    tl, tc = n_batch * seq_len, n_batch * ctx_len
    geom = dict(rows_per_batch=seq_len, n_batch=n_batch)

    cc = jnp.zeros((MOD_ROWS, d), F32).at[:n_batch].set(c).at[n_batch].set(c_ctx)
    mod = _mod_table(cc, w_mod, b_mod, t["tn_mod"])
    wg, wu, wo = _cast_ffn_weights(ffn_w_in, ffn_w_out)
    n_lat_tiles, n_ctx_tiles = tl // t["tm"], tc // t["tm"]

    def ffn(x_src, n_tiles, out_rows, layer, which, tile0=0, dst=None):
        return _ffn(x_src, n_tiles, out_rows, tile0, dst, mod, layer, which, norm_g[layer, 2 * which], wg, wu, wo,
                    tm=t["tm"], tf=t["tf"], **geom)

    cos_t, sin_t = _rope_tables(seq_len)
    proj_geom = dict(tm=t["tm_proj"], n_lat_rows=tl, **geom)

    x_all = ffn(x.reshape(tl, d), n_lat_tiles, tl + tc, 0, 0)
    x_all = ffn(ctx.reshape(tc, d), n_ctx_tiles, tl + tc, 0, 0, tile0=n_lat_tiles, dst=x_all)
    gains = jnp.stack([attn_q_norm[0], attn_k_norm[0]])
    head_ops = [(0, True)] * n_att + [(1, True)] * n_kv + [None] * n_kv
    plan = _head_chunks(head_ops, 2 * lru_w, 1, 0) + _chunked(2 * lru_w, 0)
    p, qkv = _inproj(x_all, mod, 0, norm_g[0, 1], ab_w_in[0].astype(BF16), gains, cos_t, sin_t, plan,
                     ((2 * lru_w, F32), (len(head_ops) * HEAD_DIM, BF16)), **proj_geom)
    lru_args = lambda dr: (lru_conv_w[0], lru_conv_b[0], lru_w_a[0, dr].astype(BF16), lru_b_a[0, dr],
                           lru_w_x[0, dr].astype(BF16), lru_b_x[0, dr], lru_lambda[0, dr])
    scan_geom = dict(n_batch=n_batch, seq_len=seq_len, ctx_len=ctx_len, tt=ctx_len)
    h_fwd = _lru_pass(p, *lru_args(0), None, reverse=False, **scan_geom)
    lru = _lru_pass(p, *lru_args(1), h_fwd, reverse=True, **scan_geom)
    att = _swa(qkv, attn_sink[0], n_batch=n_batch, seq_len=seq_len, ctx_len=ctx_len, n_kv_heads=n_kv,
               blk=WINDOW)
    x_all = _outproj(lru, att, 0, ab_w_out[0].astype(BF16), x_all, tl + tc, mod, 0, tm=t["tm_out"], **geom)
    x_all = ffn(x_all, n_lat_tiles + n_ctx_tiles, tl + tc, 0, 1)

    x_all = ffn(x_all, n_lat_tiles + n_ctx_tiles, tl + tc, 1, 0)
    gains = jnp.stack([na_q_norm[0], na_k_norm[0]])
    na_d = na_heads * HEAD_DIM
    gw = NA_HEADS_PER_STEP * HEAD_DIM
    groups = range(0, na_d, gw)
    plan = sum([_head_chunks([(0, False)] * NA_HEADS_PER_STEP, g0, 0, 2 * na_d + g0) for g0 in groups], [])
    plan += sum([_head_chunks([(1, False)] * NA_HEADS_PER_STEP, na_d + g0, 0, 2 * g0) for g0 in groups], [])
    plan += sum([_head_chunks([None] * NA_HEADS_PER_STEP, 2 * na_d + g0, 0, 2 * g0 + gw) for g0 in groups], [])
    (qkv,) = _inproj(x_all, mod, 1, norm_g[1, 1], na_w_in[0].astype(BF16), gains, cos_t, sin_t, plan,
                     ((3 * na_d, BF16),), **proj_geom)
    o = _na(qkv, _na_bias_table(na_rpb[0], seq_len // GRID_W), n_batch=n_batch, seq_len=seq_len, ctx_len=ctx_len,
            n_heads=na_heads)
    x_lat = _outproj(o, o, 1, na_w_out[0].astype(BF16), x_all, tl, mod, 1, tm=t["tm_out"], **geom)
    x_lat = ffn(x_lat, n_lat_tiles, tl, 1, 1)
    return x_lat.reshape(n_batch, seq_len, d)
```

```python
import functools

import numpy as np
import jax
import jax.numpy as jnp
from jax import lax
from jax.experimental import pallas as pl
from jax.experimental.pallas import tpu as pltpu

HEAD_DIM = 128
EPS = 1e-6
N_MOD = 9
GRID_W = 64
CONV_W = 4
LRU_C = 8.0
WINDOW = 128
ROPE_THETA = 10000.0
NA_KH = 8
NA_KW = 16
ATTN_SCALE = HEAD_DIM ** -0.5
GQA_GROUP = 4

LANES = 128
SUBLANES = 8
VMEM_LIMIT_BYTES = 56 * 1024 * 1024
MOD_ROWS = SUBLANES

NEG = -1e30
NA_ROWS_PER_BLOCK = 4

BF16 = jnp.bfloat16
F32 = jnp.float32


def _cparams(n_axes):
    return pltpu.CompilerParams(dimension_semantics=("arbitrary",) * n_axes,
                                vmem_limit_bytes=VMEM_LIMIT_BYTES)


def _sigmoid(x):
    return 1.0 / (1.0 + jnp.exp(-x))


def _sigmoid_tanh(x):
    return 0.5 * jnp.tanh(0.5 * x) + 0.5


NORM_CHUNK_ROWS = 16


def _norm_modulate_into(h_scr, x_ref, g, shift, scale, alt=None):
    gain = g * (1.0 + scale)

    def chunk(c, carry):
        r0 = pl.multiple_of(c * NORM_CHUNK_ROWS, NORM_CHUNK_ROWS)
        x = x_ref[pl.ds(r0, NORM_CHUNK_ROWS), :]
        if alt is not None:
            x = jnp.where(alt[0], alt[1][pl.ds(r0, NORM_CHUNK_ROWS), :], x)
        inv = lax.rsqrt(jnp.mean(x * x, axis=-1, keepdims=True) + EPS)
        h_scr[pl.ds(r0, NORM_CHUNK_ROWS), :] = ((x * inv) * gain + shift).astype(h_scr.dtype)
        return carry

    lax.fori_loop(0, x_ref.shape[0] // NORM_CHUNK_ROWS, chunk, 0, unroll=4)


def _mod_kernel(c_ref, w_ref, b_ref, o_ref):
    c = c_ref[...]
    s = c * _sigmoid(c)
    o_ref[...] = jnp.dot(s.astype(BF16), w_ref[...].astype(BF16),
                         preferred_element_type=F32) + b_ref[...]


def _mod_table(cc, w_mod, b_mod, tn):
    depth, d, n = w_mod.shape
    return pl.pallas_call(
        _mod_kernel,
        out_shape=jax.ShapeDtypeStruct((depth, MOD_ROWS, n), F32),
        grid=(depth, n // tn),
        in_specs=[pl.BlockSpec((MOD_ROWS, d), lambda l, j: (0, 0)),
                  pl.BlockSpec((None, d, tn), lambda l, j: (l, 0, j)),
                  pl.BlockSpec((None, 1, tn), lambda l, j: (l, 0, j))],
        out_specs=pl.BlockSpec((None, MOD_ROWS, tn), lambda l, j: (l, 0, j)),
        compiler_params=_cparams(2),
        name="mod_table",
    )(cc, w_mod, b_mod.reshape(depth, 1, n))


def _mod_spec(d, layer, k, n_grid_axes):
    if n_grid_axes == 1:
        return pl.BlockSpec((None, MOD_ROWS, d), lambda i: (layer, 0, k))
    return pl.BlockSpec((None, MOD_ROWS, d), lambda i, j: (layer, 0, k))


def _mod_row(tile_idx, tm, rows_per_batch, n_batch):
    return jnp.minimum(tile_idx * tm // rows_per_batch, n_batch)


def _ffn_kernel(x_ref, g_ref, sh_ref, sc_ref, gt_ref, wg_hbm, wu_hbm, wo_hbm, *rest,
                tm, tf, n_tiles1, layer, which, rows_per_batch, n_batch, widths):
    o_ref, h_scr, wg_buf, wu_buf, wo_buf, sems = rest[-6:]
    i = pl.program_id(0)
    r = _mod_row(i, tm, rows_per_batch, n_batch)
    alt = (i >= n_tiles1, rest[0]) if n_tiles1 is not None else None

    def weight_copies(j):
        slot, c0, w = j % 2, j * tf, widths[j]
        return (pltpu.make_async_copy(wg_hbm.at[layer, which, :, pl.ds(c0, w)], wg_buf.at[slot, :, pl.ds(0, w)],
                                      sems.at[0, slot]),
                pltpu.make_async_copy(wu_hbm.at[layer, which, :, pl.ds(c0, w)], wu_buf.at[slot, :, pl.ds(0, w)],
                                      sems.at[1, slot]),
                pltpu.make_async_copy(wo_hbm.at[layer, which, pl.ds(c0, w), :], wo_buf.at[slot, pl.ds(0, w), :],
                                      sems.at[2, slot]))

    @pl.when(i == 0)
    def _():
        for cp in weight_copies(0):
            cp.start()

    _norm_modulate_into(h_scr, x_ref, g_ref[...], sh_ref[pl.ds(r, 1), :], sc_ref[pl.ds(r, 1), :], alt)

    for j, w in enumerate(widths):
        slot = j % 2
        if j + 1 < len(widths):
            for cp in weight_copies(j + 1):
                cp.start()
        for cp in weight_copies(j):
            cp.wait()
        h = h_scr[...]
        a = jnp.dot(h, wg_buf[slot, :, :w], preferred_element_type=F32)
        u = jnp.dot(h, wu_buf[slot, :, :w], preferred_element_type=F32)
        act = ((a * _sigmoid(a)) * u).astype(BF16)
        y = jnp.dot(act, wo_buf[slot, :w, :], preferred_element_type=F32)
        if j == 0:
            o_ref[...] = y
        elif j + 1 < len(widths):
            o_ref[...] += y
        else:
            x = x_ref[...] if alt is None else jnp.where(alt[0], alt[1][...], x_ref[...])
            o_ref[...] = x + 0.5 * gt_ref[pl.ds(r, 1), :] * (o_ref[...] + y)

    @pl.when(i + 1 < pl.num_programs(0))
    def _():
        for cp in weight_copies(0):
            cp.start()


def _ffn(x_src, x_src2, n_rows, mod, layer, which, g, wg, wu, wo, *, tm, tf, rows_per_batch, n_batch):
    d = x_src.shape[1]
    n_tiles = n_rows // tm
    n_tiles1 = None if x_src2 is None else x_src.shape[0] // tm
    d_ff = wo.shape[2]
    nf = pl.cdiv(d_ff, tf)
    widths = (tf,) * (nf - 1) + (d_ff - (nf - 1) * tf,)
    assert widths[-1] % LANES == 0 and nf >= 2
    k0 = 6 * which
    kern = functools.partial(_ffn_kernel, tm=tm, tf=tf, n_tiles1=n_tiles1, layer=layer, which=which,
                             rows_per_batch=rows_per_batch, n_batch=n_batch, widths=widths)
    x_index = (lambda i: (i, 0)) if x_src2 is None else (lambda i: (jnp.minimum(i, n_tiles1 - 1), 0))
    in_specs = [pl.BlockSpec((tm, d), x_index),
                pl.BlockSpec((1, d), lambda i: (0, 0)),
                _mod_spec(d, layer, k0, 1), _mod_spec(d, layer, k0 + 1, 1), _mod_spec(d, layer, k0 + 2, 1),
                pl.BlockSpec(memory_space=pl.ANY), pl.BlockSpec(memory_space=pl.ANY),
                pl.BlockSpec(memory_space=pl.ANY)]
    args = [x_src, g.reshape(1, d), mod, mod, mod, wg, wu, wo]
    if x_src2 is not None:
        in_specs.append(pl.BlockSpec((tm, d), lambda i: (jnp.maximum(i - n_tiles1, 0), 0)))
        args.append(x_src2)
    return pl.pallas_call(
        kern,
        out_shape=jax.ShapeDtypeStruct((n_rows, d), F32),
        grid=(n_tiles,),
        in_specs=in_specs,
        out_specs=pl.BlockSpec((tm, d), lambda i: (i, 0)),
        scratch_shapes=[pltpu.VMEM((tm, d), BF16), pltpu.VMEM((2, d, tf), BF16), pltpu.VMEM((2, d, tf), BF16),
                        pltpu.VMEM((2, tf, d), BF16), pltpu.SemaphoreType.DMA((3, 2))],
        compiler_params=_cparams(1),
        name="half_ffn",
    )(*args)


def _cast_kernel(*refs):
    n = len(refs) // 2
    for src, dst in zip(refs[:n], refs[n:]):
        dst[...] = src[...].astype(dst.dtype)


def _largest_tile(n, multiple, cap):
    return max(t for t in range(multiple, min(n, cap) + 1, multiple) if n % t == 0)


def _cast_ffn_weights(w_in, w_out):
    depth, n_ffn, d, f2 = w_in.shape
    f = f2 // 2
    assert f % LANES == 0
    rows = _largest_tile(d, 2 * SUBLANES, 256)
    half = (None, None, rows, f)
    wg, wu = pl.pallas_call(
        _cast_kernel,
        out_shape=[jax.ShapeDtypeStruct((depth, n_ffn, d, f), BF16)] * 2,
        grid=(depth, n_ffn, d // rows),
        in_specs=[pl.BlockSpec(half, lambda l, w, r: (l, w, r, 0)), pl.BlockSpec(half, lambda l, w, r: (l, w, r, 1))],
        out_specs=[pl.BlockSpec(half, lambda l, w, r: (l, w, r, 0))] * 2,
        compiler_params=_cparams(3),
        name="cast_ffn_in",
    )(w_in, w_in)
    rows = _largest_tile(f, 2 * SUBLANES, 1024)
    blk = (None, None, rows, d)
    wo = pl.pallas_call(
        _cast_kernel,
        out_shape=jax.ShapeDtypeStruct(w_out.shape, BF16),
        grid=(depth, n_ffn, f // rows),
        in_specs=[pl.BlockSpec(blk, lambda l, w, r: (l, w, r, 0))],
        out_specs=pl.BlockSpec(blk, lambda l, w, r: (l, w, r, 0)),
        compiler_params=_cparams(3),
        name="cast_ffn_out",
    )(w_out)
    return wg, wu, wo


PROJ_CHUNK = 4 * HEAD_DIM


def _swap_halves_32(y):
    lane = lax.broadcasted_iota(jnp.int32, y.shape, 1)
    return jnp.where((lane & 32) == 0, pltpu.roll(y, LANES - 32, 1), pltpu.roll(y, 32, 1))


def _inproj_kernel(x_ref, g_ref, sh_ref, sc_ref, w_ref, gains_ref, cos_ref, sin_ref, *rest,
                   plan, tm, rows_per_batch, n_batch, n_lat_tiles):
    out_refs, h_scr = rest[:-1], rest[-1]
    i = pl.program_id(0)
    r = _mod_row(i, tm, rows_per_batch, n_batch)
    is_lat = i < n_lat_tiles
    _norm_modulate_into(h_scr, x_ref, g_ref[...], sh_ref[pl.ds(r, 1), :], sc_ref[pl.ds(r, 1), :])
    for w_col, width, out_idx, out_col, ops in plan:
        y = jnp.dot(h_scr[...], w_ref[:, w_col:w_col + width], preferred_element_type=F32)
        o_ref = out_refs[out_idx]
        if ops is None:
            o_ref[:, out_col:out_col + width] = y.astype(o_ref.dtype)
            continue
        for hh, op in enumerate(ops):
            yh = y[:, hh * HEAD_DIM:(hh + 1) * HEAD_DIM]
            if op is not None:
                gain_row, rope = op
                yh = yh * lax.rsqrt(jnp.mean(yh * yh, axis=-1, keepdims=True) + EPS)
                yh = yh * gains_ref[gain_row:gain_row + 1, :]
                if rope:
                    yh = jnp.where(is_lat, yh * cos_ref[...] + _swap_halves_32(yh) * sin_ref[...], yh)
            o_ref[:, out_col + hh * HEAD_DIM:out_col + (hh + 1) * HEAD_DIM] = yh.astype(o_ref.dtype)


def _inproj(x_all, mod, layer, g, w, gains, cos_t, sin_t, plan, outs, *, tm, n_lat_rows, rows_per_batch, n_batch):
    ta, d = x_all.shape
    n_pos_tiles = cos_t.shape[0] // tm
    kern = functools.partial(_inproj_kernel, plan=tuple(plan), tm=tm, rows_per_batch=rows_per_batch,
                             n_batch=n_batch, n_lat_tiles=n_lat_rows // tm)
    return pl.pallas_call(
        kern,
        out_shape=[jax.ShapeDtypeStruct((ta, n), dt) for n, dt in outs],
        grid=(ta // tm,),
        in_specs=[pl.BlockSpec((tm, d), lambda i: (i, 0)),
                  pl.BlockSpec((1, d), lambda i: (0, 0)),
                  _mod_spec(d, layer, 3, 1), _mod_spec(d, layer, 4, 1),
                  pl.BlockSpec(w.shape, lambda i: (0, 0), pipeline_mode=pl.Buffered(1)),
                  pl.BlockSpec(gains.shape, lambda i: (0, 0)),
                  pl.BlockSpec((tm, HEAD_DIM), lambda i: (i % n_pos_tiles, 0)),
                  pl.BlockSpec((tm, HEAD_DIM), lambda i: (i % n_pos_tiles, 0))],
        out_specs=[pl.BlockSpec((tm, n), lambda i: (i, 0)) for n, _ in outs],
        scratch_shapes=[pltpu.VMEM((tm, d), BF16)],
        compiler_params=_cparams(1),
        name="mixer_inproj",
    )(x_all, g.reshape(1, d), mod, mod, w, gains, cos_t, sin_t)


def _outproj_kernel(a1_ref, a2_ref, w1_ref, w2_ref, x_ref, gt_ref, o_ref, *, tm, rows_per_batch, n_batch):
    r = _mod_row(pl.program_id(0), tm, rows_per_batch, n_batch)
    d = o_ref.shape[1]
    for c0 in range(0, d, PROJ_CHUNK):
        cs = slice(c0, min(c0 + PROJ_CHUNK, d))
        y = (jnp.dot(a1_ref[...], w1_ref[:, cs], preferred_element_type=F32)
             + jnp.dot(a2_ref[...], w2_ref[:, cs], preferred_element_type=F32))
        o_ref[:, cs] = x_ref[:, cs] + gt_ref[pl.ds(r, 1), cs] * y


def _outproj(a1, a2, a2_col_block, w, x_all, n_rows, mod, layer, *, tm, rows_per_batch, n_batch):
    d = x_all.shape[1]
    kh = w.shape[0] // 2
    kern = functools.partial(_outproj_kernel, tm=tm, rows_per_batch=rows_per_batch, n_batch=n_batch)
    return pl.pallas_call(
        kern,
        out_shape=jax.ShapeDtypeStruct((n_rows, d), F32),
        grid=(n_rows // tm,),
        in_specs=[pl.BlockSpec((tm, kh), lambda i: (i, 0)),
                  pl.BlockSpec((tm, kh), lambda i: (i, a2_col_block)),
                  pl.BlockSpec((kh, d), lambda i: (0, 0)),
                  pl.BlockSpec((kh, d), lambda i: (1, 0)),
                  pl.BlockSpec((tm, d), lambda i: (i, 0)),
                  _mod_spec(d, layer, 5, 1)],
        out_specs=pl.BlockSpec((tm, d), lambda i: (i, 0)),
        compiler_params=_cparams(1),
        name="mixer_outproj",
    )(a1, a2, w, w, x_all, mod)


def _rope_tables(seq_len):
    half = HEAD_DIM // 2
    nf = half // 2
    inv = ROPE_THETA ** (-jnp.arange(nf, dtype=F32) / nf)
    pos = jnp.arange(seq_len, dtype=jnp.int32)
    ang_r = (pos // GRID_W).astype(F32)[:, None] * inv[None, :]
    ang_c = (pos % GRID_W).astype(F32)[:, None] * inv[None, :]
    cr, sr, cc, sc = jnp.cos(ang_r), jnp.sin(ang_r), jnp.cos(ang_c), jnp.sin(ang_c)
    return (jnp.concatenate([cr, cr, cc, cc], axis=-1),
            jnp.concatenate([-sr, sr, -sc, sc], axis=-1))


def _gelu_tanh(x):
    return x * (0.5 * (1.0 + jnp.tanh(np.float32(np.sqrt(2.0 / np.pi)) * (x + 0.044715 * (x * x * x)))))


def _lru_kernel(*refs, tt, reverse, final, n_lat_chunks):
    if final:
        (xp_ref, xc_ref, xn_ref, cw_ref, cb_ref, wa_ref, ba_ref, wx_ref, bx_ref, lam_ref,
         hf_ref, gl_ref, o_ref, xs_scr, a_scr, b_scr, carry_scr) = refs
    else:
        (xp_ref, xc_ref, xn_ref, cw_ref, cb_ref, wa_ref, ba_ref, wx_ref, bx_ref, lam_ref,
         o_ref, xs_scr, a_scr, b_scr, carry_scr) = refs
    s = pl.program_id(1)
    is_ctx = s == 0
    j = (n_lat_chunks - s) if reverse else (s - 1)
    has_prev = jnp.logical_and(jnp.logical_not(is_ctx), j > 0)
    has_next = jnp.logical_and(jnp.logical_not(is_ctx), j < n_lat_chunks - 1)

    @pl.when(is_ctx)
    def _():
        carry_scr[...] = jnp.zeros_like(carry_scr)

    halo = SUBLANES
    xs_scr[0:halo, :] = jnp.where(has_prev, xp_ref[tt - halo:tt, :], 0.0)
    xs_scr[halo:halo + tt, :] = xc_ref[...]
    xs_scr[halo + tt:2 * halo + tt, :] = jnp.where(has_next, xn_ref[0:halo, :], 0.0)

    width = xc_ref.shape[1]
    left = CONV_W // 2
    z = -lam_ref[...]
    softplus = jnp.maximum(z, 0.0) + jnp.log1p(jnp.exp(-jnp.abs(z)))
    for n in range(width // HEAD_DIM):
        cs = slice(n * HEAD_DIM, (n + 1) * HEAD_DIM)
        u = cb_ref[:, cs]
        xa = xs_scr[:, cs]
        n_rows = xa.shape[0]
        for k in range(CONV_W):
            xk = xa if k == left else pltpu.roll(xa, (left - k) % n_rows, 0)
            u = u + xk[halo:halo + tt, :] * cw_ref[k:k + 1, cs]
        ub = u.astype(BF16)
        gate_r = jnp.dot(ub, wa_ref[n], preferred_element_type=F32) + ba_ref[:, cs]
        gate_i = jnp.dot(ub, wx_ref[n], preferred_element_type=F32) + bx_ref[:, cs]
        log_a = (-LRU_C) * _sigmoid_tanh(gate_r) * softplus[:, cs]
        a = jnp.exp(log_a)
        a_scr[:, cs] = a
        b_scr[:, cs] = jnp.sqrt(-jnp.tanh(log_a) * (1.0 + a * a)) * (_sigmoid_tanh(gate_i) * u)

    n_groups = tt // SUBLANES
    row = lax.broadcasted_iota(jnp.int32, (SUBLANES, width), 0)

    def group(gi, carry):
        g = (n_groups - 1 - gi) if reverse else gi
        r0 = pl.multiple_of(g * SUBLANES, SUBLANES)
        a = a_scr[pl.ds(r0, SUBLANES), :]
        b = b_scr[pl.ds(r0, SUBLANES), :]
        for k in (1, 2, 4):
            if reverse:
                keep = row < SUBLANES - k
                shift = SUBLANES - k
            else:
                keep = row >= k
                shift = k
            a_sh = jnp.where(keep, pltpu.roll(a, shift, 0), 1.0)
            b_sh = jnp.where(keep, pltpu.roll(b, shift, 0), 0.0)
            b = a * b_sh + b
            a = a * a_sh
        h = b + a * carry
        b_scr[pl.ds(r0, SUBLANES), :] = h
        last = h[0:1, :] if reverse else h[SUBLANES - 1:SUBLANES, :]
        return jnp.broadcast_to(last, (SUBLANES, width))

    carry_scr[...] = lax.fori_loop(0, n_groups, group, carry_scr[...])

    if final:
        o_ref[...] = ((hf_ref[...] + b_scr[...]) * _gelu_tanh(gl_ref[...])).astype(o_ref.dtype)
    else:
        o_ref[...] = b_scr[...]


def _lru_pass(p, conv_w, conv_b, w_a, b_a, w_x, b_x, lam, hf, *, reverse, n_batch, seq_len, ctx_len, tt):
    ta = p.shape[0]
    width = conv_w.shape[1]
    assert ctx_len == tt and seq_len % tt == 0
    n_lat_chunks = seq_len // tt
    ctx_block0 = n_batch * n_lat_chunks
    final = hf is not None

    def chunk_index(b, s, off):
        j = (n_lat_chunks - s) if reverse else (s - 1)
        j = jnp.clip(j + off, 0, n_lat_chunks - 1)
        return jnp.where(s == 0, ctx_block0 + b, b * n_lat_chunks + j)

    def xspec(off):
        return pl.BlockSpec((tt, width), lambda b, s: (chunk_index(b, s, off), 0))

    row_spec = pl.BlockSpec((1, width), lambda b, s: (0, 0))
    w_spec = pl.BlockSpec(w_a.shape, lambda b, s: (0, 0, 0))
    in_specs = [xspec(-1), xspec(0), xspec(1),
                pl.BlockSpec((CONV_W, width), lambda b, s: (0, 0)), row_spec,
                w_spec, row_spec, w_spec, row_spec, row_spec]
    args = [p, p, p, conv_w, conv_b.reshape(1, width), w_a, b_a.reshape(1, width),
            w_x, b_x.reshape(1, width), lam.reshape(1, width)]
    if final:
        in_specs += [xspec(0), pl.BlockSpec((tt, width), lambda b, s: (chunk_index(b, s, 0), 1))]
        args += [hf, p]
    kern = functools.partial(_lru_kernel, tt=tt, reverse=reverse, final=final, n_lat_chunks=n_lat_chunks)
    return pl.pallas_call(
        kern,
        out_shape=jax.ShapeDtypeStruct((ta, width), BF16 if final else F32),
        grid=(n_batch, n_lat_chunks + 1),
        in_specs=in_specs,
        out_specs=xspec(0),
        scratch_shapes=[pltpu.VMEM((tt + 2 * SUBLANES, width), F32), pltpu.VMEM((tt, width), F32),
                        pltpu.VMEM((tt, width), F32), pltpu.VMEM((SUBLANES, width), F32)],
        compiler_params=_cparams(2),
        name="rglru_rev" if reverse else "rglru_fwd",
    )(*args)


LOG2E = float(np.log2(np.e))
LOGIT_SCALE = ATTN_SCALE * LOG2E


def _softmax_parts(s2, extra_logit2=None):
    m = jnp.max(s2, axis=-1, keepdims=True)
    if extra_logit2 is not None:
        m = jnp.maximum(m, extra_logit2)
    e = jnp.exp2(s2 - m)
    denom = jnp.sum(e, axis=-1, keepdims=True)
    if extra_logit2 is not None:
        denom = denom + jnp.exp2(extra_logit2 - m)
    return e, 1.0 / denom


def _swa_kernel(sink_ref, q_ref, kp_ref, kc_ref, kn_ref, kx_ref, vp_ref, vc_ref, vn_ref, vx_ref, mask_ref, o_ref,
                *, n_kv_heads):
    blk = q_ref.shape[0]
    head = lax.broadcasted_iota(jnp.int32, (GQA_GROUP * blk, 1), 0) // blk
    logits = []
    for kh in range(n_kv_heads):
        ks = slice(kh * HEAD_DIM, (kh + 1) * HEAD_DIM)
        q0 = kh * GQA_GROUP
        q = jnp.concatenate([q_ref[:, (q0 + h) * HEAD_DIM:(q0 + h + 1) * HEAD_DIM] for h in range(GQA_GROUP)],
                            axis=0)
        k = jnp.concatenate([kp_ref[:, ks], kc_ref[:, ks], kn_ref[:, ks], kx_ref[:, ks]], axis=0)
        logits.append(lax.dot_general(q, k, (((1,), (1,)), ((), ())), preferred_element_type=F32))
    for kh in range(n_kv_heads):
        ks = slice(kh * HEAD_DIM, (kh + 1) * HEAD_DIM)
        q0 = kh * GQA_GROUP
        v = jnp.concatenate([vp_ref[:, ks], vc_ref[:, ks], vn_ref[:, ks], vx_ref[:, ks]], axis=0)
        s = logits[kh] * LOGIT_SCALE + mask_ref[...]
        sink = jnp.zeros((GQA_GROUP * blk, 1), F32)
        for h in range(GQA_GROUP):
            sink = jnp.where(head == h, sink_ref[q0 + h] * LOG2E, sink)
        e, inv = _softmax_parts(s, sink)
        o = jnp.dot(e.astype(BF16), v, preferred_element_type=F32) * inv
        for h in range(GQA_GROUP):
            o_ref[:, (q0 + h) * HEAD_DIM:(q0 + h + 1) * HEAD_DIM] = o[h * blk:(h + 1) * blk, :].astype(o_ref.dtype)


def _swa_mask_table(blk, ctx_len):
    shape = (4, GQA_GROUP * blk, 3 * blk + ctx_len)
    cls = lax.broadcasted_iota(jnp.int32, shape, 0)
    qi = lax.broadcasted_iota(jnp.int32, shape, 1) % blk
    col = lax.broadcasted_iota(jnp.int32, shape, 2)
    lo = jnp.where(cls == 0, blk, 0)
    hi = jnp.where(cls == 3, 0, jnp.where(cls == 2, 2 * blk, 3 * blk))
    rel = col - blk - qi
    ok = ((col >= lo) & (col < hi) & (rel >= -WINDOW) & (rel <= WINDOW)) | (col >= 3 * blk)
    return jnp.where(ok, 0.0, NEG).astype(F32)


def _swa(qkv, sink, *, n_batch, seq_len, ctx_len, n_kv_heads, blk):
    ta = qkv.shape[0]
    n_heads = n_kv_heads * GQA_GROUP
    n_lat_blocks = seq_len // blk
    n_ctx_blocks = ctx_len // blk
    assert blk == WINDOW and seq_len % blk == 0 and ctx_len % blk == 0 and n_lat_blocks >= 2
    ctx_q0 = n_batch * n_lat_blocks
    ctx_kv0 = n_batch * seq_len // ctx_len
    kv_w = n_kv_heads * HEAD_DIM
    k_col, v_col = n_heads * HEAD_DIM // kv_w, n_heads * HEAD_DIM // kv_w + 1

    def q_index(b, n):
        return (jnp.where(n < n_lat_blocks, b * n_lat_blocks + n, ctx_q0 + b * n_ctx_blocks + n - n_lat_blocks), 0)

    def kv_spec(off, col):
        return pl.BlockSpec((blk, kv_w),
                            lambda b, n: (b * n_lat_blocks + jnp.clip(n + off, 0, n_lat_blocks - 1), col))

    def ctx_spec(col):
        return pl.BlockSpec((ctx_len, kv_w), lambda b, n: (ctx_kv0 + b, col))

    def mask_index(b, n):
        interior = jnp.where(n == 0, 0, jnp.where(n == n_lat_blocks - 1, 2, 1))
        return (jnp.where(n < n_lat_blocks, interior, 3), 0, 0)

    mask = _swa_mask_table(blk, ctx_len)
    kern = functools.partial(_swa_kernel, n_kv_heads=n_kv_heads)
    return pl.pallas_call(
        kern,
        out_shape=jax.ShapeDtypeStruct((ta, n_heads * HEAD_DIM), BF16),
        grid=(n_batch, n_lat_blocks + n_ctx_blocks),
        in_specs=[pl.BlockSpec(memory_space=pltpu.SMEM),
                  pl.BlockSpec((blk, n_heads * HEAD_DIM), q_index),
                  kv_spec(-1, k_col), kv_spec(0, k_col), kv_spec(1, k_col), ctx_spec(k_col),
                  kv_spec(-1, v_col), kv_spec(0, v_col), kv_spec(1, v_col), ctx_spec(v_col),
                  pl.BlockSpec((None,) + mask.shape[1:], mask_index)],
        out_specs=pl.BlockSpec((blk, n_heads * HEAD_DIM), q_index),
        compiler_params=_cparams(2),
        name="windowed_gqa",
    )(sink, *([qkv] * 9), mask)


NA_HEADS_PER_STEP = 8


def _na_kernel(q_ref, kvp_ref, kvc_ref, kvn_ref, kvx_ref, bias_ref, o_ref):
    n_local = bias_ref.shape[-1]
    gw = NA_HEADS_PER_STEP * HEAD_DIM
    kv_refs = (kvp_ref, kvc_ref, kvn_ref, kvx_ref)
    heads = [slice(h * HEAD_DIM, (h + 1) * HEAD_DIM) for h in range(NA_HEADS_PER_STEP)]
    def qk(h):
        k = jnp.concatenate([ref[:, heads[h]] for ref in kv_refs], axis=0)
        return lax.dot_general(q_ref[:, heads[h]], k, (((1,), (1,)), ((), ())), preferred_element_type=F32)

    def softmax(h, logit):
        s = logit * LOGIT_SCALE
        s = jnp.concatenate([s[:, :n_local] + bias_ref[h], s[:, n_local:]], axis=1)
        e, inv = _softmax_parts(s)
        return e.astype(BF16), inv

    def pv(h, e, inv):
        vs = slice(gw + heads[h].start, gw + heads[h].stop)
        v = jnp.concatenate([ref[:, vs] for ref in kv_refs], axis=0)
        o_ref[:, heads[h]] = (jnp.dot(e, v, preferred_element_type=F32) * inv).astype(o_ref.dtype)

    n = NA_HEADS_PER_STEP
    logits = {0: qk(0)}
    weights = {}
    for h in range(n):
        if h + 1 < n:
            logits[h + 1] = qk(h + 1)
        weights[h] = softmax(h, logits.pop(h))
        if h >= 1:
            pv(h - 1, *weights.pop(h - 1))
    pv(n - 1, *weights.pop(n - 1))


def _na_bias_table(rpb, n_grid_rows):
    r_in, w = NA_ROWS_PER_BLOCK, GRID_W
    kh = min(NA_KH, n_grid_rows)
    n_blocks = n_grid_rows // r_in
    n_ro, n_co = 2 * NA_KH - 1, 2 * NA_KW - 1
    qc, kc = np.arange(w)[:, None], np.arange(w)[None, :]
    e_col = (kc - qc + NA_KW - 1 == np.arange(n_co)[:, None, None]).astype(np.float32)
    win_start = np.clip(qc - NA_KW // 2, 0, w - NA_KW)
    col_ok = (kc >= win_start) & (kc < win_start + NA_KW)
    ri, krj = np.arange(r_in)[:, None], np.arange(3 * r_in)[None, :]
    row_ok = []
    for jb in (0, max(n_blocks // 2, 1) if n_blocks > 2 else 0, n_blocks - 1):
        r, kr = r_in * jb + ri, r_in * (jb - 1) + krj
        rs = np.clip(r - kh // 2, 0, n_grid_rows - kh)
        row_ok.append((kr >= rs) & (kr < rs + kh))
    ok = np.stack(row_ok)[:, :, None, :, None] & col_ok[None, None, :, None, :]
    ok = ok.reshape(3, r_in * w, 3 * r_in * w)
    tz = jnp.einsum('hab,bqc->haqc', rpb.astype(F32), e_col, precision=lax.Precision.HIGHEST) * LOG2E
    assert r_in - 1 <= NA_KH - 1 - r_in and 3 * r_in - 1 - r_in + NA_KH - 1 < n_ro
    full = jnp.concatenate(
        [jnp.concatenate([tz[:, k - i - r_in + NA_KH - 1] for k in range(3 * r_in)], axis=-1) for i in range(r_in)],
        axis=-2)
    return jnp.where(ok[None], full[:, None], NEG)


def _na(qkv, bias, *, n_batch, seq_len, ctx_len, n_heads):
    tq = NA_ROWS_PER_BLOCK * GRID_W
    n_blocks = seq_len // tq
    assert ctx_len == tq and seq_len % tq == 0 and NA_KH == 2 * NA_ROWS_PER_BLOCK and n_blocks >= 2
    assert n_heads % NA_HEADS_PER_STEP == 0
    n_groups = n_heads // NA_HEADS_PER_STEP
    gw = NA_HEADS_PER_STEP * HEAD_DIM
    ctx0 = n_batch * n_blocks

    def kv_spec(off):
        return pl.BlockSpec((tq, 2 * gw),
                            lambda b, g, jb: (b * n_blocks + jnp.clip(jb + off, 0, n_blocks - 1), g))

    def bias_index(b, g, jb):
        return (g, jnp.where(jb == 0, 0, jnp.where(jb == n_blocks - 1, 2, 1)), 0, 0)

    return pl.pallas_call(
        _na_kernel,
        out_shape=jax.ShapeDtypeStruct((n_batch * seq_len, n_heads * HEAD_DIM), BF16),
        grid=(n_batch, n_groups, n_blocks),
        in_specs=[pl.BlockSpec((tq, gw), lambda b, g, jb: (b * n_blocks + jb, 2 * n_groups + g)),
                  kv_spec(-1), kv_spec(0), kv_spec(1),
                  pl.BlockSpec((ctx_len, 2 * gw), lambda b, g, jb: (ctx0 + b, g)),
                  pl.BlockSpec((NA_HEADS_PER_STEP, None) + bias.shape[2:], bias_index)],
        out_specs=pl.BlockSpec((tq, gw), lambda b, g, jb: (b * n_blocks + jb, g)),
        compiler_params=_cparams(3),
        name="neighbourhood_attn",
    )(*([qkv] * 5), bias)


def _tiles():
    return dict(tm=1024, tm_two_source=512, tm_proj=512, tm_out=512, tf=512, tn_mod=1024)


def _chunked(total, out_idx, out_col0=0, w_col0=0):
    return [(w_col0 + c, min(PROJ_CHUNK, total - c), out_idx, out_col0 + c, None)
            for c in range(0, total, PROJ_CHUNK)]


def _head_chunks(head_ops, w_col0, out_idx, out_col0):
    per = PROJ_CHUNK // HEAD_DIM
    return [(w_col0 + h0 * HEAD_DIM, len(head_ops[h0:h0 + per]) * HEAD_DIM, out_idx, out_col0 + h0 * HEAD_DIM,
             tuple(head_ops[h0:h0 + per])) for h0 in range(0, len(head_ops), per)]


def kernel(x, c, ctx, c_ctx, w_mod, b_mod, norm_g, ffn_w_in, ffn_w_out, ab_w_in, lru_conv_w, lru_conv_b,
           lru_w_a, lru_b_a, lru_w_x, lru_b_x, lru_lambda, attn_q_norm, attn_k_norm, attn_sink, ab_w_out,
           na_w_in, na_q_norm, na_k_norm, na_rpb, na_w_out):
    n_batch, seq_len, d = x.shape
    ctx_len = ctx.shape[1]
    depth = w_mod.shape[0]
    lru_w = lru_conv_w.shape[2]
    n_kv = attn_sink.shape[1] // GQA_GROUP
    n_att = n_kv * GQA_GROUP
    na_heads = na_rpb.shape[1]
    assert n_batch < MOD_ROWS and depth == 2
    t = _tiles()
    tl, tc = n_batch * seq_len, n_batch * ctx_len
    geom = dict(rows_per_batch=seq_len, n_batch=n_batch)

    cc = jnp.zeros((MOD_ROWS, d), F32).at[:n_batch].set(c).at[n_batch].set(c_ctx)
    mod = _mod_table(cc, w_mod, b_mod, t["tn_mod"])
    wg, wu, wo = _cast_ffn_weights(ffn_w_in, ffn_w_out)

    def ffn(x_src, n_rows, layer, which, x_src2=None):
        tm = t["tm"] if x_src2 is None else t["tm_two_source"]
        return _ffn(x_src, x_src2, n_rows, mod, layer, which, norm_g[layer, 2 * which], wg, wu, wo,
                    tm=tm, tf=t["tf"], **geom)

    cos_t, sin_t = _rope_tables(seq_len)
    proj_geom = dict(tm=t["tm_proj"], n_lat_rows=tl, **geom)

    x_all = ffn(x.reshape(tl, d), tl + tc, 0, 0, x_src2=ctx.reshape(tc, d))
    gains = jnp.stack([attn_q_norm[0], attn_k_norm[0]])
    head_ops = [(0, True)] * n_att + [(1, True)] * n_kv + [None] * n_kv
    plan = _head_chunks(head_ops, 2 * lru_w, 1, 0) + _chunked(2 * lru_w, 0)
    p, qkv = _inproj(x_all, mod, 0, norm_g[0, 1], ab_w_in[0].astype(BF16), gains, cos_t, sin_t, plan,
                     ((2 * lru_w, F32), (len(head_ops) * HEAD_DIM, BF16)), **proj_geom)
    lru_args = lambda dr: (lru_conv_w[0], lru_conv_b[0], lru_w_a[0, dr].astype(BF16), lru_b_a[0, dr],
                           lru_w_x[0, dr].astype(BF16), lru_b_x[0, dr], lru_lambda[0, dr])
    scan_geom = dict(n_batch=n_batch, seq_len=seq_len, ctx_len=ctx_len, tt=ctx_len)
    h_fwd = _lru_pass(p, *lru_args(0), None, reverse=False, **scan_geom)
    lru = _lru_pass(p, *lru_args(1), h_fwd, reverse=True, **scan_geom)
    att = _swa(qkv, attn_sink[0], n_batch=n_batch, seq_len=seq_len, ctx_len=ctx_len, n_kv_heads=n_kv,
               blk=WINDOW)
    x_all = _outproj(lru, att, 0, ab_w_out[0].astype(BF16), x_all, tl + tc, mod, 0, tm=t["tm_out"], **geom)
    x_all = ffn(x_all, tl + tc, 0, 1)

    x_all = ffn(x_all, tl + tc, 1, 0)
    gains = jnp.stack([na_q_norm[0], na_k_norm[0]])
    na_d = na_heads * HEAD_DIM
    gw = NA_HEADS_PER_STEP * HEAD_DIM
    groups = range(0, na_d, gw)
    plan = sum([_head_chunks([(0, False)] * NA_HEADS_PER_STEP, g0, 0, 2 * na_d + g0) for g0 in groups], [])
    plan += sum([_head_chunks([(1, False)] * NA_HEADS_PER_STEP, na_d + g0, 0, 2 * g0) for g0 in groups], [])
    plan += sum([_head_chunks([None] * NA_HEADS_PER_STEP, 2 * na_d + g0, 0, 2 * g0 + gw) for g0 in groups], [])
    (qkv,) = _inproj(x_all, mod, 1, norm_g[1, 1], na_w_in[0].astype(BF16), gains, cos_t, sin_t, plan,
                     ((3 * na_d, BF16),), **proj_geom)
    o = _na(qkv, _na_bias_table(na_rpb[0], seq_len // GRID_W), n_batch=n_batch, seq_len=seq_len, ctx_len=ctx_len,
            n_heads=na_heads)
    x_lat = _outproj(o, o, 1, na_w_out[0].astype(BF16), x_all, tl, mod, 1, tm=t["tm_out"], **geom)
    x_lat = ffn(x_lat, tl, 1, 1)
    return x_lat.reshape(n_batch, seq_len, d)
```

```python
import functools

import numpy as np
import jax
import jax.numpy as jnp
from jax import lax
from jax.experimental import pallas as pl
from jax.experimental.pallas import tpu as pltpu

HEAD_DIM = 128
EPS = 1e-6
N_MOD = 9
GRID_W = 64
CONV_W = 4
LRU_C = 8.0
WINDOW = 128
ROPE_THETA = 10000.0
NA_KH = 8
NA_KW = 16
ATTN_SCALE = HEAD_DIM ** -0.5
GQA_GROUP = 4

LANES = 128
SUBLANES = 8
VMEM_LIMIT_BYTES = 56 * 1024 * 1024
MOD_ROWS = SUBLANES

NEG = -1e30
NA_ROWS_PER_BLOCK = 4

BF16 = jnp.bfloat16
F32 = jnp.float32


def _cparams(n_axes):
    return pltpu.CompilerParams(dimension_semantics=("arbitrary",) * n_axes,
                                vmem_limit_bytes=VMEM_LIMIT_BYTES)


def _sigmoid(x):
    return 1.0 / (1.0 + jnp.exp(-x))


def _sigmoid_tanh(x):
    return 0.5 * jnp.tanh(0.5 * x) + 0.5


NORM_CHUNK_ROWS = 16


def _norm_modulate_into(h_scr, x_ref, g, shift, scale, alt=None):
    gain = g * (1.0 + scale)

    def chunk(c, carry):
        r0 = pl.multiple_of(c * NORM_CHUNK_ROWS, NORM_CHUNK_ROWS)
        x = x_ref[pl.ds(r0, NORM_CHUNK_ROWS), :]
        if alt is not None:
            x = jnp.where(alt[0], alt[1][pl.ds(r0, NORM_CHUNK_ROWS), :], x)
        inv = lax.rsqrt(jnp.mean(x * x, axis=-1, keepdims=True) + EPS)
        h_scr[pl.ds(r0, NORM_CHUNK_ROWS), :] = ((x * inv) * gain + shift).astype(h_scr.dtype)
        return carry

    lax.fori_loop(0, x_ref.shape[0] // NORM_CHUNK_ROWS, chunk, 0, unroll=4)


def _mod_kernel(c_ref, w_ref, b_ref, o_ref):
    c = c_ref[...]
    s = c * _sigmoid(c)
    o_ref[...] = jnp.dot(s.astype(BF16), w_ref[...].astype(BF16),
                         preferred_element_type=F32) + b_ref[...]


def _mod_table(cc, w_mod, b_mod, tn):
    depth, d, n = w_mod.shape
    return pl.pallas_call(
        _mod_kernel,
        out_shape=jax.ShapeDtypeStruct((depth, MOD_ROWS, n), F32),
        grid=(depth, n // tn),
        in_specs=[pl.BlockSpec((MOD_ROWS, d), lambda l, j: (0, 0)),
                  pl.BlockSpec((None, d, tn), lambda l, j: (l, 0, j)),
                  pl.BlockSpec((None, 1, tn), lambda l, j: (l, 0, j))],
        out_specs=pl.BlockSpec((None, MOD_ROWS, tn), lambda l, j: (l, 0, j)),
        compiler_params=_cparams(2),
        name="mod_table",
    )(cc, w_mod, b_mod.reshape(depth, 1, n))


def _mod_spec(d, layer, k, n_grid_axes):
    if n_grid_axes == 1:
        return pl.BlockSpec((None, MOD_ROWS, d), lambda i: (layer, 0, k))
    return pl.BlockSpec((None, MOD_ROWS, d), lambda i, j: (layer, 0, k))


def _mod_row(tile_idx, tm, rows_per_batch, n_batch):
    return jnp.minimum(tile_idx * tm // rows_per_batch, n_batch)


def _ffn_kernel(x_ref, g_ref, sh_ref, sc_ref, gt_ref, wg_ref, wu_ref, wo_ref, *rest,
                tm, n_tiles1, rows_per_batch, n_batch, last_width):
    o_ref, h_scr = rest[-2:]
    i, j = pl.program_id(0), pl.program_id(1)
    last = pl.num_programs(1) - 1
    r = _mod_row(i, tm, rows_per_batch, n_batch)
    alt = (i >= n_tiles1, rest[0]) if n_tiles1 is not None else None

    @pl.when(j == 0)
    def _():
        _norm_modulate_into(h_scr, x_ref, g_ref[...], sh_ref[pl.ds(r, 1), :], sc_ref[pl.ds(r, 1), :], alt)
        o_ref[...] = jnp.zeros_like(o_ref)

    def hidden_block(width):
        h = h_scr[...]
        a = jnp.dot(h, wg_ref[:, :width], preferred_element_type=F32)
        u = jnp.dot(h, wu_ref[:, :width], preferred_element_type=F32)
        act = (a * _sigmoid(a)) * u
        o_ref[...] += jnp.dot(act.astype(BF16), wo_ref[:width, :], preferred_element_type=F32)

    if last_width == wg_ref.shape[1]:
        hidden_block(last_width)
    else:
        pl.when(j < last)(lambda: hidden_block(wg_ref.shape[1]))
        pl.when(j == last)(lambda: hidden_block(last_width))

    @pl.when(j == last)
    def _():
        x = x_ref[...] if alt is None else jnp.where(alt[0], alt[1][...], x_ref[...])
        o_ref[...] = x + 0.5 * gt_ref[pl.ds(r, 1), :] * o_ref[...]


def _ffn(x_src, x_src2, n_rows, mod, layer, which, g, wg, wu, wo, *, tm, tf, rows_per_batch, n_batch):
    d = x_src.shape[1]
    n_tiles = n_rows // tm
    n_tiles1 = None if x_src2 is None else x_src.shape[0] // tm
    d_ff = wo.shape[2]
    nf = pl.cdiv(d_ff, tf)
    last_width = d_ff - (nf - 1) * tf
    assert last_width % LANES == 0
    k0 = 6 * which
    kern = functools.partial(_ffn_kernel, tm=tm, n_tiles1=n_tiles1, rows_per_batch=rows_per_batch,
                             n_batch=n_batch, last_width=last_width)
    x_index = (lambda i, j: (i, 0)) if x_src2 is None else (lambda i, j: (jnp.minimum(i, n_tiles1 - 1), 0))
    in_specs = [pl.BlockSpec((tm, d), x_index),
                pl.BlockSpec((1, d), lambda i, j: (0, 0)),
                _mod_spec(d, layer, k0, 2), _mod_spec(d, layer, k0 + 1, 2), _mod_spec(d, layer, k0 + 2, 2),
                pl.BlockSpec((None, None, d, tf), lambda i, j: (layer, which, 0, j)),
                pl.BlockSpec((None, None, d, tf), lambda i, j: (layer, which, 0, j)),
                pl.BlockSpec((None, None, tf, d), lambda i, j: (layer, which, j, 0))]
    args = [x_src, g.reshape(1, d), mod, mod, mod, wg, wu, wo]
    if x_src2 is not None:
        in_specs.append(pl.BlockSpec((tm, d), lambda i, j: (jnp.maximum(i - n_tiles1, 0), 0)))
        args.append(x_src2)
    return pl.pallas_call(
        kern,
        out_shape=jax.ShapeDtypeStruct((n_rows, d), F32),
        grid=(n_tiles, nf),
        in_specs=in_specs,
        out_specs=pl.BlockSpec((tm, d), lambda i, j: (i, 0)),
        scratch_shapes=[pltpu.VMEM((tm, d), BF16)],
        compiler_params=_cparams(2),
        name="half_ffn",
    )(*args)


def _cast_kernel(*refs):
    n = len(refs) // 2
    for src, dst in zip(refs[:n], refs[n:]):
        dst[...] = src[...].astype(dst.dtype)


def _largest_tile(n, multiple, cap):
    return max(t for t in range(multiple, min(n, cap) + 1, multiple) if n % t == 0)


def _cast_ffn_weights(w_in, w_out):
    depth, n_ffn, d, f2 = w_in.shape
    f = f2 // 2
    assert f % LANES == 0
    rows = _largest_tile(d, 2 * SUBLANES, 256)
    half = (None, None, rows, f)
    wg, wu = pl.pallas_call(
        _cast_kernel,
        out_shape=[jax.ShapeDtypeStruct((depth, n_ffn, d, f), BF16)] * 2,
        grid=(depth, n_ffn, d // rows),
        in_specs=[pl.BlockSpec(half, lambda l, w, r: (l, w, r, 0)), pl.BlockSpec(half, lambda l, w, r: (l, w, r, 1))],
        out_specs=[pl.BlockSpec(half, lambda l, w, r: (l, w, r, 0))] * 2,
        compiler_params=_cparams(3),
        name="cast_ffn_in",
    )(w_in, w_in)
    rows = _largest_tile(f, 2 * SUBLANES, 1024)
    blk = (None, None, rows, d)
    wo = pl.pallas_call(
        _cast_kernel,
        out_shape=jax.ShapeDtypeStruct(w_out.shape, BF16),
        grid=(depth, n_ffn, f // rows),
        in_specs=[pl.BlockSpec(blk, lambda l, w, r: (l, w, r, 0))],
        out_specs=pl.BlockSpec(blk, lambda l, w, r: (l, w, r, 0)),
        compiler_params=_cparams(3),
        name="cast_ffn_out",
    )(w_out)
    return wg, wu, wo


PROJ_CHUNK = 4 * HEAD_DIM


def _swap_halves_32(y):
    lane = lax.broadcasted_iota(jnp.int32, y.shape, 1)
    return jnp.where((lane & 32) == 0, pltpu.roll(y, LANES - 32, 1), pltpu.roll(y, 32, 1))


def _inproj_kernel(x_ref, g_ref, sh_ref, sc_ref, w_ref, gains_ref, cos_ref, sin_ref, *rest,
                   plan, tm, rows_per_batch, n_batch, n_lat_tiles):
    out_refs, h_scr = rest[:-1], rest[-1]
    i = pl.program_id(0)
    r = _mod_row(i, tm, rows_per_batch, n_batch)
    is_lat = i < n_lat_tiles
    _norm_modulate_into(h_scr, x_ref, g_ref[...], sh_ref[pl.ds(r, 1), :], sc_ref[pl.ds(r, 1), :])
    for w_col, width, out_idx, out_col, ops in plan:
        y = jnp.dot(h_scr[...], w_ref[:, w_col:w_col + width], preferred_element_type=F32)
        o_ref = out_refs[out_idx]
        if ops is None:
            o_ref[:, out_col:out_col + width] = y.astype(o_ref.dtype)
            continue
        for hh, op in enumerate(ops):
            yh = y[:, hh * HEAD_DIM:(hh + 1) * HEAD_DIM]
            if op is not None:
                gain_row, rope = op
                yh = yh * lax.rsqrt(jnp.mean(yh * yh, axis=-1, keepdims=True) + EPS)
                yh = yh * gains_ref[gain_row:gain_row + 1, :]
                if rope:
                    yh = jnp.where(is_lat, yh * cos_ref[...] + _swap_halves_32(yh) * sin_ref[...], yh)
            o_ref[:, out_col + hh * HEAD_DIM:out_col + (hh + 1) * HEAD_DIM] = yh.astype(o_ref.dtype)


def _inproj(x_all, mod, layer, g, w, gains, cos_t, sin_t, plan, outs, *, tm, n_lat_rows, rows_per_batch, n_batch):
    ta, d = x_all.shape
    n_pos_tiles = cos_t.shape[0] // tm
    kern = functools.partial(_inproj_kernel, plan=tuple(plan), tm=tm, rows_per_batch=rows_per_batch,
                             n_batch=n_batch, n_lat_tiles=n_lat_rows // tm)
    return pl.pallas_call(
        kern,
        out_shape=[jax.ShapeDtypeStruct((ta, n), dt) for n, dt in outs],
        grid=(ta // tm,),
        in_specs=[pl.BlockSpec((tm, d), lambda i: (i, 0)),
                  pl.BlockSpec((1, d), lambda i: (0, 0)),
                  _mod_spec(d, layer, 3, 1), _mod_spec(d, layer, 4, 1),
                  pl.BlockSpec(w.shape, lambda i: (0, 0), pipeline_mode=pl.Buffered(1)),
                  pl.BlockSpec(gains.shape, lambda i: (0, 0)),
                  pl.BlockSpec((tm, HEAD_DIM), lambda i: (i % n_pos_tiles, 0)),
                  pl.BlockSpec((tm, HEAD_DIM), lambda i: (i % n_pos_tiles, 0))],
        out_specs=[pl.BlockSpec((tm, n), lambda i: (i, 0)) for n, _ in outs],
        scratch_shapes=[pltpu.VMEM((tm, d), BF16)],
        compiler_params=_cparams(1),
        name="mixer_inproj",
    )(x_all, g.reshape(1, d), mod, mod, w, gains, cos_t, sin_t)


def _outproj_kernel(a1_ref, a2_ref, w1_ref, w2_ref, x_ref, gt_ref, o_ref, *, tm, rows_per_batch, n_batch):
    r = _mod_row(pl.program_id(0), tm, rows_per_batch, n_batch)
    d = o_ref.shape[1]
    for c0 in range(0, d, PROJ_CHUNK):
        cs = slice(c0, min(c0 + PROJ_CHUNK, d))
        y = (jnp.dot(a1_ref[...], w1_ref[:, cs], preferred_element_type=F32)
             + jnp.dot(a2_ref[...], w2_ref[:, cs], preferred_element_type=F32))
        o_ref[:, cs] = x_ref[:, cs] + gt_ref[pl.ds(r, 1), cs] * y


def _outproj(a1, a2, a2_col_block, w, x_all, n_rows, mod, layer, *, tm, rows_per_batch, n_batch):
    d = x_all.shape[1]
    kh = w.shape[0] // 2
    kern = functools.partial(_outproj_kernel, tm=tm, rows_per_batch=rows_per_batch, n_batch=n_batch)
    return pl.pallas_call(
        kern,
        out_shape=jax.ShapeDtypeStruct((n_rows, d), F32),
        grid=(n_rows // tm,),
        in_specs=[pl.BlockSpec((tm, kh), lambda i: (i, 0)),
                  pl.BlockSpec((tm, kh), lambda i: (i, a2_col_block)),
                  pl.BlockSpec((kh, d), lambda i: (0, 0)),
                  pl.BlockSpec((kh, d), lambda i: (1, 0)),
                  pl.BlockSpec((tm, d), lambda i: (i, 0)),
                  _mod_spec(d, layer, 5, 1)],
        out_specs=pl.BlockSpec((tm, d), lambda i: (i, 0)),
        compiler_params=_cparams(1),
        name="mixer_outproj",
    )(a1, a2, w, w, x_all, mod)


def _rope_tables(seq_len):
    half = HEAD_DIM // 2
    nf = half // 2
    inv = ROPE_THETA ** (-jnp.arange(nf, dtype=F32) / nf)
    pos = jnp.arange(seq_len, dtype=jnp.int32)
    ang_r = (pos // GRID_W).astype(F32)[:, None] * inv[None, :]
    ang_c = (pos % GRID_W).astype(F32)[:, None] * inv[None, :]
    cr, sr, cc, sc = jnp.cos(ang_r), jnp.sin(ang_r), jnp.cos(ang_c), jnp.sin(ang_c)
    return (jnp.concatenate([cr, cr, cc, cc], axis=-1),
            jnp.concatenate([-sr, sr, -sc, sc], axis=-1))


def _gelu_tanh(x):
    return x * (0.5 * (1.0 + jnp.tanh(np.float32(np.sqrt(2.0 / np.pi)) * (x + 0.044715 * (x * x * x)))))


def _lru_kernel(*refs, tt, reverse, final, n_lat_chunks):
    if final:
        (xp_ref, xc_ref, xn_ref, cw_ref, cb_ref, wa_ref, ba_ref, wx_ref, bx_ref, lam_ref,
         hf_ref, gl_ref, o_ref, xs_scr, a_scr, b_scr, carry_scr) = refs
    else:
        (xp_ref, xc_ref, xn_ref, cw_ref, cb_ref, wa_ref, ba_ref, wx_ref, bx_ref, lam_ref,
         o_ref, xs_scr, a_scr, b_scr, carry_scr) = refs
    s = pl.program_id(1)
    is_ctx = s == 0
    j = (n_lat_chunks - s) if reverse else (s - 1)
    has_prev = jnp.logical_and(jnp.logical_not(is_ctx), j > 0)
    has_next = jnp.logical_and(jnp.logical_not(is_ctx), j < n_lat_chunks - 1)

    @pl.when(is_ctx)
    def _():
        carry_scr[...] = jnp.zeros_like(carry_scr)

    halo = SUBLANES
    xs_scr[0:halo, :] = jnp.where(has_prev, xp_ref[tt - halo:tt, :], 0.0)
    xs_scr[halo:halo + tt, :] = xc_ref[...]
    xs_scr[halo + tt:2 * halo + tt, :] = jnp.where(has_next, xn_ref[0:halo, :], 0.0)

    width = xc_ref.shape[1]
    left = CONV_W // 2
    z = -lam_ref[...]
    softplus = jnp.maximum(z, 0.0) + jnp.log1p(jnp.exp(-jnp.abs(z)))
    for n in range(width // HEAD_DIM):
        cs = slice(n * HEAD_DIM, (n + 1) * HEAD_DIM)
        u = cb_ref[:, cs]
        xa = xs_scr[:, cs]
        n_rows = xa.shape[0]
        for k in range(CONV_W):
            xk = xa if k == left else pltpu.roll(xa, (left - k) % n_rows, 0)
            u = u + xk[halo:halo + tt, :] * cw_ref[k:k + 1, cs]
        ub = u.astype(BF16)
        gate_r = jnp.dot(ub, wa_ref[n], preferred_element_type=F32) + ba_ref[:, cs]
        gate_i = jnp.dot(ub, wx_ref[n], preferred_element_type=F32) + bx_ref[:, cs]
        log_a = (-LRU_C) * _sigmoid_tanh(gate_r) * softplus[:, cs]
        a = jnp.exp(log_a)
        a_scr[:, cs] = a
        b_scr[:, cs] = jnp.sqrt(-jnp.tanh(log_a) * (1.0 + a * a)) * (_sigmoid_tanh(gate_i) * u)

    n_groups = tt // SUBLANES
    row = lax.broadcasted_iota(jnp.int32, (SUBLANES, width), 0)

    def group(gi, carry):
        g = (n_groups - 1 - gi) if reverse else gi
        r0 = pl.multiple_of(g * SUBLANES, SUBLANES)
        a = a_scr[pl.ds(r0, SUBLANES), :]
        b = b_scr[pl.ds(r0, SUBLANES), :]
        for k in (1, 2, 4):
            if reverse:
                keep = row < SUBLANES - k
                shift = SUBLANES - k
            else:
                keep = row >= k
                shift = k
            a_sh = jnp.where(keep, pltpu.roll(a, shift, 0), 1.0)
            b_sh = jnp.where(keep, pltpu.roll(b, shift, 0), 0.0)
            b = a * b_sh + b
            a = a * a_sh
        h = b + a * carry
        b_scr[pl.ds(r0, SUBLANES), :] = h
        last = h[0:1, :] if reverse else h[SUBLANES - 1:SUBLANES, :]
        return jnp.broadcast_to(last, (SUBLANES, width))

    carry_scr[...] = lax.fori_loop(0, n_groups, group, carry_scr[...])

    if final:
        o_ref[...] = ((hf_ref[...] + b_scr[...]) * _gelu_tanh(gl_ref[...])).astype(o_ref.dtype)
    else:
        o_ref[...] = b_scr[...]


def _lru_pass(p, conv_w, conv_b, w_a, b_a, w_x, b_x, lam, hf, *, reverse, n_batch, seq_len, ctx_len, tt):
    ta = p.shape[0]
    width = conv_w.shape[1]
    assert ctx_len == tt and seq_len % tt == 0
    n_lat_chunks = seq_len // tt
    ctx_block0 = n_batch * n_lat_chunks
    final = hf is not None

    def chunk_index(b, s, off):
        j = (n_lat_chunks - s) if reverse else (s - 1)
        j = jnp.clip(j + off, 0, n_lat_chunks - 1)
        return jnp.where(s == 0, ctx_block0 + b, b * n_lat_chunks + j)

    def xspec(off):
        return pl.BlockSpec((tt, width), lambda b, s: (chunk_index(b, s, off), 0))

    row_spec = pl.BlockSpec((1, width), lambda b, s: (0, 0))
    w_spec = pl.BlockSpec(w_a.shape, lambda b, s: (0, 0, 0))
    in_specs = [xspec(-1), xspec(0), xspec(1),
                pl.BlockSpec((CONV_W, width), lambda b, s: (0, 0)), row_spec,
                w_spec, row_spec, w_spec, row_spec, row_spec]
    args = [p, p, p, conv_w, conv_b.reshape(1, width), w_a, b_a.reshape(1, width),
            w_x, b_x.reshape(1, width), lam.reshape(1, width)]
    if final:
        in_specs += [xspec(0), pl.BlockSpec((tt, width), lambda b, s: (chunk_index(b, s, 0), 1))]
        args += [hf, p]
    kern = functools.partial(_lru_kernel, tt=tt, reverse=reverse, final=final, n_lat_chunks=n_lat_chunks)
    return pl.pallas_call(
        kern,
        out_shape=jax.ShapeDtypeStruct((ta, width), BF16 if final else F32),
        grid=(n_batch, n_lat_chunks + 1),
        in_specs=in_specs,
        out_specs=xspec(0),
        scratch_shapes=[pltpu.VMEM((tt + 2 * SUBLANES, width), F32), pltpu.VMEM((tt, width), F32),
                        pltpu.VMEM((tt, width), F32), pltpu.VMEM((SUBLANES, width), F32)],
        compiler_params=_cparams(2),
        name="rglru_rev" if reverse else "rglru_fwd",
    )(*args)


LOG2E = float(np.log2(np.e))
LOGIT_SCALE = ATTN_SCALE * LOG2E


def _softmax_parts(s2, extra_logit2=None):
    m = jnp.max(s2, axis=-1, keepdims=True)
    if extra_logit2 is not None:
        m = jnp.maximum(m, extra_logit2)
    e = jnp.exp2(s2 - m)
    denom = jnp.sum(e, axis=-1, keepdims=True)
    if extra_logit2 is not None:
        denom = denom + jnp.exp2(extra_logit2 - m)
    return e, 1.0 / denom


def _swa_kernel(sink_ref, q_ref, kp_ref, kc_ref, kn_ref, kx_ref, vp_ref, vc_ref, vn_ref, vx_ref, mask_ref, o_ref,
                *, n_kv_heads):
    blk = q_ref.shape[0]
    head = lax.broadcasted_iota(jnp.int32, (GQA_GROUP * blk, 1), 0) // blk
    logits = []
    for kh in range(n_kv_heads):
        ks = slice(kh * HEAD_DIM, (kh + 1) * HEAD_DIM)
        q0 = kh * GQA_GROUP
        q = jnp.concatenate([q_ref[:, (q0 + h) * HEAD_DIM:(q0 + h + 1) * HEAD_DIM] for h in range(GQA_GROUP)],
                            axis=0)
        k = jnp.concatenate([kp_ref[:, ks], kc_ref[:, ks], kn_ref[:, ks], kx_ref[:, ks]], axis=0)
        logits.append(lax.dot_general(q, k, (((1,), (1,)), ((), ())), preferred_element_type=F32))
    for kh in range(n_kv_heads):
        ks = slice(kh * HEAD_DIM, (kh + 1) * HEAD_DIM)
        q0 = kh * GQA_GROUP
        v = jnp.concatenate([vp_ref[:, ks], vc_ref[:, ks], vn_ref[:, ks], vx_ref[:, ks]], axis=0)
        s = logits[kh] * LOGIT_SCALE + mask_ref[...]
        sink = jnp.zeros((GQA_GROUP * blk, 1), F32)
        for h in range(GQA_GROUP):
            sink = jnp.where(head == h, sink_ref[q0 + h] * LOG2E, sink)
        e, inv = _softmax_parts(s, sink)
        o = jnp.dot(e.astype(BF16), v, preferred_element_type=F32) * inv
        for h in range(GQA_GROUP):
            o_ref[:, (q0 + h) * HEAD_DIM:(q0 + h + 1) * HEAD_DIM] = o[h * blk:(h + 1) * blk, :].astype(o_ref.dtype)


def _swa_mask_table(blk, ctx_len):
    shape = (4, GQA_GROUP * blk, 3 * blk + ctx_len)
    cls = lax.broadcasted_iota(jnp.int32, shape, 0)
    qi = lax.broadcasted_iota(jnp.int32, shape, 1) % blk
    col = lax.broadcasted_iota(jnp.int32, shape, 2)
    lo = jnp.where(cls == 0, blk, 0)
    hi = jnp.where(cls == 3, 0, jnp.where(cls == 2, 2 * blk, 3 * blk))
    rel = col - blk - qi
    ok = ((col >= lo) & (col < hi) & (rel >= -WINDOW) & (rel <= WINDOW)) | (col >= 3 * blk)
    return jnp.where(ok, 0.0, NEG).astype(F32)


def _swa(qkv, sink, *, n_batch, seq_len, ctx_len, n_kv_heads, blk):
    ta = qkv.shape[0]
    n_heads = n_kv_heads * GQA_GROUP
    n_lat_blocks = seq_len // blk
    n_ctx_blocks = ctx_len // blk
    assert blk == WINDOW and seq_len % blk == 0 and ctx_len % blk == 0 and n_lat_blocks >= 2
    ctx_q0 = n_batch * n_lat_blocks
    ctx_kv0 = n_batch * seq_len // ctx_len
    kv_w = n_kv_heads * HEAD_DIM
    k_col, v_col = n_heads * HEAD_DIM // kv_w, n_heads * HEAD_DIM // kv_w + 1

    def q_index(b, n):
        return (jnp.where(n < n_lat_blocks, b * n_lat_blocks + n, ctx_q0 + b * n_ctx_blocks + n - n_lat_blocks), 0)

    def kv_spec(off, col):
        return pl.BlockSpec((blk, kv_w),
                            lambda b, n: (b * n_lat_blocks + jnp.clip(n + off, 0, n_lat_blocks - 1), col))

    def ctx_spec(col):
        return pl.BlockSpec((ctx_len, kv_w), lambda b, n: (ctx_kv0 + b, col))

    def mask_index(b, n):
        interior = jnp.where(n == 0, 0, jnp.where(n == n_lat_blocks - 1, 2, 1))
        return (jnp.where(n < n_lat_blocks, interior, 3), 0, 0)

    mask = _swa_mask_table(blk, ctx_len)
    kern = functools.partial(_swa_kernel, n_kv_heads=n_kv_heads)
    return pl.pallas_call(
        kern,
        out_shape=jax.ShapeDtypeStruct((ta, n_heads * HEAD_DIM), BF16),
        grid=(n_batch, n_lat_blocks + n_ctx_blocks),
        in_specs=[pl.BlockSpec(memory_space=pltpu.SMEM),
                  pl.BlockSpec((blk, n_heads * HEAD_DIM), q_index),
                  kv_spec(-1, k_col), kv_spec(0, k_col), kv_spec(1, k_col), ctx_spec(k_col),
                  kv_spec(-1, v_col), kv_spec(0, v_col), kv_spec(1, v_col), ctx_spec(v_col),
                  pl.BlockSpec((None,) + mask.shape[1:], mask_index)],
        out_specs=pl.BlockSpec((blk, n_heads * HEAD_DIM), q_index),
        compiler_params=_cparams(2),
        name="windowed_gqa",
    )(sink, *([qkv] * 9), mask)


NA_HEADS_PER_STEP = 8


def _na_kernel(q_ref, kvp_ref, kvc_ref, kvn_ref, kvx_ref, bias_ref, o_ref):
    n_local = bias_ref.shape[-1]
    gw = NA_HEADS_PER_STEP * HEAD_DIM
    kv_refs = (kvp_ref, kvc_ref, kvn_ref, kvx_ref)
    heads = [slice(h * HEAD_DIM, (h + 1) * HEAD_DIM) for h in range(NA_HEADS_PER_STEP)]
    def qk(h):
        k = jnp.concatenate([ref[:, heads[h]] for ref in kv_refs], axis=0)
        return lax.dot_general(q_ref[:, heads[h]], k, (((1,), (1,)), ((), ())), preferred_element_type=F32)

    def softmax(h, logit):
        s = logit * LOGIT_SCALE
        s = jnp.concatenate([s[:, :n_local] + bias_ref[h], s[:, n_local:]], axis=1)
        e, inv = _softmax_parts(s)
        return e.astype(BF16), inv

    def pv(h, e, inv):
        vs = slice(gw + heads[h].start, gw + heads[h].stop)
        v = jnp.concatenate([ref[:, vs] for ref in kv_refs], axis=0)
        o_ref[:, heads[h]] = (jnp.dot(e, v, preferred_element_type=F32) * inv).astype(o_ref.dtype)

    n = NA_HEADS_PER_STEP
    logits = {0: qk(0)}
    weights = {}
    for h in range(n):
        if h + 1 < n:
            logits[h + 1] = qk(h + 1)
        weights[h] = softmax(h, logits.pop(h))
        if h >= 1:
            pv(h - 1, *weights.pop(h - 1))
    pv(n - 1, *weights.pop(n - 1))


def _na_bias_table(rpb, n_grid_rows):
    r_in, w = NA_ROWS_PER_BLOCK, GRID_W
    kh = min(NA_KH, n_grid_rows)
    n_blocks = n_grid_rows // r_in
    n_ro, n_co = 2 * NA_KH - 1, 2 * NA_KW - 1
    qc, kc = np.arange(w)[:, None], np.arange(w)[None, :]
    e_col = (kc - qc + NA_KW - 1 == np.arange(n_co)[:, None, None]).astype(np.float32)
    win_start = np.clip(qc - NA_KW // 2, 0, w - NA_KW)
    col_ok = (kc >= win_start) & (kc < win_start + NA_KW)
    ri, krj = np.arange(r_in)[:, None], np.arange(3 * r_in)[None, :]
    row_ok = []
    for jb in (0, max(n_blocks // 2, 1) if n_blocks > 2 else 0, n_blocks - 1):
        r, kr = r_in * jb + ri, r_in * (jb - 1) + krj
        rs = np.clip(r - kh // 2, 0, n_grid_rows - kh)
        row_ok.append((kr >= rs) & (kr < rs + kh))
    ok = np.stack(row_ok)[:, :, None, :, None] & col_ok[None, None, :, None, :]
    ok = ok.reshape(3, r_in * w, 3 * r_in * w)
    tz = jnp.einsum('hab,bqc->haqc', rpb.astype(F32), e_col, precision=lax.Precision.HIGHEST) * LOG2E
    assert r_in - 1 <= NA_KH - 1 - r_in and 3 * r_in - 1 - r_in + NA_KH - 1 < n_ro
    full = jnp.concatenate(
        [jnp.concatenate([tz[:, k - i - r_in + NA_KH - 1] for k in range(3 * r_in)], axis=-1) for i in range(r_in)],
        axis=-2)
    return jnp.where(ok[None], full[:, None], NEG)


def _na(qkv, bias, *, n_batch, seq_len, ctx_len, n_heads):
    tq = NA_ROWS_PER_BLOCK * GRID_W
    n_blocks = seq_len // tq
    assert ctx_len == tq and seq_len % tq == 0 and NA_KH == 2 * NA_ROWS_PER_BLOCK and n_blocks >= 2
    assert n_heads % NA_HEADS_PER_STEP == 0
    n_groups = n_heads // NA_HEADS_PER_STEP
    gw = NA_HEADS_PER_STEP * HEAD_DIM
    ctx0 = n_batch * n_blocks

    def kv_spec(off):
        return pl.BlockSpec((tq, 2 * gw),
                            lambda b, g, jb: (b * n_blocks + jnp.clip(jb + off, 0, n_blocks - 1), g))

    def bias_index(b, g, jb):
        return (g, jnp.where(jb == 0, 0, jnp.where(jb == n_blocks - 1, 2, 1)), 0, 0)

    return pl.pallas_call(
        _na_kernel,
        out_shape=jax.ShapeDtypeStruct((n_batch * seq_len, n_heads * HEAD_DIM), BF16),
        grid=(n_batch, n_groups, n_blocks),
        in_specs=[pl.BlockSpec((tq, gw), lambda b, g, jb: (b * n_blocks + jb, 2 * n_groups + g)),
                  kv_spec(-1), kv_spec(0), kv_spec(1),
                  pl.BlockSpec((ctx_len, 2 * gw), lambda b, g, jb: (ctx0 + b, g)),
                  pl.BlockSpec((NA_HEADS_PER_STEP, None) + bias.shape[2:], bias_index)],
        out_specs=pl.BlockSpec((tq, gw), lambda b, g, jb: (b * n_blocks + jb, g)),
        compiler_params=_cparams(3),
        name="neighbourhood_attn",
    )(*([qkv] * 5), bias)


def _tiles():
    return dict(tm=1024, tm_two_source=512, tm_proj=512, tm_out=512, tf=512, tn_mod=1024)


def _chunked(total, out_idx, out_col0=0, w_col0=0):
    return [(w_col0 + c, min(PROJ_CHUNK, total - c), out_idx, out_col0 + c, None)
            for c in range(0, total, PROJ_CHUNK)]


def _head_chunks(head_ops, w_col0, out_idx, out_col0):
    per = PROJ_CHUNK // HEAD_DIM
    return [(w_col0 + h0 * HEAD_DIM, len(head_ops[h0:h0 + per]) * HEAD_DIM, out_idx, out_col0 + h0 * HEAD_DIM,
             tuple(head_ops[h0:h0 + per])) for h0 in range(0, len(head_ops), per)]


def kernel(x, c, ctx, c_ctx, w_mod, b_mod, norm_g, ffn_w_in, ffn_w_out, ab_w_in, lru_conv_w, lru_conv_b,
           lru_w_a, lru_b_a, lru_w_x, lru_b_x, lru_lambda, attn_q_norm, attn_k_norm, attn_sink, ab_w_out,
           na_w_in, na_q_norm, na_k_norm, na_rpb, na_w_out):
    n_batch, seq_len, d = x.shape
    ctx_len = ctx.shape[1]
    depth = w_mod.shape[0]
    lru_w = lru_conv_w.shape[2]
    n_kv = attn_sink.shape[1] // GQA_GROUP
    n_att = n_kv * GQA_GROUP
    na_heads = na_rpb.shape[1]
    assert n_batch < MOD_ROWS and depth == 2
    t = _tiles()
    tl, tc = n_batch * seq_len, n_batch * ctx_len
    geom = dict(rows_per_batch=seq_len, n_batch=n_batch)

    cc = jnp.zeros((MOD_ROWS, d), F32).at[:n_batch].set(c).at[n_batch].set(c_ctx)
    mod = _mod_table(cc, w_mod, b_mod, t["tn_mod"])
    wg, wu, wo = _cast_ffn_weights(ffn_w_in, ffn_w_out)

    def ffn(x_src, n_rows, layer, which, x_src2=None):
        tm = t["tm"] if x_src2 is None else t["tm_two_source"]
        return _ffn(x_src, x_src2, n_rows, mod, layer, which, norm_g[layer, 2 * which], wg, wu, wo,
                    tm=tm, tf=t["tf"], **geom)

    cos_t, sin_t = _rope_tables(seq_len)
    proj_geom = dict(tm=t["tm_proj"], n_lat_rows=tl, **geom)

    x_all = ffn(x.reshape(tl, d), tl + tc, 0, 0, x_src2=ctx.reshape(tc, d))
    gains = jnp.stack([attn_q_norm[0], attn_k_norm[0]])
    head_ops = [(0, True)] * n_att + [(1, True)] * n_kv + [None] * n_kv
    plan = _head_chunks(head_ops, 2 * lru_w, 1, 0) + _chunked(2 * lru_w, 0)
    p, qkv = _inproj(x_all, mod, 0, norm_g[0, 1], ab_w_in[0].astype(BF16), gains, cos_t, sin_t, plan,
                     ((2 * lru_w, F32), (len(head_ops) * HEAD_DIM, BF16)), **proj_geom)
    lru_args = lambda dr: (lru_conv_w[0], lru_conv_b[0], lru_w_a[0, dr].astype(BF16), lru_b_a[0, dr],
                           lru_w_x[0, dr].astype(BF16), lru_b_x[0, dr], lru_lambda[0, dr])
    scan_geom = dict(n_batch=n_batch, seq_len=seq_len, ctx_len=ctx_len, tt=ctx_len)
    h_fwd = _lru_pass(p, *lru_args(0), None, reverse=False, **scan_geom)
    lru = _lru_pass(p, *lru_args(1), h_fwd, reverse=True, **scan_geom)
    att = _swa(qkv, attn_sink[0], n_batch=n_batch, seq_len=seq_len, ctx_len=ctx_len, n_kv_heads=n_kv,
               blk=WINDOW)
    x_all = _outproj(lru, att, 0, ab_w_out[0].astype(BF16), x_all, tl + tc, mod, 0, tm=t["tm_out"], **geom)
    x_all = ffn(x_all, tl + tc, 0, 1)

    x_all = ffn(x_all, tl + tc, 1, 0)
    gains = jnp.stack([na_q_norm[0], na_k_norm[0]])
    na_d = na_heads * HEAD_DIM
    gw = NA_HEADS_PER_STEP * HEAD_DIM
    groups = range(0, na_d, gw)
    plan = sum([_head_chunks([(0, False)] * NA_HEADS_PER_STEP, g0, 0, 2 * na_d + g0) for g0 in groups], [])
    plan += sum([_head_chunks([(1, False)] * NA_HEADS_PER_STEP, na_d + g0, 0, 2 * g0) for g0 in groups], [])
    plan += sum([_head_chunks([None] * NA_HEADS_PER_STEP, 2 * na_d + g0, 0, 2 * g0 + gw) for g0 in groups], [])
    (qkv,) = _inproj(x_all, mod, 1, norm_g[1, 1], na_w_in[0].astype(BF16), gains, cos_t, sin_t, plan,
                     ((3 * na_d, BF16),), **proj_geom)
    o = _na(qkv, _na_bias_table(na_rpb[0], seq_len // GRID_W), n_batch=n_batch, seq_len=seq_len, ctx_len=ctx_len,
            n_heads=na_heads)
    x_lat = _outproj(o, o, 1, na_w_out[0].astype(BF16), x_all, tl, mod, 1, tm=t["tm_out"], **geom)
    x_lat = ffn(x_lat, tl, 1, 1)
    return x_lat.reshape(n_batch, seq_len, d)
```

```python
import functools

import numpy as np
import jax
import jax.numpy as jnp
from jax import lax
from jax.experimental import pallas as pl
from jax.experimental.pallas import tpu as pltpu

HEAD_DIM = 128
EPS = 1e-6
N_MOD = 9
GRID_W = 64
CONV_W = 4
LRU_C = 8.0
WINDOW = 128
ROPE_THETA = 10000.0
NA_KH = 8
NA_KW = 16
ATTN_SCALE = HEAD_DIM ** -0.5
GQA_GROUP = 4

LANES = 128
SUBLANES = 8
VMEM_LIMIT_BYTES = 56 * 1024 * 1024
MOD_ROWS = SUBLANES

NEG = -1e30
NA_ROWS_PER_BLOCK = 4

BF16 = jnp.bfloat16
F32 = jnp.float32


def _cparams(n_axes):
    return pltpu.CompilerParams(dimension_semantics=("arbitrary",) * n_axes,
                                vmem_limit_bytes=VMEM_LIMIT_BYTES)


def _sigmoid(x):
    return 1.0 / (1.0 + jnp.exp(-x))


def _sigmoid_tanh(x):
    return 0.5 * jnp.tanh(0.5 * x) + 0.5


NORM_CHUNK_ROWS = 16


def _norm_modulate_into(h_scr, x_ref, g, shift, scale, alt=None):
    gain = g * (1.0 + scale)

    def chunk(c, carry):
        r0 = pl.multiple_of(c * NORM_CHUNK_ROWS, NORM_CHUNK_ROWS)
        x = x_ref[pl.ds(r0, NORM_CHUNK_ROWS), :]
        if alt is not None:
            x = jnp.where(alt[0], alt[1][pl.ds(r0, NORM_CHUNK_ROWS), :], x)
        inv = lax.rsqrt(jnp.mean(x * x, axis=-1, keepdims=True) + EPS)
        h_scr[pl.ds(r0, NORM_CHUNK_ROWS), :] = ((x * inv) * gain + shift).astype(h_scr.dtype)
        return carry

    lax.fori_loop(0, x_ref.shape[0] // NORM_CHUNK_ROWS, chunk, 0, unroll=4)


def _mod_kernel(c_ref, w_ref, b_ref, o_ref):
    c = c_ref[...]
    s = c * _sigmoid(c)
    o_ref[...] = jnp.dot(s.astype(BF16), w_ref[...].astype(BF16),
                         preferred_element_type=F32) + b_ref[...]


def _mod_table(cc, w_mod, b_mod, tn):
    depth, d, n = w_mod.shape
    return pl.pallas_call(
        _mod_kernel,
        out_shape=jax.ShapeDtypeStruct((depth, MOD_ROWS, n), F32),
        grid=(depth, n // tn),
        in_specs=[pl.BlockSpec((MOD_ROWS, d), lambda l, j: (0, 0)),
                  pl.BlockSpec((None, d, tn), lambda l, j: (l, 0, j)),
                  pl.BlockSpec((None, 1, tn), lambda l, j: (l, 0, j))],
        out_specs=pl.BlockSpec((None, MOD_ROWS, tn), lambda l, j: (l, 0, j)),
        compiler_params=_cparams(2),
        name="mod_table",
    )(cc, w_mod, b_mod.reshape(depth, 1, n))


def _mod_spec(d, layer, k, n_grid_axes):
    if n_grid_axes == 1:
        return pl.BlockSpec((None, MOD_ROWS, d), lambda i: (layer, 0, k))
    return pl.BlockSpec((None, MOD_ROWS, d), lambda i, j: (layer, 0, k))


def _mod_row(tile_idx, tm, rows_per_batch, n_batch):
    return jnp.minimum(tile_idx * tm // rows_per_batch, n_batch)


def _ffn_kernel(x_ref, g_ref, sh_ref, sc_ref, gt_ref, wg_ref, wu_ref, wo_ref, o_ref, h_scr,
                *, tm, tile0, rows_per_batch, n_batch, last_width):
    i, j = pl.program_id(0), pl.program_id(1)
    last = pl.num_programs(1) - 1
    r = _mod_row(i + tile0, tm, rows_per_batch, n_batch)

    @pl.when(j == 0)
    def _():
        _norm_modulate_into(h_scr, x_ref, g_ref[...], sh_ref[pl.ds(r, 1), :], sc_ref[pl.ds(r, 1), :])
        o_ref[...] = jnp.zeros_like(o_ref)

    def hidden_block(width):
        h = h_scr[...]
        a = jnp.dot(h, wg_ref[:, :width], preferred_element_type=F32)
        u = jnp.dot(h, wu_ref[:, :width], preferred_element_type=F32)
        act = (a * _sigmoid(a)) * u
        o_ref[...] += jnp.dot(act.astype(BF16), wo_ref[:width, :], preferred_element_type=F32)

    if last_width == wg_ref.shape[1]:
        hidden_block(last_width)
    else:
        pl.when(j < last)(lambda: hidden_block(wg_ref.shape[1]))
        pl.when(j == last)(lambda: hidden_block(last_width))

    @pl.when(j == last)
    def _():
        o_ref[...] = x_ref[...] + 0.5 * gt_ref[pl.ds(r, 1), :] * o_ref[...]


def _ffn(x_src, n_rows, row0, mod, layer, which, g, wg, wu, wo, *, tm, tf, rows_per_batch, n_batch):
    d = x_src.shape[1]
    d_ff = wo.shape[2]
    nf = pl.cdiv(d_ff, tf)
    last_width = d_ff - (nf - 1) * tf
    assert last_width % LANES == 0 and row0 % tm == 0
    k0 = 6 * which
    kern = functools.partial(_ffn_kernel, tm=tm, tile0=row0 // tm, rows_per_batch=rows_per_batch,
                             n_batch=n_batch, last_width=last_width)
    return pl.pallas_call(
        kern,
        out_shape=jax.ShapeDtypeStruct((n_rows, d), F32),
        grid=(n_rows // tm, nf),
        in_specs=[pl.BlockSpec((tm, d), lambda i, j: (i, 0)),
                  pl.BlockSpec((1, d), lambda i, j: (0, 0)),
                  _mod_spec(d, layer, k0, 2), _mod_spec(d, layer, k0 + 1, 2), _mod_spec(d, layer, k0 + 2, 2),
                  pl.BlockSpec((None, None, d, tf), lambda i, j: (layer, which, 0, j)),
                  pl.BlockSpec((None, None, d, tf), lambda i, j: (layer, which, 0, j)),
                  pl.BlockSpec((None, None, tf, d), lambda i, j: (layer, which, j, 0))],
        out_specs=pl.BlockSpec((tm, d), lambda i, j: (i, 0)),
        scratch_shapes=[pltpu.VMEM((tm, d), BF16)],
        compiler_params=_cparams(2),
        name="half_ffn",
    )(x_src, g.reshape(1, d), mod, mod, mod, wg, wu, wo)


def _cast_kernel(*refs):
    n = len(refs) // 2
    for src, dst in zip(refs[:n], refs[n:]):
        dst[...] = src[...].astype(dst.dtype)


def _largest_tile(n, multiple, cap):
    return max(t for t in range(multiple, min(n, cap) + 1, multiple) if n % t == 0)


def _cast_ffn_weights(w_in, w_out):
    depth, n_ffn, d, f2 = w_in.shape
    f = f2 // 2
    assert f % LANES == 0
    rows = _largest_tile(d, 2 * SUBLANES, 256)
    half = (None, None, rows, f)
    wg, wu = pl.pallas_call(
        _cast_kernel,
        out_shape=[jax.ShapeDtypeStruct((depth, n_ffn, d, f), BF16)] * 2,
        grid=(depth, n_ffn, d // rows),
        in_specs=[pl.BlockSpec(half, lambda l, w, r: (l, w, r, 0)), pl.BlockSpec(half, lambda l, w, r: (l, w, r, 1))],
        out_specs=[pl.BlockSpec(half, lambda l, w, r: (l, w, r, 0))] * 2,
        compiler_params=_cparams(3),
        name="cast_ffn_in",
    )(w_in, w_in)
    rows = _largest_tile(f, 2 * SUBLANES, 1024)
    blk = (None, None, rows, d)
    wo = pl.pallas_call(
        _cast_kernel,
        out_shape=jax.ShapeDtypeStruct(w_out.shape, BF16),
        grid=(depth, n_ffn, f // rows),
        in_specs=[pl.BlockSpec(blk, lambda l, w, r: (l, w, r, 0))],
        out_specs=pl.BlockSpec(blk, lambda l, w, r: (l, w, r, 0)),
        compiler_params=_cparams(3),
        name="cast_ffn_out",
    )(w_out)
    return wg, wu, wo


PROJ_CHUNK = 4 * HEAD_DIM


def _swap_halves_32(y):
    lane = lax.broadcasted_iota(jnp.int32, y.shape, 1)
    return jnp.where((lane & 32) == 0, pltpu.roll(y, LANES - 32, 1), pltpu.roll(y, 32, 1))


def _inproj_kernel(x_ref, g_ref, sh_ref, sc_ref, w_ref, gains_ref, cos_ref, sin_ref, *rest,
                   plan, tm, rows_per_batch, n_batch, n_lat_tiles, n_tiles1):
    i = pl.program_id(0)
    alt = (i >= n_tiles1, rest[0]) if n_tiles1 is not None else None
    out_refs, h_scr = rest[(0 if alt is None else 1):-1], rest[-1]
    r = _mod_row(i, tm, rows_per_batch, n_batch)
    is_lat = i < n_lat_tiles
    _norm_modulate_into(h_scr, x_ref, g_ref[...], sh_ref[pl.ds(r, 1), :], sc_ref[pl.ds(r, 1), :], alt)
    for w_col, width, out_idx, out_col, ops in plan:
        y = jnp.dot(h_scr[...], w_ref[:, w_col:w_col + width], preferred_element_type=F32)
        o_ref = out_refs[out_idx]
        if ops is None:
            o_ref[:, out_col:out_col + width] = y.astype(o_ref.dtype)
            continue
        for hh, op in enumerate(ops):
            yh = y[:, hh * HEAD_DIM:(hh + 1) * HEAD_DIM]
            if op is not None:
                gain_row, rope = op
                yh = yh * lax.rsqrt(jnp.mean(yh * yh, axis=-1, keepdims=True) + EPS)
                yh = yh * gains_ref[gain_row:gain_row + 1, :]
                if rope:
                    yh = jnp.where(is_lat, yh * cos_ref[...] + _swap_halves_32(yh) * sin_ref[...], yh)
            o_ref[:, out_col + hh * HEAD_DIM:out_col + (hh + 1) * HEAD_DIM] = yh.astype(o_ref.dtype)


def _stream_specs(x_src, x_src2, tm):
    d = x_src.shape[1]
    if x_src2 is None:
        return [pl.BlockSpec((tm, d), lambda i: (i, 0))], None
    n1 = x_src.shape[0] // tm
    return [pl.BlockSpec((tm, d), lambda i: (jnp.minimum(i, n1 - 1), 0)),
            pl.BlockSpec((tm, d), lambda i: (jnp.maximum(i - n1, 0), 0))], n1


def _inproj(x_src, x_src2, mod, layer, g, w, gains, cos_t, sin_t, plan, outs, *, tm, n_lat_rows, rows_per_batch,
            n_batch):
    d = x_src.shape[1]
    ta = x_src.shape[0] + (0 if x_src2 is None else x_src2.shape[0])
    n_pos_tiles = cos_t.shape[0] // tm
    (x_spec, *x2_spec), n_tiles1 = _stream_specs(x_src, x_src2, tm)
    kern = functools.partial(_inproj_kernel, plan=tuple(plan), tm=tm, rows_per_batch=rows_per_batch,
                             n_batch=n_batch, n_lat_tiles=n_lat_rows // tm, n_tiles1=n_tiles1)
    return pl.pallas_call(
        kern,
        out_shape=[jax.ShapeDtypeStruct((ta, n), dt) for n, dt in outs],
        grid=(ta // tm,),
        in_specs=[x_spec,
                  pl.BlockSpec((1, d), lambda i: (0, 0)),
                  _mod_spec(d, layer, 3, 1), _mod_spec(d, layer, 4, 1),
                  pl.BlockSpec(w.shape, lambda i: (0, 0), pipeline_mode=pl.Buffered(1)),
                  pl.BlockSpec(gains.shape, lambda i: (0, 0)),
                  pl.BlockSpec((tm, HEAD_DIM), lambda i: (i % n_pos_tiles, 0)),
                  pl.BlockSpec((tm, HEAD_DIM), lambda i: (i % n_pos_tiles, 0))] + x2_spec,
        out_specs=[pl.BlockSpec((tm, n), lambda i: (i, 0)) for n, _ in outs],
        scratch_shapes=[pltpu.VMEM((tm, d), BF16)],
        compiler_params=_cparams(1),
        name="mixer_inproj",
    )(x_src, g.reshape(1, d), mod, mod, w, gains, cos_t, sin_t, *([] if x_src2 is None else [x_src2]))


def _outproj_kernel(a1_ref, a2_ref, w1_ref, w2_ref, gt_ref, x_ref, *rest, tm, rows_per_batch, n_batch, n_tiles1):
    o_ref = rest[-1]
    i = pl.program_id(0)
    r = _mod_row(i, tm, rows_per_batch, n_batch)
    d = o_ref.shape[1]
    for c0 in range(0, d, PROJ_CHUNK):
        cs = slice(c0, min(c0 + PROJ_CHUNK, d))
        y = (jnp.dot(a1_ref[...], w1_ref[:, cs], preferred_element_type=F32)
             + jnp.dot(a2_ref[...], w2_ref[:, cs], preferred_element_type=F32))
        x = x_ref[:, cs] if n_tiles1 is None else jnp.where(i >= n_tiles1, rest[0][:, cs], x_ref[:, cs])
        o_ref[:, cs] = x + gt_ref[pl.ds(r, 1), cs] * y


def _outproj(a1, a2, a2_col_block, w, x_src, x_src2, n_rows, mod, layer, *, tm, rows_per_batch, n_batch):
    d = x_src.shape[1]
    kh = w.shape[0] // 2
    x_specs, n_tiles1 = _stream_specs(x_src, x_src2, tm)
    kern = functools.partial(_outproj_kernel, tm=tm, rows_per_batch=rows_per_batch, n_batch=n_batch,
                             n_tiles1=n_tiles1)
    return pl.pallas_call(
        kern,
        out_shape=jax.ShapeDtypeStruct((n_rows, d), F32),
        grid=(n_rows // tm,),
        in_specs=[pl.BlockSpec((tm, kh), lambda i: (i, 0)),
                  pl.BlockSpec((tm, kh), lambda i: (i, a2_col_block)),
                  pl.BlockSpec((kh, d), lambda i: (0, 0)),
                  pl.BlockSpec((kh, d), lambda i: (1, 0)),
                  _mod_spec(d, layer, 5, 1)] + x_specs,
        out_specs=pl.BlockSpec((tm, d), lambda i: (i, 0)),
        compiler_params=_cparams(1),
        name="mixer_outproj",
    )(a1, a2, w, w, mod, x_src, *([] if x_src2 is None else [x_src2]))


def _rope_tables(seq_len):
    half = HEAD_DIM // 2
    nf = half // 2
    inv = ROPE_THETA ** (-jnp.arange(nf, dtype=F32) / nf)
    pos = jnp.arange(seq_len, dtype=jnp.int32)
    ang_r = (pos // GRID_W).astype(F32)[:, None] * inv[None, :]
    ang_c = (pos % GRID_W).astype(F32)[:, None] * inv[None, :]
    cr, sr, cc, sc = jnp.cos(ang_r), jnp.sin(ang_r), jnp.cos(ang_c), jnp.sin(ang_c)
    return (jnp.concatenate([cr, cr, cc, cc], axis=-1),
            jnp.concatenate([-sr, sr, -sc, sc], axis=-1))


def _gelu_tanh(x):
    return x * (0.5 * (1.0 + jnp.tanh(np.float32(np.sqrt(2.0 / np.pi)) * (x + 0.044715 * (x * x * x)))))


def _lru_kernel(*refs, tt, reverse, final, n_lat_chunks):
    if final:
        (xp_ref, xc_ref, xn_ref, cw_ref, cb_ref, wa_ref, ba_ref, wx_ref, bx_ref, lam_ref,
         hf_ref, gl_ref, o_ref, xs_scr, a_scr, b_scr, carry_scr) = refs
    else:
        (xp_ref, xc_ref, xn_ref, cw_ref, cb_ref, wa_ref, ba_ref, wx_ref, bx_ref, lam_ref,
         o_ref, xs_scr, a_scr, b_scr, carry_scr) = refs
    s = pl.program_id(1)
    is_ctx = s == 0
    j = (n_lat_chunks - s) if reverse else (s - 1)
    has_prev = jnp.logical_and(jnp.logical_not(is_ctx), j > 0)
    has_next = jnp.logical_and(jnp.logical_not(is_ctx), j < n_lat_chunks - 1)

    @pl.when(is_ctx)
    def _():
        carry_scr[...] = jnp.zeros_like(carry_scr)

    halo = SUBLANES
    xs_scr[0:halo, :] = jnp.where(has_prev, xp_ref[tt - halo:tt, :], 0.0)
    xs_scr[halo:halo + tt, :] = xc_ref[...]
    xs_scr[halo + tt:2 * halo + tt, :] = jnp.where(has_next, xn_ref[0:halo, :], 0.0)

    width = xc_ref.shape[1]
    left = CONV_W // 2
    z = -lam_ref[...]
    softplus = jnp.maximum(z, 0.0) + jnp.log1p(jnp.exp(-jnp.abs(z)))
    for n in range(width // HEAD_DIM):
        cs = slice(n * HEAD_DIM, (n + 1) * HEAD_DIM)
        u = cb_ref[:, cs]
        xa = xs_scr[:, cs]
        n_rows = xa.shape[0]
        for k in range(CONV_W):
            xk = xa if k == left else pltpu.roll(xa, (left - k) % n_rows, 0)
            u = u + xk[halo:halo + tt, :] * cw_ref[k:k + 1, cs]
        ub = u.astype(BF16)
        gate_r = jnp.dot(ub, wa_ref[n], preferred_element_type=F32) + ba_ref[:, cs]
        gate_i = jnp.dot(ub, wx_ref[n], preferred_element_type=F32) + bx_ref[:, cs]
        log_a = (-LRU_C) * _sigmoid_tanh(gate_r) * softplus[:, cs]
        a = jnp.exp(log_a)
        a_scr[:, cs] = a
        b_scr[:, cs] = jnp.sqrt(-jnp.tanh(log_a) * (1.0 + a * a)) * (_sigmoid_tanh(gate_i) * u)

    n_groups = tt // SUBLANES
    row = lax.broadcasted_iota(jnp.int32, (SUBLANES, width), 0)

    def group(gi, carry):
        g = (n_groups - 1 - gi) if reverse else gi
        r0 = pl.multiple_of(g * SUBLANES, SUBLANES)
        a = a_scr[pl.ds(r0, SUBLANES), :]
        b = b_scr[pl.ds(r0, SUBLANES), :]
        for k in (1, 2, 4):
            if reverse:
                keep = row < SUBLANES - k
                shift = SUBLANES - k
            else:
                keep = row >= k
                shift = k
            a_sh = jnp.where(keep, pltpu.roll(a, shift, 0), 1.0)
            b_sh = jnp.where(keep, pltpu.roll(b, shift, 0), 0.0)
            b = a * b_sh + b
            a = a * a_sh
        h = b + a * carry
        b_scr[pl.ds(r0, SUBLANES), :] = h
        last = h[0:1, :] if reverse else h[SUBLANES - 1:SUBLANES, :]
        return jnp.broadcast_to(last, (SUBLANES, width))

    carry_scr[...] = lax.fori_loop(0, n_groups, group, carry_scr[...])

    if final:
        o_ref[...] = ((hf_ref[...] + b_scr[...]) * _gelu_tanh(gl_ref[...])).astype(o_ref.dtype)
    else:
        o_ref[...] = b_scr[...]


def _lru_pass(p, conv_w, conv_b, w_a, b_a, w_x, b_x, lam, hf, *, reverse, n_batch, seq_len, ctx_len, tt):
    ta = p.shape[0]
    width = conv_w.shape[1]
    assert ctx_len == tt and seq_len % tt == 0
    n_lat_chunks = seq_len // tt
    ctx_block0 = n_batch * n_lat_chunks
    final = hf is not None

    def chunk_index(b, s, off):
        j = (n_lat_chunks - s) if reverse else (s - 1)
        j = jnp.clip(j + off, 0, n_lat_chunks - 1)
        return jnp.where(s == 0, ctx_block0 + b, b * n_lat_chunks + j)

    def xspec(off):
        return pl.BlockSpec((tt, width), lambda b, s: (chunk_index(b, s, off), 0))

    row_spec = pl.BlockSpec((1, width), lambda b, s: (0, 0))
    w_spec = pl.BlockSpec(w_a.shape, lambda b, s: (0, 0, 0))
    in_specs = [xspec(-1), xspec(0), xspec(1),
                pl.BlockSpec((CONV_W, width), lambda b, s: (0, 0)), row_spec,
                w_spec, row_spec, w_spec, row_spec, row_spec]
    args = [p, p, p, conv_w, conv_b.reshape(1, width), w_a, b_a.reshape(1, width),
            w_x, b_x.reshape(1, width), lam.reshape(1, width)]
    if final:
        in_specs += [xspec(0), pl.BlockSpec((tt, width), lambda b, s: (chunk_index(b, s, 0), 1))]
        args += [hf, p]
    kern = functools.partial(_lru_kernel, tt=tt, reverse=reverse, final=final, n_lat_chunks=n_lat_chunks)
    return pl.pallas_call(
        kern,
        out_shape=jax.ShapeDtypeStruct((ta, width), BF16 if final else F32),
        grid=(n_batch, n_lat_chunks + 1),
        in_specs=in_specs,
        out_specs=xspec(0),
        scratch_shapes=[pltpu.VMEM((tt + 2 * SUBLANES, width), F32), pltpu.VMEM((tt, width), F32),
                        pltpu.VMEM((tt, width), F32), pltpu.VMEM((SUBLANES, width), F32)],
        compiler_params=_cparams(2),
        name="rglru_rev" if reverse else "rglru_fwd",
    )(*args)


LOG2E = float(np.log2(np.e))
LOGIT_SCALE = ATTN_SCALE * LOG2E


def _softmax_parts(s2, extra_logit2=None):
    m = jnp.max(s2, axis=-1, keepdims=True)
    if extra_logit2 is not None:
        m = jnp.maximum(m, extra_logit2)
    e = jnp.exp2(s2 - m)
    denom = jnp.sum(e, axis=-1, keepdims=True)
    if extra_logit2 is not None:
        denom = denom + jnp.exp2(extra_logit2 - m)
    return e, 1.0 / denom


def _swa_kernel(sink_ref, q_ref, kp_ref, kc_ref, kn_ref, kx_ref, vp_ref, vc_ref, vn_ref, vx_ref, mask_ref, o_ref,
                *, n_kv_heads):
    blk = q_ref.shape[0]
    head = lax.broadcasted_iota(jnp.int32, (GQA_GROUP * blk, 1), 0) // blk
    logits = []
    for kh in range(n_kv_heads):
        ks = slice(kh * HEAD_DIM, (kh + 1) * HEAD_DIM)
        q0 = kh * GQA_GROUP
        q = jnp.concatenate([q_ref[:, (q0 + h) * HEAD_DIM:(q0 + h + 1) * HEAD_DIM] for h in range(GQA_GROUP)],
                            axis=0)
        k = jnp.concatenate([kp_ref[:, ks], kc_ref[:, ks], kn_ref[:, ks], kx_ref[:, ks]], axis=0)
        logits.append(lax.dot_general(q, k, (((1,), (1,)), ((), ())), preferred_element_type=F32))
    for kh in range(n_kv_heads):
        ks = slice(kh * HEAD_DIM, (kh + 1) * HEAD_DIM)
        q0 = kh * GQA_GROUP
        v = jnp.concatenate([vp_ref[:, ks], vc_ref[:, ks], vn_ref[:, ks], vx_ref[:, ks]], axis=0)
        s = logits[kh] * LOGIT_SCALE + mask_ref[...]
        sink = jnp.zeros((GQA_GROUP * blk, 1), F32)
        for h in range(GQA_GROUP):
            sink = jnp.where(head == h, sink_ref[q0 + h] * LOG2E, sink)
        e, inv = _softmax_parts(s, sink)
        o = jnp.dot(e.astype(BF16), v, preferred_element_type=F32) * inv
        for h in range(GQA_GROUP):
            o_ref[:, (q0 + h) * HEAD_DIM:(q0 + h + 1) * HEAD_DIM] = o[h * blk:(h + 1) * blk, :].astype(o_ref.dtype)


def _swa_mask_table(blk, ctx_len):
    shape = (4, GQA_GROUP * blk, 3 * blk + ctx_len)
    cls = lax.broadcasted_iota(jnp.int32, shape, 0)
    qi = lax.broadcasted_iota(jnp.int32, shape, 1) % blk
    col = lax.broadcasted_iota(jnp.int32, shape, 2)
    lo = jnp.where(cls == 0, blk, 0)
    hi = jnp.where(cls == 3, 0, jnp.where(cls == 2, 2 * blk, 3 * blk))
    rel = col - blk - qi
    ok = ((col >= lo) & (col < hi) & (rel >= -WINDOW) & (rel <= WINDOW)) | (col >= 3 * blk)
    return jnp.where(ok, 0.0, NEG).astype(F32)


def _swa(qkv, sink, *, n_batch, seq_len, ctx_len, n_kv_heads, blk):
    ta = qkv.shape[0]
    n_heads = n_kv_heads * GQA_GROUP
    n_lat_blocks = seq_len // blk
    n_ctx_blocks = ctx_len // blk
    assert blk == WINDOW and seq_len % blk == 0 and ctx_len % blk == 0 and n_lat_blocks >= 2
    ctx_q0 = n_batch * n_lat_blocks
    ctx_kv0 = n_batch * seq_len // ctx_len
    kv_w = n_kv_heads * HEAD_DIM
    k_col, v_col = n_heads * HEAD_DIM // kv_w, n_heads * HEAD_DIM // kv_w + 1

    def q_index(b, n):
        return (jnp.where(n < n_lat_blocks, b * n_lat_blocks + n, ctx_q0 + b * n_ctx_blocks + n - n_lat_blocks), 0)

    def kv_spec(off, col):
        return pl.BlockSpec((blk, kv_w),
                            lambda b, n: (b * n_lat_blocks + jnp.clip(n + off, 0, n_lat_blocks - 1), col))

    def ctx_spec(col):
        return pl.BlockSpec((ctx_len, kv_w), lambda b, n: (ctx_kv0 + b, col))

    def mask_index(b, n):
        interior = jnp.where(n == 0, 0, jnp.where(n == n_lat_blocks - 1, 2, 1))
        return (jnp.where(n < n_lat_blocks, interior, 3), 0, 0)

    mask = _swa_mask_table(blk, ctx_len)
    kern = functools.partial(_swa_kernel, n_kv_heads=n_kv_heads)
    return pl.pallas_call(
        kern,
        out_shape=jax.ShapeDtypeStruct((ta, n_heads * HEAD_DIM), BF16),
        grid=(n_batch, n_lat_blocks + n_ctx_blocks),
        in_specs=[pl.BlockSpec(memory_space=pltpu.SMEM),
                  pl.BlockSpec((blk, n_heads * HEAD_DIM), q_index),
                  kv_spec(-1, k_col), kv_spec(0, k_col), kv_spec(1, k_col), ctx_spec(k_col),
                  kv_spec(-1, v_col), kv_spec(0, v_col), kv_spec(1, v_col), ctx_spec(v_col),
                  pl.BlockSpec((None,) + mask.shape[1:], mask_index)],
        out_specs=pl.BlockSpec((blk, n_heads * HEAD_DIM), q_index),
        compiler_params=_cparams(2),
        name="windowed_gqa",
    )(sink, *([qkv] * 9), mask)


NA_HEADS_PER_STEP = 8


def _na_kernel(q_ref, kvp_ref, kvc_ref, kvn_ref, kvx_ref, bias_ref, o_ref):
    n_local = bias_ref.shape[-1]
    gw = NA_HEADS_PER_STEP * HEAD_DIM
    kv_refs = (kvp_ref, kvc_ref, kvn_ref, kvx_ref)
    heads = [slice(h * HEAD_DIM, (h + 1) * HEAD_DIM) for h in range(NA_HEADS_PER_STEP)]
    def qk(h):
        k = jnp.concatenate([ref[:, heads[h]] for ref in kv_refs], axis=0)
        return lax.dot_general(q_ref[:, heads[h]], k, (((1,), (1,)), ((), ())), preferred_element_type=F32)

    def softmax(h, logit):
        s = logit * LOGIT_SCALE
        s = jnp.concatenate([s[:, :n_local] + bias_ref[h], s[:, n_local:]], axis=1)
        e, inv = _softmax_parts(s)
        return e.astype(BF16), inv

    def pv(h, e, inv):
        vs = slice(gw + heads[h].start, gw + heads[h].stop)
        v = jnp.concatenate([ref[:, vs] for ref in kv_refs], axis=0)
        o_ref[:, heads[h]] = (jnp.dot(e, v, preferred_element_type=F32) * inv).astype(o_ref.dtype)

    n = NA_HEADS_PER_STEP
    logits = {0: qk(0)}
    weights = {}
    for h in range(n):
        if h + 1 < n:
            logits[h + 1] = qk(h + 1)
        weights[h] = softmax(h, logits.pop(h))
        if h >= 1:
            pv(h - 1, *weights.pop(h - 1))
    pv(n - 1, *weights.pop(n - 1))


def _na_bias_table(rpb, n_grid_rows):
    r_in, w = NA_ROWS_PER_BLOCK, GRID_W
    kh = min(NA_KH, n_grid_rows)
    n_blocks = n_grid_rows // r_in
    n_ro, n_co = 2 * NA_KH - 1, 2 * NA_KW - 1
    qc, kc = np.arange(w)[:, None], np.arange(w)[None, :]
    e_col = (kc - qc + NA_KW - 1 == np.arange(n_co)[:, None, None]).astype(np.float32)
    win_start = np.clip(qc - NA_KW // 2, 0, w - NA_KW)
    col_ok = (kc >= win_start) & (kc < win_start + NA_KW)
    ri, krj = np.arange(r_in)[:, None], np.arange(3 * r_in)[None, :]
    row_ok = []
    for jb in (0, max(n_blocks // 2, 1) if n_blocks > 2 else 0, n_blocks - 1):
        r, kr = r_in * jb + ri, r_in * (jb - 1) + krj
        rs = np.clip(r - kh // 2, 0, n_grid_rows - kh)
        row_ok.append((kr >= rs) & (kr < rs + kh))
    ok = np.stack(row_ok)[:, :, None, :, None] & col_ok[None, None, :, None, :]
    ok = ok.reshape(3, r_in * w, 3 * r_in * w)
    tz = jnp.einsum('hab,bqc->haqc', rpb.astype(F32), e_col, precision=lax.Precision.HIGHEST) * LOG2E
    assert r_in - 1 <= NA_KH - 1 - r_in and 3 * r_in - 1 - r_in + NA_KH - 1 < n_ro
    full = jnp.concatenate(
        [jnp.concatenate([tz[:, k - i - r_in + NA_KH - 1] for k in range(3 * r_in)], axis=-1) for i in range(r_in)],
        axis=-2)
    return jnp.where(ok[None], full[:, None], NEG)


def _na(qkv, bias, *, n_batch, seq_len, ctx_len, n_heads):
    tq = NA_ROWS_PER_BLOCK * GRID_W
    n_blocks = seq_len // tq
    assert ctx_len == tq and seq_len % tq == 0 and NA_KH == 2 * NA_ROWS_PER_BLOCK and n_blocks >= 2
    assert n_heads % NA_HEADS_PER_STEP == 0
    n_groups = n_heads // NA_HEADS_PER_STEP
    gw = NA_HEADS_PER_STEP * HEAD_DIM
    ctx0 = n_batch * n_blocks

    def kv_spec(off):
        return pl.BlockSpec((tq, 2 * gw),
                            lambda b, g, jb: (b * n_blocks + jnp.clip(jb + off, 0, n_blocks - 1), g))

    def bias_index(b, g, jb):
        return (g, jnp.where(jb == 0, 0, jnp.where(jb == n_blocks - 1, 2, 1)), 0, 0)

    return pl.pallas_call(
        _na_kernel,
        out_shape=jax.ShapeDtypeStruct((n_batch * seq_len, n_heads * HEAD_DIM), BF16),
        grid=(n_batch, n_groups, n_blocks),
        in_specs=[pl.BlockSpec((tq, gw), lambda b, g, jb: (b * n_blocks + jb, 2 * n_groups + g)),
                  kv_spec(-1), kv_spec(0), kv_spec(1),
                  pl.BlockSpec((ctx_len, 2 * gw), lambda b, g, jb: (ctx0 + b, g)),
                  pl.BlockSpec((NA_HEADS_PER_STEP, None) + bias.shape[2:], bias_index)],
        out_specs=pl.BlockSpec((tq, gw), lambda b, g, jb: (b * n_blocks + jb, g)),
        compiler_params=_cparams(3),
        name="neighbourhood_attn",
    )(*([qkv] * 5), bias)


def _tiles():
    return dict(tm=1024, tm_proj=512, tm_out=512, tf=512, tn_mod=1024)


def _chunked(total, out_idx, out_col0=0, w_col0=0):
    return [(w_col0 + c, min(PROJ_CHUNK, total - c), out_idx, out_col0 + c, None)
            for c in range(0, total, PROJ_CHUNK)]


def _head_chunks(head_ops, w_col0, out_idx, out_col0):
    per = PROJ_CHUNK // HEAD_DIM
    return [(w_col0 + h0 * HEAD_DIM, len(head_ops[h0:h0 + per]) * HEAD_DIM, out_idx, out_col0 + h0 * HEAD_DIM,
             tuple(head_ops[h0:h0 + per])) for h0 in range(0, len(head_ops), per)]


def kernel(x, c, ctx, c_ctx, w_mod, b_mod, norm_g, ffn_w_in, ffn_w_out, ab_w_in, lru_conv_w, lru_conv_b,
           lru_w_a, lru_b_a, lru_w_x, lru_b_x, lru_lambda, attn_q_norm, attn_k_norm, attn_sink, ab_w_out,
           na_w_in, na_q_norm, na_k_norm, na_rpb, na_w_out):
    n_batch, seq_len, d = x.shape
    ctx_len = ctx.shape[1]
    depth = w_mod.shape[0]
    lru_w = lru_conv_w.shape[2]
    n_kv = attn_sink.shape[1] // GQA_GROUP
    n_att = n_kv * GQA_GROUP
    na_heads = na_rpb.shape[1]
    assert n_batch < MOD_ROWS and depth == 2
    t = _tiles()
    tl, tc = n_batch * seq_len, n_batch * ctx_len
    geom = dict(rows_per_batch=seq_len, n_batch=n_batch)

    cc = jnp.zeros((MOD_ROWS, d), F32).at[:n_batch].set(c).at[n_batch].set(c_ctx)
    mod = _mod_table(cc, w_mod, b_mod, t["tn_mod"])
    wg, wu, wo = _cast_ffn_weights(ffn_w_in, ffn_w_out)

    def ffn(x_src, n_rows, layer, which, row0=0):
        return _ffn(x_src, n_rows, row0, mod, layer, which, norm_g[layer, 2 * which], wg, wu, wo,
                    tm=t["tm"], tf=t["tf"], **geom)

    cos_t, sin_t = _rope_tables(seq_len)
    proj_geom = dict(tm=t["tm_proj"], n_lat_rows=tl, **geom)

    x_lat = ffn(x.reshape(tl, d), tl, 0, 0)
    x_ctx = ffn(ctx.reshape(tc, d), tc, 0, 0, row0=tl)
    gains = jnp.stack([attn_q_norm[0], attn_k_norm[0]])
    head_ops = [(0, True)] * n_att + [(1, True)] * n_kv + [None] * n_kv
    plan = _head_chunks(head_ops, 2 * lru_w, 1, 0) + _chunked(2 * lru_w, 0)
    p, qkv = _inproj(x_lat, x_ctx, mod, 0, norm_g[0, 1], ab_w_in[0].astype(BF16), gains, cos_t, sin_t, plan,
                     ((2 * lru_w, F32), (len(head_ops) * HEAD_DIM, BF16)), **proj_geom)
    lru_args = lambda dr: (lru_conv_w[0], lru_conv_b[0], lru_w_a[0, dr].astype(BF16), lru_b_a[0, dr],
                           lru_w_x[0, dr].astype(BF16), lru_b_x[0, dr], lru_lambda[0, dr])
    scan_geom = dict(n_batch=n_batch, seq_len=seq_len, ctx_len=ctx_len, tt=ctx_len)
    h_fwd = _lru_pass(p, *lru_args(0), None, reverse=False, **scan_geom)
    lru = _lru_pass(p, *lru_args(1), h_fwd, reverse=True, **scan_geom)
    att = _swa(qkv, attn_sink[0], n_batch=n_batch, seq_len=seq_len, ctx_len=ctx_len, n_kv_heads=n_kv,
               blk=WINDOW)
    x_all = _outproj(lru, att, 0, ab_w_out[0].astype(BF16), x_lat, x_ctx, tl + tc, mod, 0, tm=t["tm_out"], **geom)
    x_all = ffn(x_all, tl + tc, 0, 1)

    x_all = ffn(x_all, tl + tc, 1, 0)
    gains = jnp.stack([na_q_norm[0], na_k_norm[0]])
    na_d = na_heads * HEAD_DIM
    gw = NA_HEADS_PER_STEP * HEAD_DIM
    groups = range(0, na_d, gw)
    plan = sum([_head_chunks([(0, False)] * NA_HEADS_PER_STEP, g0, 0, 2 * na_d + g0) for g0 in groups], [])
    plan += sum([_head_chunks([(1, False)] * NA_HEADS_PER_STEP, na_d + g0, 0, 2 * g0) for g0 in groups], [])
    plan += sum([_head_chunks([None] * NA_HEADS_PER_STEP, 2 * na_d + g0, 0, 2 * g0 + gw) for g0 in groups], [])
    (qkv,) = _inproj(x_all, None, mod, 1, norm_g[1, 1], na_w_in[0].astype(BF16), gains, cos_t, sin_t, plan,
                     ((3 * na_d, BF16),), **proj_geom)
    o = _na(qkv, _na_bias_table(na_rpb[0], seq_len // GRID_W), n_batch=n_batch, seq_len=seq_len, ctx_len=ctx_len,
            n_heads=na_heads)
    x_lat = _outproj(o, o, 1, na_w_out[0].astype(BF16), x_all, None, tl, mod, 1, tm=t["tm_out"], **geom)
    x_lat = ffn(x_lat, tl, 1, 1)
    return x_lat.reshape(n_batch, seq_len, d)
```

```python
import functools

import numpy as np
import jax
import jax.numpy as jnp
from jax import lax
from jax.experimental import pallas as pl
from jax.experimental.pallas import tpu as pltpu

HEAD_DIM = 128
EPS = 1e-6
N_MOD = 9
GRID_W = 64
CONV_W = 4
LRU_C = 8.0
WINDOW = 128
ROPE_THETA = 10000.0
NA_KH = 8
NA_KW = 16
ATTN_SCALE = HEAD_DIM ** -0.5
GQA_GROUP = 4

LANES = 128
SUBLANES = 8
VMEM_LIMIT_BYTES = 56 * 1024 * 1024
MOD_ROWS = SUBLANES

NEG = -1e30
NA_ROWS_PER_BLOCK = 4

BF16 = jnp.bfloat16
F32 = jnp.float32


def _cparams(n_axes):
    return pltpu.CompilerParams(dimension_semantics=("arbitrary",) * n_axes,
                                vmem_limit_bytes=VMEM_LIMIT_BYTES)


def _sigmoid(x):
    return 1.0 / (1.0 + jnp.exp(-x))


def _sigmoid_tanh(x):
    return 0.5 * jnp.tanh(0.5 * x) + 0.5


NORM_CHUNK_ROWS = 16


def _norm_modulate_into(h_scr, x_ref, g, shift, scale, alt=None):
    gain = g * (1.0 + scale)

    def chunk(c, carry):
        r0 = pl.multiple_of(c * NORM_CHUNK_ROWS, NORM_CHUNK_ROWS)
        x = x_ref[pl.ds(r0, NORM_CHUNK_ROWS), :]
        if alt is not None:
            x = jnp.where(alt[0], alt[1][pl.ds(r0, NORM_CHUNK_ROWS), :], x)
        inv = lax.rsqrt(jnp.mean(x * x, axis=-1, keepdims=True) + EPS)
        h_scr[pl.ds(r0, NORM_CHUNK_ROWS), :] = ((x * inv) * gain + shift).astype(h_scr.dtype)
        return carry

    lax.fori_loop(0, x_ref.shape[0] // NORM_CHUNK_ROWS, chunk, 0, unroll=4)


def _mod_kernel(c_ref, w_ref, b_ref, o_ref):
    c = c_ref[...]
    s = c * _sigmoid(c)
    o_ref[...] = jnp.dot(s.astype(BF16), w_ref[...].astype(BF16),
                         preferred_element_type=F32) + b_ref[...]


def _mod_table(cc, w_mod, b_mod, tn):
    depth, d, n = w_mod.shape
    return pl.pallas_call(
        _mod_kernel,
        out_shape=jax.ShapeDtypeStruct((depth, MOD_ROWS, n), F32),
        grid=(depth, n // tn),
        in_specs=[pl.BlockSpec((MOD_ROWS, d), lambda l, j: (0, 0)),
                  pl.BlockSpec((None, d, tn), lambda l, j: (l, 0, j)),
                  pl.BlockSpec((None, 1, tn), lambda l, j: (l, 0, j))],
        out_specs=pl.BlockSpec((None, MOD_ROWS, tn), lambda l, j: (l, 0, j)),
        compiler_params=_cparams(2),
        name="mod_table",
    )(cc, w_mod, b_mod.reshape(depth, 1, n))


def _mod_spec(d, layer, k, n_grid_axes):
    if n_grid_axes == 1:
        return pl.BlockSpec((None, MOD_ROWS, d), lambda i: (layer, 0, k))
    return pl.BlockSpec((None, MOD_ROWS, d), lambda i, j: (layer, 0, k))


def _mod_row(tile_idx, tm, rows_per_batch, n_batch):
    return jnp.minimum(tile_idx * tm // rows_per_batch, n_batch)


def _ffn_kernel(x_ref, g_ref, sh_ref, sc_ref, gt_ref, wg_ref, wu_ref, wo_ref, o_ref, h_scr,
                *, tm, tile0, rows_per_batch, n_batch, last_width):
    i, j = pl.program_id(0), pl.program_id(1)
    last = pl.num_programs(1) - 1
    r = _mod_row(i + tile0, tm, rows_per_batch, n_batch)

    def hidden_block(width, first):
        h = h_scr[...]
        a = jnp.dot(h, wg_ref[:, :width], preferred_element_type=F32)
        u = jnp.dot(h, wu_ref[:, :width], preferred_element_type=F32)
        act = (a * _sigmoid(a)) * u
        y = jnp.dot(act.astype(BF16), wo_ref[:width, :], preferred_element_type=F32)
        o_ref[...] = y if first else o_ref[...] + y

    @pl.when(j == 0)
    def _():
        _norm_modulate_into(h_scr, x_ref, g_ref[...], sh_ref[pl.ds(r, 1), :], sc_ref[pl.ds(r, 1), :])
        hidden_block(wg_ref.shape[1], True)

    pl.when(jnp.logical_and(j > 0, j < last))(lambda: hidden_block(wg_ref.shape[1], False))

    @pl.when(j == last)
    def _():
        hidden_block(last_width, False)
        o_ref[...] = x_ref[...] + 0.5 * gt_ref[pl.ds(r, 1), :] * o_ref[...]


def _ffn(x_src, n_rows, row0, mod, layer, which, g, wg, wu, wo, *, tm, tf, rows_per_batch, n_batch):
    d = x_src.shape[1]
    d_ff = wo.shape[2]
    nf = pl.cdiv(d_ff, tf)
    last_width = d_ff - (nf - 1) * tf
    assert last_width % LANES == 0 and row0 % tm == 0 and nf >= 2
    k0 = 6 * which
    kern = functools.partial(_ffn_kernel, tm=tm, tile0=row0 // tm, rows_per_batch=rows_per_batch,
                             n_batch=n_batch, last_width=last_width)
    return pl.pallas_call(
        kern,
        out_shape=jax.ShapeDtypeStruct((n_rows, d), F32),
        grid=(n_rows // tm, nf),
        in_specs=[pl.BlockSpec((tm, d), lambda i, j: (i, 0)),
                  pl.BlockSpec((1, d), lambda i, j: (0, 0)),
                  _mod_spec(d, layer, k0, 2), _mod_spec(d, layer, k0 + 1, 2), _mod_spec(d, layer, k0 + 2, 2),
                  pl.BlockSpec((None, None, d, tf), lambda i, j: (layer, which, 0, j)),
                  pl.BlockSpec((None, None, d, tf), lambda i, j: (layer, which, 0, j)),
                  pl.BlockSpec((None, None, tf, d), lambda i, j: (layer, which, j, 0))],
        out_specs=pl.BlockSpec((tm, d), lambda i, j: (i, 0)),
        scratch_shapes=[pltpu.VMEM((tm, d), BF16)],
        compiler_params=_cparams(2),
        name="half_ffn",
    )(x_src, g.reshape(1, d), mod, mod, mod, wg, wu, wo)


def _cast_kernel(*refs):
    n = len(refs) // 2
    for src, dst in zip(refs[:n], refs[n:]):
        dst[...] = src[...].astype(dst.dtype)


def _largest_tile(n, multiple, cap):
    return max(t for t in range(multiple, min(n, cap) + 1, multiple) if n % t == 0)


def _cast_ffn_weights(w_in, w_out):
    depth, n_ffn, d, f2 = w_in.shape
    f = f2 // 2
    assert f % LANES == 0
    rows = _largest_tile(d, 2 * SUBLANES, 256)
    half = (None, None, rows, f)
    wg, wu = pl.pallas_call(
        _cast_kernel,
        out_shape=[jax.ShapeDtypeStruct((depth, n_ffn, d, f), BF16)] * 2,
        grid=(depth, n_ffn, d // rows),
        in_specs=[pl.BlockSpec(half, lambda l, w, r: (l, w, r, 0)), pl.BlockSpec(half, lambda l, w, r: (l, w, r, 1))],
        out_specs=[pl.BlockSpec(half, lambda l, w, r: (l, w, r, 0))] * 2,
        compiler_params=_cparams(3),
        name="cast_ffn_in",
    )(w_in, w_in)
    rows = _largest_tile(f, 2 * SUBLANES, 1024)
    blk = (None, None, rows, d)
    wo = pl.pallas_call(
        _cast_kernel,
        out_shape=jax.ShapeDtypeStruct(w_out.shape, BF16),
        grid=(depth, n_ffn, f // rows),
        in_specs=[pl.BlockSpec(blk, lambda l, w, r: (l, w, r, 0))],
        out_specs=pl.BlockSpec(blk, lambda l, w, r: (l, w, r, 0)),
        compiler_params=_cparams(3),
        name="cast_ffn_out",
    )(w_out)
    return wg, wu, wo


PROJ_CHUNK = 4 * HEAD_DIM


def _swap_halves_32(y):
    lane = lax.broadcasted_iota(jnp.int32, y.shape, 1)
    return jnp.where((lane & 32) == 0, pltpu.roll(y, LANES - 32, 1), pltpu.roll(y, 32, 1))


def _inproj_kernel(x_ref, g_ref, sh_ref, sc_ref, w_ref, gains_ref, cos_ref, sin_ref, *rest,
                   plan, tm, rows_per_batch, n_batch, n_lat_tiles, n_tiles1):
    i = pl.program_id(0)
    alt = (i >= n_tiles1, rest[0]) if n_tiles1 is not None else None
    out_refs, h_scr = rest[(0 if alt is None else 1):-1], rest[-1]
    r = _mod_row(i, tm, rows_per_batch, n_batch)
    is_lat = i < n_lat_tiles
    _norm_modulate_into(h_scr, x_ref, g_ref[...], sh_ref[pl.ds(r, 1), :], sc_ref[pl.ds(r, 1), :], alt)
    for w_col, width, out_idx, out_col, ops in plan:
        y = jnp.dot(h_scr[...], w_ref[:, w_col:w_col + width], preferred_element_type=F32)
        o_ref = out_refs[out_idx]
        if ops is None:
            o_ref[:, out_col:out_col + width] = y.astype(o_ref.dtype)
            continue
        for hh, op in enumerate(ops):
            yh = y[:, hh * HEAD_DIM:(hh + 1) * HEAD_DIM]
            if op is not None:
                gain_row, rope = op
                yh = yh * lax.rsqrt(jnp.mean(yh * yh, axis=-1, keepdims=True) + EPS)
                yh = yh * gains_ref[gain_row:gain_row + 1, :]
                if rope:
                    yh = jnp.where(is_lat, yh * cos_ref[...] + _swap_halves_32(yh) * sin_ref[...], yh)
            o_ref[:, out_col + hh * HEAD_DIM:out_col + (hh + 1) * HEAD_DIM] = yh.astype(o_ref.dtype)


def _stream_specs(x_src, x_src2, tm):
    d = x_src.shape[1]
    if x_src2 is None:
        return [pl.BlockSpec((tm, d), lambda i: (i, 0))], None
    n1 = x_src.shape[0] // tm
    return [pl.BlockSpec((tm, d), lambda i: (jnp.minimum(i, n1 - 1), 0)),
            pl.BlockSpec((tm, d), lambda i: (jnp.maximum(i - n1, 0), 0))], n1


def _inproj(x_src, x_src2, mod, layer, g, w, gains, cos_t, sin_t, plan, outs, *, tm, n_lat_rows, rows_per_batch,
            n_batch):
    d = x_src.shape[1]
    ta = x_src.shape[0] + (0 if x_src2 is None else x_src2.shape[0])
    n_pos_tiles = cos_t.shape[0] // tm
    (x_spec, *x2_spec), n_tiles1 = _stream_specs(x_src, x_src2, tm)
    kern = functools.partial(_inproj_kernel, plan=tuple(plan), tm=tm, rows_per_batch=rows_per_batch,
                             n_batch=n_batch, n_lat_tiles=n_lat_rows // tm, n_tiles1=n_tiles1)
    return pl.pallas_call(
        kern,
        out_shape=[jax.ShapeDtypeStruct((ta, n), dt) for n, dt in outs],
        grid=(ta // tm,),
        in_specs=[x_spec,
                  pl.BlockSpec((1, d), lambda i: (0, 0)),
                  _mod_spec(d, layer, 3, 1), _mod_spec(d, layer, 4, 1),
                  pl.BlockSpec(w.shape, lambda i: (0, 0), pipeline_mode=pl.Buffered(1)),
                  pl.BlockSpec(gains.shape, lambda i: (0, 0)),
                  pl.BlockSpec((tm, HEAD_DIM), lambda i: (i % n_pos_tiles, 0)),
                  pl.BlockSpec((tm, HEAD_DIM), lambda i: (i % n_pos_tiles, 0))] + x2_spec,
        out_specs=[pl.BlockSpec((tm, n), lambda i: (i, 0)) for n, _ in outs],
        scratch_shapes=[pltpu.VMEM((tm, d), BF16)],
        compiler_params=_cparams(1),
        name="mixer_inproj",
    )(x_src, g.reshape(1, d), mod, mod, w, gains, cos_t, sin_t, *([] if x_src2 is None else [x_src2]))


def _outproj_kernel(a1_ref, a2_ref, w1_ref, w2_ref, gt_ref, x_ref, *rest, tm, rows_per_batch, n_batch, n_tiles1):
    o_ref = rest[-1]
    i = pl.program_id(0)
    r = _mod_row(i, tm, rows_per_batch, n_batch)
    d = o_ref.shape[1]
    for c0 in range(0, d, PROJ_CHUNK):
        cs = slice(c0, min(c0 + PROJ_CHUNK, d))
        y = (jnp.dot(a1_ref[...], w1_ref[:, cs], preferred_element_type=F32)
             + jnp.dot(a2_ref[...], w2_ref[:, cs], preferred_element_type=F32))
        x = x_ref[:, cs] if n_tiles1 is None else jnp.where(i >= n_tiles1, rest[0][:, cs], x_ref[:, cs])
        o_ref[:, cs] = x + gt_ref[pl.ds(r, 1), cs] * y


def _outproj(a1, a2, a2_col_block, w, x_src, x_src2, n_rows, mod, layer, *, tm, rows_per_batch, n_batch):
    d = x_src.shape[1]
    kh = w.shape[0] // 2
    x_specs, n_tiles1 = _stream_specs(x_src, x_src2, tm)
    kern = functools.partial(_outproj_kernel, tm=tm, rows_per_batch=rows_per_batch, n_batch=n_batch,
                             n_tiles1=n_tiles1)
    return pl.pallas_call(
        kern,
        out_shape=jax.ShapeDtypeStruct((n_rows, d), F32),
        grid=(n_rows // tm,),
        in_specs=[pl.BlockSpec((tm, kh), lambda i: (i, 0)),
                  pl.BlockSpec((tm, kh), lambda i: (i, a2_col_block)),
                  pl.BlockSpec((kh, d), lambda i: (0, 0)),
                  pl.BlockSpec((kh, d), lambda i: (1, 0)),
                  _mod_spec(d, layer, 5, 1)] + x_specs,
        out_specs=pl.BlockSpec((tm, d), lambda i: (i, 0)),
        compiler_params=_cparams(1),
        name="mixer_outproj",
    )(a1, a2, w, w, mod, x_src, *([] if x_src2 is None else [x_src2]))


def _rope_tables(seq_len):
    half = HEAD_DIM // 2
    nf = half // 2
    inv = ROPE_THETA ** (-jnp.arange(nf, dtype=F32) / nf)
    pos = jnp.arange(seq_len, dtype=jnp.int32)
    ang_r = (pos // GRID_W).astype(F32)[:, None] * inv[None, :]
    ang_c = (pos % GRID_W).astype(F32)[:, None] * inv[None, :]
    cr, sr, cc, sc = jnp.cos(ang_r), jnp.sin(ang_r), jnp.cos(ang_c), jnp.sin(ang_c)
    return (jnp.concatenate([cr, cr, cc, cc], axis=-1),
            jnp.concatenate([-sr, sr, -sc, sc], axis=-1))


def _gelu_tanh(x):
    return x * (0.5 * (1.0 + jnp.tanh(np.float32(np.sqrt(2.0 / np.pi)) * (x + 0.044715 * (x * x * x)))))


def _lru_kernel(*refs, tt, reverse, final, n_lat_chunks):
    if final:
        (u_ref, wa_ref, ba_ref, wx_ref, bx_ref, lam_ref, hf_ref, gl_ref, o_ref, a_scr, b_scr, carry_scr) = refs
    else:
        (xp_ref, xc_ref, xn_ref, cw_ref, cb_ref, wa_ref, ba_ref, wx_ref, bx_ref, lam_ref,
         o_ref, u_ref, xs_scr, a_scr, b_scr, carry_scr) = refs
    s = pl.program_id(1)
    is_ctx = s == 0

    @pl.when(is_ctx)
    def _():
        carry_scr[...] = jnp.zeros_like(carry_scr)

    width = a_scr.shape[1]
    halo = SUBLANES
    left = CONV_W // 2
    if not final:
        j = (n_lat_chunks - s) if reverse else (s - 1)
        has_prev = jnp.logical_and(jnp.logical_not(is_ctx), j > 0)
        has_next = jnp.logical_and(jnp.logical_not(is_ctx), j < n_lat_chunks - 1)
        xs_scr[0:halo, :] = jnp.where(has_prev, xp_ref[tt - halo:tt, :], 0.0)
        xs_scr[halo:halo + tt, :] = xc_ref[...]
        xs_scr[halo + tt:2 * halo + tt, :] = jnp.where(has_next, xn_ref[0:halo, :], 0.0)

    z = -lam_ref[...]
    softplus = jnp.maximum(z, 0.0) + jnp.log1p(jnp.exp(-jnp.abs(z)))
    for n in range(width // HEAD_DIM):
        cs = slice(n * HEAD_DIM, (n + 1) * HEAD_DIM)
        if final:
            u = u_ref[:, cs]
        else:
            u = cb_ref[:, cs]
            xa = xs_scr[:, cs]
            n_rows = xa.shape[0]
            for k in range(CONV_W):
                xk = xa if k == left else pltpu.roll(xa, (left - k) % n_rows, 0)
                u = u + xk[halo:halo + tt, :] * cw_ref[k:k + 1, cs]
            u_ref[:, cs] = u
        ub = u.astype(BF16)
        gate_r = jnp.dot(ub, wa_ref[n], preferred_element_type=F32) + ba_ref[:, cs]
        gate_i = jnp.dot(ub, wx_ref[n], preferred_element_type=F32) + bx_ref[:, cs]
        log_a = (-LRU_C) * _sigmoid_tanh(gate_r) * softplus[:, cs]
        a = jnp.exp(log_a)
        a_scr[:, cs] = a
        b_scr[:, cs] = jnp.sqrt(-jnp.tanh(log_a) * (1.0 + a * a)) * (_sigmoid_tanh(gate_i) * u)

    n_groups = tt // SUBLANES
    row = lax.broadcasted_iota(jnp.int32, (SUBLANES, width), 0)

    def group(gi, carry):
        g = (n_groups - 1 - gi) if reverse else gi
        r0 = pl.multiple_of(g * SUBLANES, SUBLANES)
        a = a_scr[pl.ds(r0, SUBLANES), :]
        b = b_scr[pl.ds(r0, SUBLANES), :]
        for k in (1, 2, 4):
            if reverse:
                keep = row < SUBLANES - k
                shift = SUBLANES - k
            else:
                keep = row >= k
                shift = k
            a_sh = jnp.where(keep, pltpu.roll(a, shift, 0), 1.0)
            b_sh = jnp.where(keep, pltpu.roll(b, shift, 0), 0.0)
            b = a * b_sh + b
            a = a * a_sh
        h = b + a * carry
        b_scr[pl.ds(r0, SUBLANES), :] = h
        last = h[0:1, :] if reverse else h[SUBLANES - 1:SUBLANES, :]
        return jnp.broadcast_to(last, (SUBLANES, width))

    carry_scr[...] = lax.fori_loop(0, n_groups, group, carry_scr[...])

    if final:
        o_ref[...] = ((hf_ref[...] + b_scr[...]) * _gelu_tanh(gl_ref[...])).astype(o_ref.dtype)
    else:
        o_ref[...] = b_scr[...]


def _lru_pass(p, conv_w, conv_b, w_a, b_a, w_x, b_x, lam, first, *, reverse, n_batch, seq_len, ctx_len, tt):
    ta = p.shape[0]
    width = conv_w.shape[1]
    assert ctx_len == tt and seq_len % tt == 0
    n_lat_chunks = seq_len // tt
    ctx_block0 = n_batch * n_lat_chunks
    final = first is not None

    def chunk_index(b, s, off):
        j = (n_lat_chunks - s) if reverse else (s - 1)
        j = jnp.clip(j + off, 0, n_lat_chunks - 1)
        return jnp.where(s == 0, ctx_block0 + b, b * n_lat_chunks + j)

    def xspec(off):
        return pl.BlockSpec((tt, width), lambda b, s: (chunk_index(b, s, off), 0))

    row_spec = pl.BlockSpec((1, width), lambda b, s: (0, 0))
    w_spec = pl.BlockSpec(w_a.shape, lambda b, s: (0, 0, 0))
    gate_specs = [w_spec, row_spec, w_spec, row_spec, row_spec]
    gate_args = [w_a, b_a.reshape(1, width), w_x, b_x.reshape(1, width), lam.reshape(1, width)]
    scan_scratch = [pltpu.VMEM((tt, width), F32), pltpu.VMEM((tt, width), F32), pltpu.VMEM((SUBLANES, width), F32)]
    if final:
        h_first, u = first
        in_specs = [xspec(0)] + gate_specs + [xspec(0),
                                              pl.BlockSpec((tt, width), lambda b, s: (chunk_index(b, s, 0), 1))]
        args = [u] + gate_args + [h_first, p]
        out_shape, out_specs, scratch = jax.ShapeDtypeStruct((ta, width), BF16), xspec(0), scan_scratch
    else:
        in_specs = [xspec(-1), xspec(0), xspec(1),
                    pl.BlockSpec((CONV_W, width), lambda b, s: (0, 0)), row_spec] + gate_specs
        args = [p, p, p, conv_w, conv_b.reshape(1, width)] + gate_args
        out_shape = [jax.ShapeDtypeStruct((ta, width), F32)] * 2
        out_specs = [xspec(0), xspec(0)]
        scratch = [pltpu.VMEM((tt + 2 * SUBLANES, width), F32)] + scan_scratch
    kern = functools.partial(_lru_kernel, tt=tt, reverse=reverse, final=final, n_lat_chunks=n_lat_chunks)
    return pl.pallas_call(
        kern,
        out_shape=out_shape,
        grid=(n_batch, n_lat_chunks + 1),
        in_specs=in_specs,
        out_specs=out_specs,
        scratch_shapes=scratch,
        compiler_params=_cparams(2),
        name="rglru_rev" if reverse else "rglru_fwd",
    )(*args)


LOG2E = float(np.log2(np.e))
LOGIT_SCALE = ATTN_SCALE * LOG2E


def _softmax_parts(s2, extra_logit2=None):
    m = jnp.max(s2, axis=-1, keepdims=True)
    if extra_logit2 is not None:
        m = jnp.maximum(m, extra_logit2)
    e = jnp.exp2(s2 - m)
    denom = jnp.sum(e, axis=-1, keepdims=True)
    if extra_logit2 is not None:
        denom = denom + jnp.exp2(extra_logit2 - m)
    return e, 1.0 / denom


def _swa_kernel(sink_ref, q_ref, kp_ref, kc_ref, kn_ref, kx_ref, vp_ref, vc_ref, vn_ref, vx_ref, mask_ref, o_ref,
                *, n_kv_heads):
    blk = q_ref.shape[0]
    head = lax.broadcasted_iota(jnp.int32, (GQA_GROUP * blk, 1), 0) // blk
    logits = []
    for kh in range(n_kv_heads):
        ks = slice(kh * HEAD_DIM, (kh + 1) * HEAD_DIM)
        q0 = kh * GQA_GROUP
        q = jnp.concatenate([q_ref[:, (q0 + h) * HEAD_DIM:(q0 + h + 1) * HEAD_DIM] for h in range(GQA_GROUP)],
                            axis=0)
        k = jnp.concatenate([kp_ref[:, ks], kc_ref[:, ks], kn_ref[:, ks], kx_ref[:, ks]], axis=0)
        logits.append(lax.dot_general(q, k, (((1,), (1,)), ((), ())), preferred_element_type=F32))
    for kh in range(n_kv_heads):
        ks = slice(kh * HEAD_DIM, (kh + 1) * HEAD_DIM)
        q0 = kh * GQA_GROUP
        v = jnp.concatenate([vp_ref[:, ks], vc_ref[:, ks], vn_ref[:, ks], vx_ref[:, ks]], axis=0)
        s = logits[kh] * LOGIT_SCALE + mask_ref[...]
        sink = jnp.zeros((GQA_GROUP * blk, 1), F32)
        for h in range(GQA_GROUP):
            sink = jnp.where(head == h, sink_ref[q0 + h] * LOG2E, sink)
        e, inv = _softmax_parts(s, sink)
        o = jnp.dot(e.astype(BF16), v, preferred_element_type=F32) * inv
        for h in range(GQA_GROUP):
            o_ref[:, (q0 + h) * HEAD_DIM:(q0 + h + 1) * HEAD_DIM] = o[h * blk:(h + 1) * blk, :].astype(o_ref.dtype)


def _swa_mask_table(blk, ctx_len):
    shape = (4, GQA_GROUP * blk, 3 * blk + ctx_len)
    cls = lax.broadcasted_iota(jnp.int32, shape, 0)
    qi = lax.broadcasted_iota(jnp.int32, shape, 1) % blk
    col = lax.broadcasted_iota(jnp.int32, shape, 2)
    lo = jnp.where(cls == 0, blk, 0)
    hi = jnp.where(cls == 3, 0, jnp.where(cls == 2, 2 * blk, 3 * blk))
    rel = col - blk - qi
    ok = ((col >= lo) & (col < hi) & (rel >= -WINDOW) & (rel <= WINDOW)) | (col >= 3 * blk)
    return jnp.where(ok, 0.0, NEG).astype(F32)


def _swa(qkv, sink, *, n_batch, seq_len, ctx_len, n_kv_heads, blk):
    ta = qkv.shape[0]
    n_heads = n_kv_heads * GQA_GROUP
    n_lat_blocks = seq_len // blk
    n_ctx_blocks = ctx_len // blk
    assert blk == WINDOW and seq_len % blk == 0 and ctx_len % blk == 0 and n_lat_blocks >= 2
    ctx_q0 = n_batch * n_lat_blocks
    ctx_kv0 = n_batch * seq_len // ctx_len
    kv_w = n_kv_heads * HEAD_DIM
    k_col, v_col = n_heads * HEAD_DIM // kv_w, n_heads * HEAD_DIM // kv_w + 1

    def q_index(b, n):
        return (jnp.where(n < n_lat_blocks, b * n_lat_blocks + n, ctx_q0 + b * n_ctx_blocks + n - n_lat_blocks), 0)

    def kv_spec(off, col):
        return pl.BlockSpec((blk, kv_w),
                            lambda b, n: (b * n_lat_blocks + jnp.clip(n + off, 0, n_lat_blocks - 1), col))

    def ctx_spec(col):
        return pl.BlockSpec((ctx_len, kv_w), lambda b, n: (ctx_kv0 + b, col))

    def mask_index(b, n):
        interior = jnp.where(n == 0, 0, jnp.where(n == n_lat_blocks - 1, 2, 1))
        return (jnp.where(n < n_lat_blocks, interior, 3), 0, 0)

    mask = _swa_mask_table(blk, ctx_len)
    kern = functools.partial(_swa_kernel, n_kv_heads=n_kv_heads)
    return pl.pallas_call(
        kern,
        out_shape=jax.ShapeDtypeStruct((ta, n_heads * HEAD_DIM), BF16),
        grid=(n_batch, n_lat_blocks + n_ctx_blocks),
        in_specs=[pl.BlockSpec(memory_space=pltpu.SMEM),
                  pl.BlockSpec((blk, n_heads * HEAD_DIM), q_index),
                  kv_spec(-1, k_col), kv_spec(0, k_col), kv_spec(1, k_col), ctx_spec(k_col),
                  kv_spec(-1, v_col), kv_spec(0, v_col), kv_spec(1, v_col), ctx_spec(v_col),
                  pl.BlockSpec((None,) + mask.shape[1:], mask_index)],
        out_specs=pl.BlockSpec((blk, n_heads * HEAD_DIM), q_index),
        compiler_params=_cparams(2),
        name="windowed_gqa",
    )(sink, *([qkv] * 9), mask)


NA_MAX_HEADS_PER_STEP = 16


def _na_heads_per_step(n_heads):
    return min(NA_MAX_HEADS_PER_STEP, n_heads)


def _na_kernel(q_ref, kvp_ref, kvc_ref, kvn_ref, kvx_ref, bias_ref, o_ref):
    n_local = bias_ref.shape[-1]
    gw = q_ref.shape[1]
    n = gw // HEAD_DIM
    kv_refs = (kvp_ref, kvc_ref, kvn_ref, kvx_ref)
    heads = [slice(h * HEAD_DIM, (h + 1) * HEAD_DIM) for h in range(n)]
    def qk(h):
        k = jnp.concatenate([ref[:, heads[h]] for ref in kv_refs], axis=0)
        return lax.dot_general(q_ref[:, heads[h]], k, (((1,), (1,)), ((), ())), preferred_element_type=F32)

    def softmax(h, logit):
        s = logit * LOGIT_SCALE
        s = jnp.concatenate([s[:, :n_local] + bias_ref[h], s[:, n_local:]], axis=1)
        e, inv = _softmax_parts(s)
        return e.astype(BF16), inv

    def pv(h, e, inv):
        vs = slice(gw + heads[h].start, gw + heads[h].stop)
        v = jnp.concatenate([ref[:, vs] for ref in kv_refs], axis=0)
        o_ref[:, heads[h]] = (jnp.dot(e, v, preferred_element_type=F32) * inv).astype(o_ref.dtype)

    logits = {0: qk(0)}
    weights = {}
    for h in range(n):
        if h + 1 < n:
            logits[h + 1] = qk(h + 1)
        weights[h] = softmax(h, logits.pop(h))
        if h >= 1:
            pv(h - 1, *weights.pop(h - 1))
    pv(n - 1, *weights.pop(n - 1))


def _na_bias_table(rpb, n_grid_rows):
    r_in, w = NA_ROWS_PER_BLOCK, GRID_W
    kh = min(NA_KH, n_grid_rows)
    n_blocks = n_grid_rows // r_in
    n_ro, n_co = 2 * NA_KH - 1, 2 * NA_KW - 1
    qc, kc = np.arange(w)[:, None], np.arange(w)[None, :]
    e_col = (kc - qc + NA_KW - 1 == np.arange(n_co)[:, None, None]).astype(np.float32)
    win_start = np.clip(qc - NA_KW // 2, 0, w - NA_KW)
    col_ok = (kc >= win_start) & (kc < win_start + NA_KW)
    ri, krj = np.arange(r_in)[:, None], np.arange(3 * r_in)[None, :]
    row_ok = []
    for jb in (0, max(n_blocks // 2, 1) if n_blocks > 2 else 0, n_blocks - 1):
        r, kr = r_in * jb + ri, r_in * (jb - 1) + krj
        rs = np.clip(r - kh // 2, 0, n_grid_rows - kh)
        row_ok.append((kr >= rs) & (kr < rs + kh))
    ok = np.stack(row_ok)[:, :, None, :, None] & col_ok[None, None, :, None, :]
    ok = ok.reshape(3, r_in * w, 3 * r_in * w)
    tz = jnp.einsum('hab,bqc->haqc', rpb.astype(F32), e_col, precision=lax.Precision.HIGHEST) * LOG2E
    assert r_in - 1 <= NA_KH - 1 - r_in and 3 * r_in - 1 - r_in + NA_KH - 1 < n_ro
    full = jnp.concatenate(
        [jnp.concatenate([tz[:, k - i - r_in + NA_KH - 1] for k in range(3 * r_in)], axis=-1) for i in range(r_in)],
        axis=-2)
    return jnp.where(ok[None], full[:, None], NEG)


def _na(qkv, bias, *, n_batch, seq_len, ctx_len, n_heads):
    tq = NA_ROWS_PER_BLOCK * GRID_W
    n_blocks = seq_len // tq
    assert ctx_len == tq and seq_len % tq == 0 and NA_KH == 2 * NA_ROWS_PER_BLOCK and n_blocks >= 2
    hps = _na_heads_per_step(n_heads)
    assert n_heads % hps == 0
    n_groups = n_heads // hps
    gw = hps * HEAD_DIM
    ctx0 = n_batch * n_blocks

    def kv_spec(off):
        return pl.BlockSpec((tq, 2 * gw),
                            lambda b, g, jb: (b * n_blocks + jnp.clip(jb + off, 0, n_blocks - 1), g))

    def bias_index(b, g, jb):
        return (g, jnp.where(jb == 0, 0, jnp.where(jb == n_blocks - 1, 2, 1)), 0, 0)

    return pl.pallas_call(
        _na_kernel,
        out_shape=jax.ShapeDtypeStruct((n_batch * seq_len, n_heads * HEAD_DIM), BF16),
        grid=(n_batch, n_groups, n_blocks),
        in_specs=[pl.BlockSpec((tq, gw), lambda b, g, jb: (b * n_blocks + jb, 2 * n_groups + g)),
                  kv_spec(-1), kv_spec(0), kv_spec(1),
                  pl.BlockSpec((ctx_len, 2 * gw), lambda b, g, jb: (ctx0 + b, g)),
                  pl.BlockSpec((hps, None) + bias.shape[2:], bias_index)],
        out_specs=pl.BlockSpec((tq, gw), lambda b, g, jb: (b * n_blocks + jb, g)),
        compiler_params=_cparams(3),
        name="neighbourhood_attn",
    )(*([qkv] * 5), bias)


def _tiles():
    return dict(tm=1024, tm_proj=512, tm_out=512, tf=512, tn_mod=1024)


def _chunked(total, out_idx, out_col0=0, w_col0=0):
    return [(w_col0 + c, min(PROJ_CHUNK, total - c), out_idx, out_col0 + c, None)
            for c in range(0, total, PROJ_CHUNK)]


def _head_chunks(head_ops, w_col0, out_idx, out_col0):
    per = PROJ_CHUNK // HEAD_DIM
    return [(w_col0 + h0 * HEAD_DIM, len(head_ops[h0:h0 + per]) * HEAD_DIM, out_idx, out_col0 + h0 * HEAD_DIM,
             tuple(head_ops[h0:h0 + per])) for h0 in range(0, len(head_ops), per)]


def kernel(x, c, ctx, c_ctx, w_mod, b_mod, norm_g, ffn_w_in, ffn_w_out, ab_w_in, lru_conv_w, lru_conv_b,
           lru_w_a, lru_b_a, lru_w_x, lru_b_x, lru_lambda, attn_q_norm, attn_k_norm, attn_sink, ab_w_out,
           na_w_in, na_q_norm, na_k_norm, na_rpb, na_w_out):
    n_batch, seq_len, d = x.shape
    ctx_len = ctx.shape[1]
    depth = w_mod.shape[0]
    lru_w = lru_conv_w.shape[2]
    n_kv = attn_sink.shape[1] // GQA_GROUP
    n_att = n_kv * GQA_GROUP
    na_heads = na_rpb.shape[1]
    assert n_batch < MOD_ROWS and depth == 2
    t = _tiles()
    tl, tc = n_batch * seq_len, n_batch * ctx_len
    geom = dict(rows_per_batch=seq_len, n_batch=n_batch)

    cc = jnp.zeros((MOD_ROWS, d), F32).at[:n_batch].set(c).at[n_batch].set(c_ctx)
    mod = _mod_table(cc, w_mod, b_mod, t["tn_mod"])
    wg, wu, wo = _cast_ffn_weights(ffn_w_in, ffn_w_out)

    def ffn(x_src, n_rows, layer, which, row0=0):
        return _ffn(x_src, n_rows, row0, mod, layer, which, norm_g[layer, 2 * which], wg, wu, wo,
                    tm=t["tm"], tf=t["tf"], **geom)

    cos_t, sin_t = _rope_tables(seq_len)
    proj_geom = dict(tm=t["tm_proj"], n_lat_rows=tl, **geom)

    x_lat = ffn(x.reshape(tl, d), tl, 0, 0)
    x_ctx = ffn(ctx.reshape(tc, d), tc, 0, 0, row0=tl)
    gains = jnp.stack([attn_q_norm[0], attn_k_norm[0]])
    head_ops = [(0, True)] * n_att + [(1, True)] * n_kv + [None] * n_kv
    plan = _head_chunks(head_ops, 2 * lru_w, 1, 0) + _chunked(2 * lru_w, 0)
    p, qkv = _inproj(x_lat, x_ctx, mod, 0, norm_g[0, 1], ab_w_in[0].astype(BF16), gains, cos_t, sin_t, plan,
                     ((2 * lru_w, F32), (len(head_ops) * HEAD_DIM, BF16)), **proj_geom)
    lru_args = lambda dr: (lru_conv_w[0], lru_conv_b[0], lru_w_a[0, dr].astype(BF16), lru_b_a[0, dr],
                           lru_w_x[0, dr].astype(BF16), lru_b_x[0, dr], lru_lambda[0, dr])
    scan_geom = dict(n_batch=n_batch, seq_len=seq_len, ctx_len=ctx_len, tt=ctx_len)
    h_fwd = _lru_pass(p, *lru_args(0), None, reverse=False, **scan_geom)
    lru = _lru_pass(p, *lru_args(1), h_fwd, reverse=True, **scan_geom)
    att = _swa(qkv, attn_sink[0], n_batch=n_batch, seq_len=seq_len, ctx_len=ctx_len, n_kv_heads=n_kv,
               blk=WINDOW)
    x_all = _outproj(lru, att, 0, ab_w_out[0].astype(BF16), x_lat, x_ctx, tl + tc, mod, 0, tm=t["tm_out"], **geom)
    x_all = ffn(x_all, tl + tc, 0, 1)

    x_all = ffn(x_all, tl + tc, 1, 0)
    gains = jnp.stack([na_q_norm[0], na_k_norm[0]])
    na_d = na_heads * HEAD_DIM
    hps = _na_heads_per_step(na_heads)
    gw = hps * HEAD_DIM
    groups = range(0, na_d, gw)
    plan = sum([_head_chunks([(0, False)] * hps, g0, 0, 2 * na_d + g0) for g0 in groups], [])
    plan += sum([_head_chunks([(1, False)] * hps, na_d + g0, 0, 2 * g0) for g0 in groups], [])
    plan += sum([_head_chunks([None] * hps, 2 * na_d + g0, 0, 2 * g0 + gw) for g0 in groups], [])
    (qkv,) = _inproj(x_all, None, mod, 1, norm_g[1, 1], na_w_in[0].astype(BF16), gains, cos_t, sin_t, plan,
                     ((3 * na_d, BF16),), **proj_geom)
    o = _na(qkv, _na_bias_table(na_rpb[0], seq_len // GRID_W), n_batch=n_batch, seq_len=seq_len, ctx_len=ctx_len,
            n_heads=na_heads)
    x_lat = _outproj(o, o, 1, na_w_out[0].astype(BF16), x_all, None, tl, mod, 1, tm=t["tm_out"], **geom)
    x_lat = ffn(x_lat, tl, 1, 1)
    return x_lat.reshape(n_batch, seq_len, d)
```

```python
import functools

import numpy as np
import jax
import jax.numpy as jnp
from jax import lax
from jax.experimental import pallas as pl
from jax.experimental.pallas import tpu as pltpu

HEAD_DIM = 128
EPS = 1e-6
N_MOD = 9
GRID_W = 64
CONV_W = 4
LRU_C = 8.0
WINDOW = 128
ROPE_THETA = 10000.0
NA_KH = 8
NA_KW = 16
ATTN_SCALE = HEAD_DIM ** -0.5
GQA_GROUP = 4

LANES = 128
SUBLANES = 8
VMEM_LIMIT_BYTES = 56 * 1024 * 1024
MOD_ROWS = SUBLANES

NEG = -1e30
NA_ROWS_PER_BLOCK = 4

BF16 = jnp.bfloat16
F32 = jnp.float32


def _cparams(n_axes):
    return pltpu.CompilerParams(dimension_semantics=("arbitrary",) * n_axes,
                                vmem_limit_bytes=VMEM_LIMIT_BYTES)


def _sigmoid(x):
    return 1.0 / (1.0 + jnp.exp(-x))


def _sigmoid_tanh(x):
    return 0.5 * jnp.tanh(0.5 * x) + 0.5


NORM_CHUNK_ROWS = 16


def _norm_modulate_into(h_scr, x_ref, g, shift, scale, alt=None):
    gain = g * (1.0 + scale)
    n_chunks = x_ref.shape[0] // NORM_CHUNK_ROWS

    def load(c):
        r0 = pl.multiple_of(c * NORM_CHUNK_ROWS, NORM_CHUNK_ROWS)
        x = x_ref[pl.ds(r0, NORM_CHUNK_ROWS), :]
        if alt is not None:
            x = jnp.where(alt[0], alt[1][pl.ds(r0, NORM_CHUNK_ROWS), :], x)
        return r0, x

    def inv_rms(c):
        _, x = load(c)
        return lax.rsqrt(jnp.mean(x * x, axis=-1, keepdims=True) + EPS)

    def emit(c, inv):
        r0, x = load(c)
        h_scr[pl.ds(r0, NORM_CHUNK_ROWS), :] = ((x * inv) * gain + shift).astype(h_scr.dtype)

    def step(c, inv_prev):
        inv = inv_rms(c)
        emit(c - 1, inv_prev)
        return inv

    emit(n_chunks - 1, lax.fori_loop(1, n_chunks, step, inv_rms(0), unroll=8))


def _mod_kernel(c_ref, w_ref, b_ref, o_ref):
    c = c_ref[...]
    s = c * _sigmoid(c)
    o_ref[...] = jnp.dot(s.astype(BF16), w_ref[...].astype(BF16),
                         preferred_element_type=F32) + b_ref[...]


def _mod_table(cc, w_mod, b_mod, tn):
    depth, d, n = w_mod.shape
    return pl.pallas_call(
        _mod_kernel,
        out_shape=jax.ShapeDtypeStruct((depth, MOD_ROWS, n), F32),
        grid=(depth, n // tn),
        in_specs=[pl.BlockSpec((MOD_ROWS, d), lambda l, j: (0, 0)),
                  pl.BlockSpec((None, d, tn), lambda l, j: (l, 0, j)),
                  pl.BlockSpec((None, 1, tn), lambda l, j: (l, 0, j))],
        out_specs=pl.BlockSpec((None, MOD_ROWS, tn), lambda l, j: (l, 0, j)),
        compiler_params=_cparams(2),
        name="mod_table",
    )(cc, w_mod, b_mod.reshape(depth, 1, n))


def _mod_spec(d, layer, k, n_grid_axes):
    if n_grid_axes == 1:
        return pl.BlockSpec((None, MOD_ROWS, d), lambda i: (layer, 0, k))
    return pl.BlockSpec((None, MOD_ROWS, d), lambda i, j: (layer, 0, k))


def _mod_row(tile_idx, tm, rows_per_batch, n_batch):
    return jnp.minimum(tile_idx * tm // rows_per_batch, n_batch)


def _ffn_kernel(x_ref, g_ref, sh_ref, sc_ref, gt_ref, wg_ref, wu_ref, wo_ref, o_ref, h_scr,
                *, tm, tile0, rows_per_batch, n_batch, last_width):
    i, j = pl.program_id(0), pl.program_id(1)
    last = pl.num_programs(1) - 1
    r = _mod_row(i + tile0, tm, rows_per_batch, n_batch)

    def hidden_block(width, first):
        h = h_scr[...]
        a = jnp.dot(h, wg_ref[:, :width], preferred_element_type=F32)
        u = jnp.dot(h, wu_ref[:, :width], preferred_element_type=F32)
        act = (a * _sigmoid(a)) * u
        y = jnp.dot(act.astype(BF16), wo_ref[:width, :], preferred_element_type=F32)
        o_ref[...] = y if first else o_ref[...] + y

    @pl.when(j == 0)
    def _():
        _norm_modulate_into(h_scr, x_ref, g_ref[...], sh_ref[pl.ds(r, 1), :], sc_ref[pl.ds(r, 1), :])
        hidden_block(wg_ref.shape[1], True)

    pl.when(jnp.logical_and(j > 0, j < last))(lambda: hidden_block(wg_ref.shape[1], False))

    @pl.when(j == last)
    def _():
        hidden_block(last_width, False)
        o_ref[...] = x_ref[...] + 0.5 * gt_ref[pl.ds(r, 1), :] * o_ref[...]


def _ffn(x_src, n_rows, row0, mod, layer, which, g, wg, wu, wo, *, tm, tf, rows_per_batch, n_batch):
    d = x_src.shape[1]
    d_ff = wo.shape[2]
    nf = pl.cdiv(d_ff, tf)
    last_width = d_ff - (nf - 1) * tf
    assert last_width % LANES == 0 and row0 % tm == 0 and nf >= 2
    k0 = 6 * which
    kern = functools.partial(_ffn_kernel, tm=tm, tile0=row0 // tm, rows_per_batch=rows_per_batch,
                             n_batch=n_batch, last_width=last_width)
    return pl.pallas_call(
        kern,
        out_shape=jax.ShapeDtypeStruct((n_rows, d), F32),
        grid=(n_rows // tm, nf),
        in_specs=[pl.BlockSpec((tm, d), lambda i, j: (i, 0)),
                  pl.BlockSpec((1, d), lambda i, j: (0, 0)),
                  _mod_spec(d, layer, k0, 2), _mod_spec(d, layer, k0 + 1, 2), _mod_spec(d, layer, k0 + 2, 2),
                  pl.BlockSpec((None, None, d, tf), lambda i, j: (layer, which, 0, j)),
                  pl.BlockSpec((None, None, d, tf), lambda i, j: (layer, which, 0, j)),
                  pl.BlockSpec((None, None, tf, d), lambda i, j: (layer, which, j, 0))],
        out_specs=pl.BlockSpec((tm, d), lambda i, j: (i, 0)),
        scratch_shapes=[pltpu.VMEM((tm, d), BF16)],
        compiler_params=_cparams(2),
        name="half_ffn",
    )(x_src, g.reshape(1, d), mod, mod, mod, wg, wu, wo)


def _cast_kernel(*refs):
    n = len(refs) // 2
    for src, dst in zip(refs[:n], refs[n:]):
        dst[...] = src[...].astype(dst.dtype)


def _largest_tile(n, multiple, cap):
    return max(t for t in range(multiple, min(n, cap) + 1, multiple) if n % t == 0)


def _cast_ffn_weights(w_in, w_out):
    depth, n_ffn, d, f2 = w_in.shape
    f = f2 // 2
    assert f % LANES == 0
    rows = _largest_tile(d, 2 * SUBLANES, 256)
    half = (None, None, rows, f)
    wg, wu = pl.pallas_call(
        _cast_kernel,
        out_shape=[jax.ShapeDtypeStruct((depth, n_ffn, d, f), BF16)] * 2,
        grid=(depth, n_ffn, d // rows),
        in_specs=[pl.BlockSpec(half, lambda l, w, r: (l, w, r, 0)), pl.BlockSpec(half, lambda l, w, r: (l, w, r, 1))],
        out_specs=[pl.BlockSpec(half, lambda l, w, r: (l, w, r, 0))] * 2,
        compiler_params=_cparams(3),
        name="cast_ffn_in",
    )(w_in, w_in)
    rows = _largest_tile(f, 2 * SUBLANES, 1024)
    blk = (None, None, rows, d)
    wo = pl.pallas_call(
        _cast_kernel,
        out_shape=jax.ShapeDtypeStruct(w_out.shape, BF16),
        grid=(depth, n_ffn, f // rows),
        in_specs=[pl.BlockSpec(blk, lambda l, w, r: (l, w, r, 0))],
        out_specs=pl.BlockSpec(blk, lambda l, w, r: (l, w, r, 0)),
        compiler_params=_cparams(3),
        name="cast_ffn_out",
    )(w_out)
    return wg, wu, wo


PROJ_CHUNK = 4 * HEAD_DIM


def _swap_halves_32(y):
    lane = lax.broadcasted_iota(jnp.int32, y.shape, 1)
    return jnp.where((lane & 32) == 0, pltpu.roll(y, LANES - 32, 1), pltpu.roll(y, 32, 1))


def _inproj_kernel(x_ref, g_ref, sh_ref, sc_ref, w_ref, gains_ref, cos_ref, sin_ref, *rest,
                   plan, tm, rows_per_batch, n_batch, n_lat_tiles, n_tiles1):
    i = pl.program_id(0)
    alt = (i >= n_tiles1, rest[0]) if n_tiles1 is not None else None
    out_refs, h_scr = rest[(0 if alt is None else 1):-1], rest[-1]
    r = _mod_row(i, tm, rows_per_batch, n_batch)
    is_lat = i < n_lat_tiles
    _norm_modulate_into(h_scr, x_ref, g_ref[...], sh_ref[pl.ds(r, 1), :], sc_ref[pl.ds(r, 1), :], alt)
    for w_col, width, out_idx, out_col, ops in plan:
        y = jnp.dot(h_scr[...], w_ref[:, w_col:w_col + width], preferred_element_type=F32)
        o_ref = out_refs[out_idx]
        if ops is None:
            o_ref[:, out_col:out_col + width] = y.astype(o_ref.dtype)
            continue
        for hh, op in enumerate(ops):
            yh = y[:, hh * HEAD_DIM:(hh + 1) * HEAD_DIM]
            if op is not None:
                gain_row, rope = op
                yh = yh * lax.rsqrt(jnp.mean(yh * yh, axis=-1, keepdims=True) + EPS)
                yh = yh * gains_ref[gain_row:gain_row + 1, :]
                if rope:
                    yh = jnp.where(is_lat, yh * cos_ref[...] + _swap_halves_32(yh) * sin_ref[...], yh)
            o_ref[:, out_col + hh * HEAD_DIM:out_col + (hh + 1) * HEAD_DIM] = yh.astype(o_ref.dtype)


def _stream_specs(x_src, x_src2, tm):
    d = x_src.shape[1]
    if x_src2 is None:
        return [pl.BlockSpec((tm, d), lambda i: (i, 0))], None
    n1 = x_src.shape[0] // tm
    return [pl.BlockSpec((tm, d), lambda i: (jnp.minimum(i, n1 - 1), 0)),
            pl.BlockSpec((tm, d), lambda i: (jnp.maximum(i - n1, 0), 0))], n1


def _inproj(x_src, x_src2, mod, layer, g, w, gains, cos_t, sin_t, plan, outs, *, tm, n_lat_rows, rows_per_batch,
            n_batch):
    d = x_src.shape[1]
    ta = x_src.shape[0] + (0 if x_src2 is None else x_src2.shape[0])
    n_pos_tiles = cos_t.shape[0] // tm
    (x_spec, *x2_spec), n_tiles1 = _stream_specs(x_src, x_src2, tm)
    kern = functools.partial(_inproj_kernel, plan=tuple(plan), tm=tm, rows_per_batch=rows_per_batch,
                             n_batch=n_batch, n_lat_tiles=n_lat_rows // tm, n_tiles1=n_tiles1)
    return pl.pallas_call(
        kern,
        out_shape=[jax.ShapeDtypeStruct((ta, n), dt) for n, dt in outs],
        grid=(ta // tm,),
        in_specs=[x_spec,
                  pl.BlockSpec((1, d), lambda i: (0, 0)),
                  _mod_spec(d, layer, 3, 1), _mod_spec(d, layer, 4, 1),
                  pl.BlockSpec(w.shape, lambda i: (0, 0), pipeline_mode=pl.Buffered(1)),
                  pl.BlockSpec(gains.shape, lambda i: (0, 0)),
                  pl.BlockSpec((tm, HEAD_DIM), lambda i: (i % n_pos_tiles, 0)),
                  pl.BlockSpec((tm, HEAD_DIM), lambda i: (i % n_pos_tiles, 0))] + x2_spec,
        out_specs=[pl.BlockSpec((tm, n), lambda i: (i, 0)) for n, _ in outs],
        scratch_shapes=[pltpu.VMEM((tm, d), BF16)],
        compiler_params=_cparams(1),
        name="mixer_inproj",
    )(x_src, g.reshape(1, d), mod, mod, w, gains, cos_t, sin_t, *([] if x_src2 is None else [x_src2]))


def _outproj_kernel(a1_ref, a2_ref, w1_ref, w2_ref, gt_ref, x_ref, *rest, tm, rows_per_batch, n_batch, n_tiles1):
    o_ref = rest[-1]
    i = pl.program_id(0)
    r = _mod_row(i, tm, rows_per_batch, n_batch)
    d = o_ref.shape[1]
    for c0 in range(0, d, PROJ_CHUNK):
        cs = slice(c0, min(c0 + PROJ_CHUNK, d))
        y = (jnp.dot(a1_ref[...], w1_ref[:, cs], preferred_element_type=F32)
             + jnp.dot(a2_ref[...], w2_ref[:, cs], preferred_element_type=F32))
        x = x_ref[:, cs] if n_tiles1 is None else jnp.where(i >= n_tiles1, rest[0][:, cs], x_ref[:, cs])
        o_ref[:, cs] = x + gt_ref[pl.ds(r, 1), cs] * y


def _outproj(a1, a2, a2_col_block, w, x_src, x_src2, n_rows, mod, layer, *, tm, rows_per_batch, n_batch):
    d = x_src.shape[1]
    kh = w.shape[0] // 2
    x_specs, n_tiles1 = _stream_specs(x_src, x_src2, tm)
    kern = functools.partial(_outproj_kernel, tm=tm, rows_per_batch=rows_per_batch, n_batch=n_batch,
                             n_tiles1=n_tiles1)
    return pl.pallas_call(
        kern,
        out_shape=jax.ShapeDtypeStruct((n_rows, d), F32),
        grid=(n_rows // tm,),
        in_specs=[pl.BlockSpec((tm, kh), lambda i: (i, 0)),
                  pl.BlockSpec((tm, kh), lambda i: (i, a2_col_block)),
                  pl.BlockSpec((kh, d), lambda i: (0, 0)),
                  pl.BlockSpec((kh, d), lambda i: (1, 0)),
                  _mod_spec(d, layer, 5, 1)] + x_specs,
        out_specs=pl.BlockSpec((tm, d), lambda i: (i, 0)),
        compiler_params=_cparams(1),
        name="mixer_outproj",
    )(a1, a2, w, w, mod, x_src, *([] if x_src2 is None else [x_src2]))


def _rope_tables(seq_len):
    half = HEAD_DIM // 2
    nf = half // 2
    inv = ROPE_THETA ** (-jnp.arange(nf, dtype=F32) / nf)
    pos = jnp.arange(seq_len, dtype=jnp.int32)
    ang_r = (pos // GRID_W).astype(F32)[:, None] * inv[None, :]
    ang_c = (pos % GRID_W).astype(F32)[:, None] * inv[None, :]
    cr, sr, cc, sc = jnp.cos(ang_r), jnp.sin(ang_r), jnp.cos(ang_c), jnp.sin(ang_c)
    return (jnp.concatenate([cr, cr, cc, cc], axis=-1),
            jnp.concatenate([-sr, sr, -sc, sc], axis=-1))


def _gelu_tanh(x):
    return x * (0.5 * (1.0 + jnp.tanh(np.float32(np.sqrt(2.0 / np.pi)) * (x + 0.044715 * (x * x * x)))))


def _lru_kernel(*refs, tt, reverse, final, n_lat_chunks):
    if final:
        (u_ref, wa_ref, ba_ref, wx_ref, bx_ref, lam_ref, hf_ref, gl_ref, o_ref, a_scr, b_scr, carry_scr) = refs
    else:
        (xp_ref, xc_ref, xn_ref, cw_ref, cb_ref, wa_ref, ba_ref, wx_ref, bx_ref, lam_ref,
         o_ref, u_ref, xs_scr, a_scr, b_scr, carry_scr) = refs
    s = pl.program_id(1)
    is_ctx = s == 0

    @pl.when(is_ctx)
    def _():
        carry_scr[...] = jnp.zeros_like(carry_scr)

    width = a_scr.shape[1]
    halo = SUBLANES
    left = CONV_W // 2
    if not final:
        j = (n_lat_chunks - s) if reverse else (s - 1)
        has_prev = jnp.logical_and(jnp.logical_not(is_ctx), j > 0)
        has_next = jnp.logical_and(jnp.logical_not(is_ctx), j < n_lat_chunks - 1)
        xs_scr[0:halo, :] = jnp.where(has_prev, xp_ref[tt - halo:tt, :], 0.0)
        xs_scr[halo:halo + tt, :] = xc_ref[...]
        xs_scr[halo + tt:2 * halo + tt, :] = jnp.where(has_next, xn_ref[0:halo, :], 0.0)

    z = -lam_ref[...]
    softplus = jnp.maximum(z, 0.0) + jnp.log1p(jnp.exp(-jnp.abs(z)))
    for n in range(width // HEAD_DIM):
        cs = slice(n * HEAD_DIM, (n + 1) * HEAD_DIM)
        if final:
            u = u_ref[:, cs]
        else:
            u = cb_ref[:, cs]
            xa = xs_scr[:, cs]
            n_rows = xa.shape[0]
            for k in range(CONV_W):
                xk = xa if k == left else pltpu.roll(xa, (left - k) % n_rows, 0)
                u = u + xk[halo:halo + tt, :] * cw_ref[k:k + 1, cs]
            u_ref[:, cs] = u
        ub = u.astype(BF16)
        gate_r = jnp.dot(ub, wa_ref[n], preferred_element_type=F32) + ba_ref[:, cs]
        gate_i = jnp.dot(ub, wx_ref[n], preferred_element_type=F32) + bx_ref[:, cs]
        log_a = (-LRU_C) * _sigmoid_tanh(gate_r) * softplus[:, cs]
        a = jnp.exp(log_a)
        a_scr[:, cs] = a
        b_scr[:, cs] = jnp.sqrt(-jnp.tanh(log_a) * (1.0 + a * a)) * (_sigmoid_tanh(gate_i) * u)

    n_groups = tt // SUBLANES
    row = lax.broadcasted_iota(jnp.int32, (SUBLANES, width), 0)

    def group(gi, carry):
        g = (n_groups - 1 - gi) if reverse else gi
        r0 = pl.multiple_of(g * SUBLANES, SUBLANES)
        a = a_scr[pl.ds(r0, SUBLANES), :]
        b = b_scr[pl.ds(r0, SUBLANES), :]
        for k in (1, 2, 4):
            if reverse:
                keep = row < SUBLANES - k
                shift = SUBLANES - k
            else:
                keep = row >= k
                shift = k
            a_sh = jnp.where(keep, pltpu.roll(a, shift, 0), 1.0)
            b_sh = jnp.where(keep, pltpu.roll(b, shift, 0), 0.0)
            b = a * b_sh + b
            a = a * a_sh
        h = b + a * carry
        b_scr[pl.ds(r0, SUBLANES), :] = h
        last = h[0:1, :] if reverse else h[SUBLANES - 1:SUBLANES, :]
        return jnp.broadcast_to(last, (SUBLANES, width))

    carry_scr[...] = lax.fori_loop(0, n_groups, group, carry_scr[...])

    if final:
        o_ref[...] = ((hf_ref[...] + b_scr[...]) * _gelu_tanh(gl_ref[...])).astype(o_ref.dtype)
    else:
        o_ref[...] = b_scr[...]


def _lru_pass(p, conv_w, conv_b, w_a, b_a, w_x, b_x, lam, first, *, reverse, n_batch, seq_len, ctx_len, tt):
    ta = p.shape[0]
    width = conv_w.shape[1]
    assert ctx_len == tt and seq_len % tt == 0
    n_lat_chunks = seq_len // tt
    ctx_block0 = n_batch * n_lat_chunks
    final = first is not None

    def chunk_index(b, s, off):
        j = (n_lat_chunks - s) if reverse else (s - 1)
        j = jnp.clip(j + off, 0, n_lat_chunks - 1)
        return jnp.where(s == 0, ctx_block0 + b, b * n_lat_chunks + j)

    def xspec(off):
        return pl.BlockSpec((tt, width), lambda b, s: (chunk_index(b, s, off), 0))

    row_spec = pl.BlockSpec((1, width), lambda b, s: (0, 0))
    w_spec = pl.BlockSpec(w_a.shape, lambda b, s: (0, 0, 0))
    gate_specs = [w_spec, row_spec, w_spec, row_spec, row_spec]
    gate_args = [w_a, b_a.reshape(1, width), w_x, b_x.reshape(1, width), lam.reshape(1, width)]
    scan_scratch = [pltpu.VMEM((tt, width), F32), pltpu.VMEM((tt, width), F32), pltpu.VMEM((SUBLANES, width), F32)]
    if final:
        h_first, u = first
        in_specs = [xspec(0)] + gate_specs + [xspec(0),
                                              pl.BlockSpec((tt, width), lambda b, s: (chunk_index(b, s, 0), 1))]
        args = [u] + gate_args + [h_first, p]
        out_shape, out_specs, scratch = jax.ShapeDtypeStruct((ta, width), BF16), xspec(0), scan_scratch
    else:
        in_specs = [xspec(-1), xspec(0), xspec(1),
                    pl.BlockSpec((CONV_W, width), lambda b, s: (0, 0)), row_spec] + gate_specs
        args = [p, p, p, conv_w, conv_b.reshape(1, width)] + gate_args
        out_shape = [jax.ShapeDtypeStruct((ta, width), F32)] * 2
        out_specs = [xspec(0), xspec(0)]
        scratch = [pltpu.VMEM((tt + 2 * SUBLANES, width), F32)] + scan_scratch
    kern = functools.partial(_lru_kernel, tt=tt, reverse=reverse, final=final, n_lat_chunks=n_lat_chunks)
    return pl.pallas_call(
        kern,
        out_shape=out_shape,
        grid=(n_batch, n_lat_chunks + 1),
        in_specs=in_specs,
        out_specs=out_specs,
        scratch_shapes=scratch,
        compiler_params=_cparams(2),
        name="rglru_rev" if reverse else "rglru_fwd",
    )(*args)


LOG2E = float(np.log2(np.e))
LOGIT_SCALE = ATTN_SCALE * LOG2E


def _softmax_parts(s2, extra_logit2=None):
    m = jnp.max(s2, axis=-1, keepdims=True)
    if extra_logit2 is not None:
        m = jnp.maximum(m, extra_logit2)
    e = jnp.exp2(s2 - m)
    denom = jnp.sum(e, axis=-1, keepdims=True)
    if extra_logit2 is not None:
        denom = denom + jnp.exp2(extra_logit2 - m)
    return e, 1.0 / denom


def _swa_kernel(sink_ref, q_ref, kp_ref, kc_ref, kn_ref, kx_ref, vp_ref, vc_ref, vn_ref, vx_ref, mask_ref, o_ref,
                *, n_kv_heads):
    blk = q_ref.shape[0]
    head = lax.broadcasted_iota(jnp.int32, (GQA_GROUP * blk, 1), 0) // blk
    logits = []
    for kh in range(n_kv_heads):
        ks = slice(kh * HEAD_DIM, (kh + 1) * HEAD_DIM)
        q0 = kh * GQA_GROUP
        q = jnp.concatenate([q_ref[:, (q0 + h) * HEAD_DIM:(q0 + h + 1) * HEAD_DIM] for h in range(GQA_GROUP)],
                            axis=0)
        k = jnp.concatenate([kp_ref[:, ks], kc_ref[:, ks], kn_ref[:, ks], kx_ref[:, ks]], axis=0)
        logits.append(lax.dot_general(q, k, (((1,), (1,)), ((), ())), preferred_element_type=F32))
    for kh in range(n_kv_heads):
        ks = slice(kh * HEAD_DIM, (kh + 1) * HEAD_DIM)
        q0 = kh * GQA_GROUP
        v = jnp.concatenate([vp_ref[:, ks], vc_ref[:, ks], vn_ref[:, ks], vx_ref[:, ks]], axis=0)
        s = logits[kh] * LOGIT_SCALE + mask_ref[...]
        sink = jnp.zeros((GQA_GROUP * blk, 1), F32)
        for h in range(GQA_GROUP):
            sink = jnp.where(head == h, sink_ref[q0 + h] * LOG2E, sink)
        e, inv = _softmax_parts(s, sink)
        o = jnp.dot(e.astype(BF16), v, preferred_element_type=F32) * inv
        for h in range(GQA_GROUP):
            o_ref[:, (q0 + h) * HEAD_DIM:(q0 + h + 1) * HEAD_DIM] = o[h * blk:(h + 1) * blk, :].astype(o_ref.dtype)


def _swa_mask_table(blk, ctx_len):
    shape = (4, GQA_GROUP * blk, 3 * blk + ctx_len)
    cls = lax.broadcasted_iota(jnp.int32, shape, 0)
    qi = lax.broadcasted_iota(jnp.int32, shape, 1) % blk
    col = lax.broadcasted_iota(jnp.int32, shape, 2)
    lo = jnp.where(cls == 0, blk, 0)
    hi = jnp.where(cls == 3, 0, jnp.where(cls == 2, 2 * blk, 3 * blk))
    rel = col - blk - qi
    ok = ((col >= lo) & (col < hi) & (rel >= -WINDOW) & (rel <= WINDOW)) | (col >= 3 * blk)
    return jnp.where(ok, 0.0, NEG).astype(F32)


def _swa(qkv, sink, *, n_batch, seq_len, ctx_len, n_kv_heads, blk):
    ta = qkv.shape[0]
    n_heads = n_kv_heads * GQA_GROUP
    n_lat_blocks = seq_len // blk
    n_ctx_blocks = ctx_len // blk
    assert blk == WINDOW and seq_len % blk == 0 and ctx_len % blk == 0 and n_lat_blocks >= 2
    ctx_q0 = n_batch * n_lat_blocks
    ctx_kv0 = n_batch * seq_len // ctx_len
    kv_w = n_kv_heads * HEAD_DIM
    k_col, v_col = n_heads * HEAD_DIM // kv_w, n_heads * HEAD_DIM // kv_w + 1

    def q_index(b, n):
        return (jnp.where(n < n_lat_blocks, b * n_lat_blocks + n, ctx_q0 + b * n_ctx_blocks + n - n_lat_blocks), 0)

    def kv_spec(off, col):
        return pl.BlockSpec((blk, kv_w),
                            lambda b, n: (b * n_lat_blocks + jnp.clip(n + off, 0, n_lat_blocks - 1), col))

    def ctx_spec(col):
        return pl.BlockSpec((ctx_len, kv_w), lambda b, n: (ctx_kv0 + b, col))

    def mask_index(b, n):
        interior = jnp.where(n == 0, 0, jnp.where(n == n_lat_blocks - 1, 2, 1))
        return (jnp.where(n < n_lat_blocks, interior, 3), 0, 0)

    mask = _swa_mask_table(blk, ctx_len)
    kern = functools.partial(_swa_kernel, n_kv_heads=n_kv_heads)
    return pl.pallas_call(
        kern,
        out_shape=jax.ShapeDtypeStruct((ta, n_heads * HEAD_DIM), BF16),
        grid=(n_batch, n_lat_blocks + n_ctx_blocks),
        in_specs=[pl.BlockSpec(memory_space=pltpu.SMEM),
                  pl.BlockSpec((blk, n_heads * HEAD_DIM), q_index),
                  kv_spec(-1, k_col), kv_spec(0, k_col), kv_spec(1, k_col), ctx_spec(k_col),
                  kv_spec(-1, v_col), kv_spec(0, v_col), kv_spec(1, v_col), ctx_spec(v_col),
                  pl.BlockSpec((None,) + mask.shape[1:], mask_index)],
        out_specs=pl.BlockSpec((blk, n_heads * HEAD_DIM), q_index),
        compiler_params=_cparams(2),
        name="windowed_gqa",
    )(sink, *([qkv] * 9), mask)


NA_MAX_HEADS_PER_STEP = 16


def _na_heads_per_step(n_heads):
    return min(NA_MAX_HEADS_PER_STEP, n_heads)


def _na_kernel(q_ref, kvp_ref, kvc_ref, kvn_ref, kvx_ref, bias_ref, o_ref):
    n_local = bias_ref.shape[-1]
    gw = q_ref.shape[1]
    n = gw // HEAD_DIM
    kv_refs = (kvp_ref, kvc_ref, kvn_ref, kvx_ref)
    heads = [slice(h * HEAD_DIM, (h + 1) * HEAD_DIM) for h in range(n)]
    def qk(h):
        k = jnp.concatenate([ref[:, heads[h]] for ref in kv_refs], axis=0)
        return lax.dot_general(q_ref[:, heads[h]], k, (((1,), (1,)), ((), ())), preferred_element_type=F32)

    def softmax(h, logit):
        s = logit * LOGIT_SCALE
        s = jnp.concatenate([s[:, :n_local] + bias_ref[h], s[:, n_local:]], axis=1)
        e, inv = _softmax_parts(s)
        return e.astype(BF16), inv

    def pv(h, e, inv):
        vs = slice(gw + heads[h].start, gw + heads[h].stop)
        v = jnp.concatenate([ref[:, vs] for ref in kv_refs], axis=0)
        o_ref[:, heads[h]] = (jnp.dot(e, v, preferred_element_type=F32) * inv).astype(o_ref.dtype)

    logits = {0: qk(0)}
    weights = {}
    for h in range(n):
        if h + 1 < n:
            logits[h + 1] = qk(h + 1)
        weights[h] = softmax(h, logits.pop(h))
        if h >= 1:
            pv(h - 1, *weights.pop(h - 1))
    pv(n - 1, *weights.pop(n - 1))


def _na_bias_table(rpb, n_grid_rows):
    r_in, w = NA_ROWS_PER_BLOCK, GRID_W
    kh = min(NA_KH, n_grid_rows)
    n_blocks = n_grid_rows // r_in
    n_ro, n_co = 2 * NA_KH - 1, 2 * NA_KW - 1
    qc, kc = np.arange(w)[:, None], np.arange(w)[None, :]
    e_col = (kc - qc + NA_KW - 1 == np.arange(n_co)[:, None, None]).astype(np.float32)
    win_start = np.clip(qc - NA_KW // 2, 0, w - NA_KW)
    col_ok = (kc >= win_start) & (kc < win_start + NA_KW)
    ri, krj = np.arange(r_in)[:, None], np.arange(3 * r_in)[None, :]
    row_ok = []
    for jb in (0, max(n_blocks // 2, 1) if n_blocks > 2 else 0, n_blocks - 1):
        r, kr = r_in * jb + ri, r_in * (jb - 1) + krj
        rs = np.clip(r - kh // 2, 0, n_grid_rows - kh)
        row_ok.append((kr >= rs) & (kr < rs + kh))
    ok = np.stack(row_ok)[:, :, None, :, None] & col_ok[None, None, :, None, :]
    ok = ok.reshape(3, r_in * w, 3 * r_in * w)
    tz = jnp.einsum('hab,bqc->haqc', rpb.astype(F32), e_col, precision=lax.Precision.HIGHEST) * LOG2E
    assert r_in - 1 <= NA_KH - 1 - r_in and 3 * r_in - 1 - r_in + NA_KH - 1 < n_ro
    full = jnp.concatenate(
        [jnp.concatenate([tz[:, k - i - r_in + NA_KH - 1] for k in range(3 * r_in)], axis=-1) for i in range(r_in)],
        axis=-2)
    return jnp.where(ok[None], full[:, None], NEG)


def _na(qkv, bias, *, n_batch, seq_len, ctx_len, n_heads):
    tq = NA_ROWS_PER_BLOCK * GRID_W
    n_blocks = seq_len // tq
    assert ctx_len == tq and seq_len % tq == 0 and NA_KH == 2 * NA_ROWS_PER_BLOCK and n_blocks >= 2
    hps = _na_heads_per_step(n_heads)
    assert n_heads % hps == 0
    n_groups = n_heads // hps
    gw = hps * HEAD_DIM
    ctx0 = n_batch * n_blocks

    def kv_spec(off):
        return pl.BlockSpec((tq, 2 * gw),
                            lambda b, g, jb: (b * n_blocks + jnp.clip(jb + off, 0, n_blocks - 1), g))

    def bias_index(b, g, jb):
        return (g, jnp.where(jb == 0, 0, jnp.where(jb == n_blocks - 1, 2, 1)), 0, 0)

    return pl.pallas_call(
        _na_kernel,
        out_shape=jax.ShapeDtypeStruct((n_batch * seq_len, n_heads * HEAD_DIM), BF16),
        grid=(n_batch, n_groups, n_blocks),
        in_specs=[pl.BlockSpec((tq, gw), lambda b, g, jb: (b * n_blocks + jb, 2 * n_groups + g)),
                  kv_spec(-1), kv_spec(0), kv_spec(1),
                  pl.BlockSpec((ctx_len, 2 * gw), lambda b, g, jb: (ctx0 + b, g)),
                  pl.BlockSpec((hps, None) + bias.shape[2:], bias_index)],
        out_specs=pl.BlockSpec((tq, gw), lambda b, g, jb: (b * n_blocks + jb, g)),
        compiler_params=_cparams(3),
        name="neighbourhood_attn",
    )(*([qkv] * 5), bias)


def _tiles():
    return dict(tm=1024, tm_proj=512, tm_out=512, tf=512, tn_mod=1024)


def _chunked(total, out_idx, out_col0=0, w_col0=0):
    return [(w_col0 + c, min(PROJ_CHUNK, total - c), out_idx, out_col0 + c, None)
            for c in range(0, total, PROJ_CHUNK)]


def _head_chunks(head_ops, w_col0, out_idx, out_col0):
    per = PROJ_CHUNK // HEAD_DIM
    return [(w_col0 + h0 * HEAD_DIM, len(head_ops[h0:h0 + per]) * HEAD_DIM, out_idx, out_col0 + h0 * HEAD_DIM,
             tuple(head_ops[h0:h0 + per])) for h0 in range(0, len(head_ops), per)]


def kernel(x, c, ctx, c_ctx, w_mod, b_mod, norm_g, ffn_w_in, ffn_w_out, ab_w_in, lru_conv_w, lru_conv_b,
           lru_w_a, lru_b_a, lru_w_x, lru_b_x, lru_lambda, attn_q_norm, attn_k_norm, attn_sink, ab_w_out,
           na_w_in, na_q_norm, na_k_norm, na_rpb, na_w_out):
    n_batch, seq_len, d = x.shape
    ctx_len = ctx.shape[1]
    depth = w_mod.shape[0]
    lru_w = lru_conv_w.shape[2]
    n_kv = attn_sink.shape[1] // GQA_GROUP
    n_att = n_kv * GQA_GROUP
    na_heads = na_rpb.shape[1]
    assert n_batch < MOD_ROWS and depth == 2
    t = _tiles()
    tl, tc = n_batch * seq_len, n_batch * ctx_len
    geom = dict(rows_per_batch=seq_len, n_batch=n_batch)

    cc = jnp.zeros((MOD_ROWS, d), F32).at[:n_batch].set(c).at[n_batch].set(c_ctx)
    mod = _mod_table(cc, w_mod, b_mod, t["tn_mod"])
    wg, wu, wo = _cast_ffn_weights(ffn_w_in, ffn_w_out)

    def ffn(x_src, n_rows, layer, which, row0=0):
        return _ffn(x_src, n_rows, row0, mod, layer, which, norm_g[layer, 2 * which], wg, wu, wo,
                    tm=t["tm"], tf=t["tf"], **geom)

    cos_t, sin_t = _rope_tables(seq_len)
    proj_geom = dict(tm=t["tm_proj"], n_lat_rows=tl, **geom)

    x_lat = ffn(x.reshape(tl, d), tl, 0, 0)
    x_ctx = ffn(ctx.reshape(tc, d), tc, 0, 0, row0=tl)
    gains = jnp.stack([attn_q_norm[0], attn_k_norm[0]])
    head_ops = [(0, True)] * n_att + [(1, True)] * n_kv + [None] * n_kv
    plan = _head_chunks(head_ops, 2 * lru_w, 1, 0) + _chunked(2 * lru_w, 0)
    p, qkv = _inproj(x_lat, x_ctx, mod, 0, norm_g[0, 1], ab_w_in[0].astype(BF16), gains, cos_t, sin_t, plan,
                     ((2 * lru_w, F32), (len(head_ops) * HEAD_DIM, BF16)), **proj_geom)
    lru_args = lambda dr: (lru_conv_w[0], lru_conv_b[0], lru_w_a[0, dr].astype(BF16), lru_b_a[0, dr],
                           lru_w_x[0, dr].astype(BF16), lru_b_x[0, dr], lru_lambda[0, dr])
    scan_geom = dict(n_batch=n_batch, seq_len=seq_len, ctx_len=ctx_len, tt=ctx_len)
    h_fwd = _lru_pass(p, *lru_args(0), None, reverse=False, **scan_geom)
    lru = _lru_pass(p, *lru_args(1), h_fwd, reverse=True, **scan_geom)
    att = _swa(qkv, attn_sink[0], n_batch=n_batch, seq_len=seq_len, ctx_len=ctx_len, n_kv_heads=n_kv,
               blk=WINDOW)
    x_all = _outproj(lru, att, 0, ab_w_out[0].astype(BF16), x_lat, x_ctx, tl + tc, mod, 0, tm=t["tm_out"], **geom)
    x_all = ffn(x_all, tl + tc, 0, 1)

    x_all = ffn(x_all, tl + tc, 1, 0)
    gains = jnp.stack([na_q_norm[0], na_k_norm[0]])
    na_d = na_heads * HEAD_DIM
    hps = _na_heads_per_step(na_heads)
    gw = hps * HEAD_DIM
    groups = range(0, na_d, gw)
    plan = sum([_head_chunks([(0, False)] * hps, g0, 0, 2 * na_d + g0) for g0 in groups], [])
    plan += sum([_head_chunks([(1, False)] * hps, na_d + g0, 0, 2 * g0) for g0 in groups], [])
    plan += sum([_head_chunks([None] * hps, 2 * na_d + g0, 0, 2 * g0 + gw) for g0 in groups], [])
    (qkv,) = _inproj(x_all, None, mod, 1, norm_g[1, 1], na_w_in[0].astype(BF16), gains, cos_t, sin_t, plan,
                     ((3 * na_d, BF16),), **proj_geom)
    o = _na(qkv, _na_bias_table(na_rpb[0], seq_len // GRID_W), n_batch=n_batch, seq_len=seq_len, ctx_len=ctx_len,
            n_heads=na_heads)
    x_lat = _outproj(o, o, 1, na_w_out[0].astype(BF16), x_all, None, tl, mod, 1, tm=t["tm_out"], **geom)
    x_lat = ffn(x_lat, tl, 1, 1)
    return x_lat.reshape(n_batch, seq_len, d)
```

```python
import functools

import numpy as np
import jax
import jax.numpy as jnp
from jax import lax
from jax.experimental import pallas as pl
from jax.experimental.pallas import tpu as pltpu

HEAD_DIM = 128
EPS = 1e-6
N_MOD = 9
GRID_W = 64
CONV_W = 4
LRU_C = 8.0
WINDOW = 128
ROPE_THETA = 10000.0
NA_KH = 8
NA_KW = 16
ATTN_SCALE = HEAD_DIM ** -0.5
GQA_GROUP = 4

LANES = 128
SUBLANES = 8
VMEM_LIMIT_BYTES = 56 * 1024 * 1024
MOD_ROWS = SUBLANES

NEG = -1e30
NA_ROWS_PER_BLOCK = 4

BF16 = jnp.bfloat16
F32 = jnp.float32


def _cparams(n_axes):
    return pltpu.CompilerParams(dimension_semantics=("arbitrary",) * n_axes,
                                vmem_limit_bytes=VMEM_LIMIT_BYTES)


def _sigmoid(x):
    return 1.0 / (1.0 + jnp.exp(-x))


def _sigmoid_tanh(x):
    return 0.5 * jnp.tanh(0.5 * x) + 0.5


NORM_CHUNK_ROWS = 16


def _norm_modulate_into(h_scr, x_ref, g, shift, scale, alt=None):
    gain = g * (1.0 + scale)
    n_chunks = x_ref.shape[0] // NORM_CHUNK_ROWS

    def load(c):
        r0 = pl.multiple_of(c * NORM_CHUNK_ROWS, NORM_CHUNK_ROWS)
        x = x_ref[pl.ds(r0, NORM_CHUNK_ROWS), :]
        if alt is not None:
            x = jnp.where(alt[0], alt[1][pl.ds(r0, NORM_CHUNK_ROWS), :], x)
        return r0, x

    def inv_rms(c):
        _, x = load(c)
        return lax.rsqrt(jnp.mean(x * x, axis=-1, keepdims=True) + EPS)

    def emit(c, inv):
        r0, x = load(c)
        h_scr[pl.ds(r0, NORM_CHUNK_ROWS), :] = ((x * inv) * gain + shift).astype(h_scr.dtype)

    def step(c, inv_prev):
        inv = inv_rms(c)
        emit(c - 1, inv_prev)
        return inv

    emit(n_chunks - 1, lax.fori_loop(1, n_chunks, step, inv_rms(0), unroll=8))


def _mod_kernel(c_ref, w_ref, b_ref, o_ref):
    c = c_ref[...]
    s = c * _sigmoid(c)
    o_ref[...] = jnp.dot(s.astype(BF16), w_ref[...].astype(BF16),
                         preferred_element_type=F32) + b_ref[...]


def _mod_table(cc, w_mod, b_mod, tn):
    depth, d, n = w_mod.shape
    return pl.pallas_call(
        _mod_kernel,
        out_shape=jax.ShapeDtypeStruct((depth, MOD_ROWS, n), F32),
        grid=(depth, n // tn),
        in_specs=[pl.BlockSpec((MOD_ROWS, d), lambda l, j: (0, 0)),
                  pl.BlockSpec((None, d, tn), lambda l, j: (l, 0, j)),
                  pl.BlockSpec((None, 1, tn), lambda l, j: (l, 0, j))],
        out_specs=pl.BlockSpec((None, MOD_ROWS, tn), lambda l, j: (l, 0, j)),
        compiler_params=_cparams(2),
        name="mod_table",
    )(cc, w_mod, b_mod.reshape(depth, 1, n))


def _mod_spec(d, layer, k, n_grid_axes):
    if n_grid_axes == 1:
        return pl.BlockSpec((None, MOD_ROWS, d), lambda i: (layer, 0, k))
    return pl.BlockSpec((None, MOD_ROWS, d), lambda i, j: (layer, 0, k))


def _mod_row(tile_idx, tm, rows_per_batch, n_batch):
    return jnp.minimum(tile_idx * tm // rows_per_batch, n_batch)


def _ffn_kernel(x_ref, g_ref, sh_ref, sc_ref, gt_ref, wg_ref, wu_ref, wo_ref, o_ref, h_scr,
                *, tm, tile0, rows_per_batch, n_batch, last_width):
    i, j = pl.program_id(0), pl.program_id(1)
    last = pl.num_programs(1) - 1
    r = _mod_row(i + tile0, tm, rows_per_batch, n_batch)

    def hidden_block(width, position):
        h = h_scr[...]
        a = jnp.dot(h, wg_ref[:, :width], preferred_element_type=F32)
        u = jnp.dot(h, wu_ref[:, :width], preferred_element_type=F32)
        act = (a * _sigmoid(a)) * u
        y = jnp.dot(act.astype(BF16), wo_ref[:width, :], preferred_element_type=F32)
        if position == "first":
            o_ref[...] = y
        elif position == "middle":
            o_ref[...] = o_ref[...] + y
        else:
            o_ref[...] = x_ref[...] + 0.5 * gt_ref[pl.ds(r, 1), :] * (o_ref[...] + y)

    @pl.when(j == 0)
    def _():
        _norm_modulate_into(h_scr, x_ref, g_ref[...], sh_ref[pl.ds(r, 1), :], sc_ref[pl.ds(r, 1), :])
        hidden_block(wg_ref.shape[1], "first")

    pl.when(jnp.logical_and(j > 0, j < last))(lambda: hidden_block(wg_ref.shape[1], "middle"))
    pl.when(j == last)(lambda: hidden_block(last_width, "last"))


def _ffn(x_src, n_rows, row0, mod, layer, which, g, wg, wu, wo, *, tm, tf, rows_per_batch, n_batch):
    d = x_src.shape[1]
    d_ff = wo.shape[2]
    nf = pl.cdiv(d_ff, tf)
    last_width = d_ff - (nf - 1) * tf
    assert last_width % LANES == 0 and row0 % tm == 0 and nf >= 2
    k0 = 6 * which
    kern = functools.partial(_ffn_kernel, tm=tm, tile0=row0 // tm, rows_per_batch=rows_per_batch,
                             n_batch=n_batch, last_width=last_width)
    return pl.pallas_call(
        kern,
        out_shape=jax.ShapeDtypeStruct((n_rows, d), F32),
        grid=(n_rows // tm, nf),
        in_specs=[pl.BlockSpec((tm, d), lambda i, j: (i, 0)),
                  pl.BlockSpec((1, d), lambda i, j: (0, 0)),
                  _mod_spec(d, layer, k0, 2), _mod_spec(d, layer, k0 + 1, 2), _mod_spec(d, layer, k0 + 2, 2),
                  pl.BlockSpec((None, None, d, tf), lambda i, j: (layer, which, 0, j)),
                  pl.BlockSpec((None, None, d, tf), lambda i, j: (layer, which, 0, j)),
                  pl.BlockSpec((None, None, tf, d), lambda i, j: (layer, which, j, 0))],
        out_specs=pl.BlockSpec((tm, d), lambda i, j: (i, 0)),
        scratch_shapes=[pltpu.VMEM((tm, d), BF16)],
        compiler_params=_cparams(2),
        name="half_ffn",
    )(x_src, g.reshape(1, d), mod, mod, mod, wg, wu, wo)


def _cast_kernel(*refs):
    n = len(refs) // 2
    for src, dst in zip(refs[:n], refs[n:]):
        dst[...] = src[...].astype(dst.dtype)


def _largest_tile(n, multiple, cap):
    return max(t for t in range(multiple, min(n, cap) + 1, multiple) if n % t == 0)


def _cast_ffn_weights(w_in, w_out):
    depth, n_ffn, d, f2 = w_in.shape
    f = f2 // 2
    assert f % LANES == 0
    rows = _largest_tile(d, 2 * SUBLANES, 256)
    half = (None, None, rows, f)
    wg, wu = pl.pallas_call(
        _cast_kernel,
        out_shape=[jax.ShapeDtypeStruct((depth, n_ffn, d, f), BF16)] * 2,
        grid=(depth, n_ffn, d // rows),
        in_specs=[pl.BlockSpec(half, lambda l, w, r: (l, w, r, 0)), pl.BlockSpec(half, lambda l, w, r: (l, w, r, 1))],
        out_specs=[pl.BlockSpec(half, lambda l, w, r: (l, w, r, 0))] * 2,
        compiler_params=_cparams(3),
        name="cast_ffn_in",
    )(w_in, w_in)
    rows = _largest_tile(f, 2 * SUBLANES, 1024)
    blk = (None, None, rows, d)
    wo = pl.pallas_call(
        _cast_kernel,
        out_shape=jax.ShapeDtypeStruct(w_out.shape, BF16),
        grid=(depth, n_ffn, f // rows),
        in_specs=[pl.BlockSpec(blk, lambda l, w, r: (l, w, r, 0))],
        out_specs=pl.BlockSpec(blk, lambda l, w, r: (l, w, r, 0)),
        compiler_params=_cparams(3),
        name="cast_ffn_out",
    )(w_out)
    return wg, wu, wo


PROJ_CHUNK = 4 * HEAD_DIM


def _swap_halves_32(y):
    lane = lax.broadcasted_iota(jnp.int32, y.shape, 1)
    return jnp.where((lane & 32) == 0, pltpu.roll(y, LANES - 32, 1), pltpu.roll(y, 32, 1))


def _inproj_kernel(x_ref, g_ref, sh_ref, sc_ref, w_ref, gains_ref, cos_ref, sin_ref, *rest,
                   plan, tm, rows_per_batch, n_batch, n_lat_tiles, n_tiles1):
    i = pl.program_id(0)
    alt = (i >= n_tiles1, rest[0]) if n_tiles1 is not None else None
    out_refs, h_scr = rest[(0 if alt is None else 1):-1], rest[-1]
    r = _mod_row(i, tm, rows_per_batch, n_batch)
    is_lat = i < n_lat_tiles
    _norm_modulate_into(h_scr, x_ref, g_ref[...], sh_ref[pl.ds(r, 1), :], sc_ref[pl.ds(r, 1), :], alt)
    for w_col, width, out_idx, out_col, ops in plan:
        y = jnp.dot(h_scr[...], w_ref[:, w_col:w_col + width], preferred_element_type=F32)
        o_ref = out_refs[out_idx]
        if ops is None:
            o_ref[:, out_col:out_col + width] = y.astype(o_ref.dtype)
            continue
        for hh, op in enumerate(ops):
            yh = y[:, hh * HEAD_DIM:(hh + 1) * HEAD_DIM]
            if op is not None:
                gain_row, rope = op
                yh = yh * lax.rsqrt(jnp.mean(yh * yh, axis=-1, keepdims=True) + EPS)
                yh = yh * gains_ref[gain_row:gain_row + 1, :]
                if rope:
                    yh = jnp.where(is_lat, yh * cos_ref[...] + _swap_halves_32(yh) * sin_ref[...], yh)
            o_ref[:, out_col + hh * HEAD_DIM:out_col + (hh + 1) * HEAD_DIM] = yh.astype(o_ref.dtype)


def _stream_specs(x_src, x_src2, tm):
    d = x_src.shape[1]
    if x_src2 is None:
        return [pl.BlockSpec((tm, d), lambda i: (i, 0))], None
    n1 = x_src.shape[0] // tm
    return [pl.BlockSpec((tm, d), lambda i: (jnp.minimum(i, n1 - 1), 0)),
            pl.BlockSpec((tm, d), lambda i: (jnp.maximum(i - n1, 0), 0))], n1


def _inproj(x_src, x_src2, mod, layer, g, w, gains, cos_t, sin_t, plan, outs, *, tm, n_lat_rows, rows_per_batch,
            n_batch):
    d = x_src.shape[1]
    ta = x_src.shape[0] + (0 if x_src2 is None else x_src2.shape[0])
    n_pos_tiles = cos_t.shape[0] // tm
    (x_spec, *x2_spec), n_tiles1 = _stream_specs(x_src, x_src2, tm)
    kern = functools.partial(_inproj_kernel, plan=tuple(plan), tm=tm, rows_per_batch=rows_per_batch,
                             n_batch=n_batch, n_lat_tiles=n_lat_rows // tm, n_tiles1=n_tiles1)
    return pl.pallas_call(
        kern,
        out_shape=[jax.ShapeDtypeStruct((ta, n), dt) for n, dt in outs],
        grid=(ta // tm,),
        in_specs=[x_spec,
                  pl.BlockSpec((1, d), lambda i: (0, 0)),
                  _mod_spec(d, layer, 3, 1), _mod_spec(d, layer, 4, 1),
                  pl.BlockSpec(w.shape, lambda i: (0, 0), pipeline_mode=pl.Buffered(1)),
                  pl.BlockSpec(gains.shape, lambda i: (0, 0)),
                  pl.BlockSpec((tm, HEAD_DIM), lambda i: (i % n_pos_tiles, 0)),
                  pl.BlockSpec((tm, HEAD_DIM), lambda i: (i % n_pos_tiles, 0))] + x2_spec,
        out_specs=[pl.BlockSpec((tm, n), lambda i: (i, 0)) for n, _ in outs],
        scratch_shapes=[pltpu.VMEM((tm, d), BF16)],
        compiler_params=_cparams(1),
        name="mixer_inproj",
    )(x_src, g.reshape(1, d), mod, mod, w, gains, cos_t, sin_t, *([] if x_src2 is None else [x_src2]))


def _outproj_kernel(a1_ref, a2_ref, w1_ref, w2_ref, gt_ref, x_ref, *rest, tm, rows_per_batch, n_batch, n_tiles1):
    o_ref = rest[-1]
    i = pl.program_id(0)
    r = _mod_row(i, tm, rows_per_batch, n_batch)
    d = o_ref.shape[1]
    for c0 in range(0, d, PROJ_CHUNK):
        cs = slice(c0, min(c0 + PROJ_CHUNK, d))
        y = (jnp.dot(a1_ref[...], w1_ref[:, cs], preferred_element_type=F32)
             + jnp.dot(a2_ref[...], w2_ref[:, cs], preferred_element_type=F32))
        x = x_ref[:, cs] if n_tiles1 is None else jnp.where(i >= n_tiles1, rest[0][:, cs], x_ref[:, cs])
        o_ref[:, cs] = x + gt_ref[pl.ds(r, 1), cs] * y


def _outproj(a1, a2, a2_col_block, w, x_src, x_src2, n_rows, mod, layer, *, tm, rows_per_batch, n_batch):
    d = x_src.shape[1]
    kh = w.shape[0] // 2
    x_specs, n_tiles1 = _stream_specs(x_src, x_src2, tm)
    kern = functools.partial(_outproj_kernel, tm=tm, rows_per_batch=rows_per_batch, n_batch=n_batch,
                             n_tiles1=n_tiles1)
    return pl.pallas_call(
        kern,
        out_shape=jax.ShapeDtypeStruct((n_rows, d), F32),
        grid=(n_rows // tm,),
        in_specs=[pl.BlockSpec((tm, kh), lambda i: (i, 0)),
                  pl.BlockSpec((tm, kh), lambda i: (i, a2_col_block)),
                  pl.BlockSpec((kh, d), lambda i: (0, 0)),
                  pl.BlockSpec((kh, d), lambda i: (1, 0)),
                  _mod_spec(d, layer, 5, 1)] + x_specs,
        out_specs=pl.BlockSpec((tm, d), lambda i: (i, 0)),
        compiler_params=_cparams(1),
        name="mixer_outproj",
    )(a1, a2, w, w, mod, x_src, *([] if x_src2 is None else [x_src2]))


def _rope_tables(seq_len):
    half = HEAD_DIM // 2
    nf = half // 2
    inv = ROPE_THETA ** (-jnp.arange(nf, dtype=F32) / nf)
    pos = jnp.arange(seq_len, dtype=jnp.int32)
    ang_r = (pos // GRID_W).astype(F32)[:, None] * inv[None, :]
    ang_c = (pos % GRID_W).astype(F32)[:, None] * inv[None, :]
    cr, sr, cc, sc = jnp.cos(ang_r), jnp.sin(ang_r), jnp.cos(ang_c), jnp.sin(ang_c)
    return (jnp.concatenate([cr, cr, cc, cc], axis=-1),
            jnp.concatenate([-sr, sr, -sc, sc], axis=-1))


def _gelu_tanh(x):
    return x * (0.5 * (1.0 + jnp.tanh(np.float32(np.sqrt(2.0 / np.pi)) * (x + 0.044715 * (x * x * x)))))


def _lru_kernel(*refs, tt, reverse, final, n_lat_chunks):
    if final:
        (u_ref, wa_ref, ba_ref, wx_ref, bx_ref, lam_ref, hf_ref, gl_ref, o_ref, a_scr, b_scr, carry_scr) = refs
    else:
        (xp_ref, xc_ref, xn_ref, cw_ref, cb_ref, wa_ref, ba_ref, wx_ref, bx_ref, lam_ref,
         o_ref, u_ref, xs_scr, a_scr, b_scr, carry_scr) = refs
    s = pl.program_id(1)
    is_ctx = s == 0

    @pl.when(is_ctx)
    def _():
        carry_scr[...] = jnp.zeros_like(carry_scr)

    width = a_scr.shape[1]
    halo = SUBLANES
    left = CONV_W // 2
    if not final:
        j = (n_lat_chunks - s) if reverse else (s - 1)
        has_prev = jnp.logical_and(jnp.logical_not(is_ctx), j > 0)
        has_next = jnp.logical_and(jnp.logical_not(is_ctx), j < n_lat_chunks - 1)
        xs_scr[0:halo, :] = jnp.where(has_prev, xp_ref[tt - halo:tt, :], 0.0)
        xs_scr[halo:halo + tt, :] = xc_ref[...]
        xs_scr[halo + tt:2 * halo + tt, :] = jnp.where(has_next, xn_ref[0:halo, :], 0.0)

    z = -lam_ref[...]
    softplus = jnp.maximum(z, 0.0) + jnp.log1p(jnp.exp(-jnp.abs(z)))
    for n in range(width // HEAD_DIM):
        cs = slice(n * HEAD_DIM, (n + 1) * HEAD_DIM)
        if final:
            u = u_ref[:, cs]
        else:
            u = cb_ref[:, cs]
            xa = xs_scr[:, cs]
            n_rows = xa.shape[0]
            for k in range(CONV_W):
                xk = xa if k == left else pltpu.roll(xa, (left - k) % n_rows, 0)
                u = u + xk[halo:halo + tt, :] * cw_ref[k:k + 1, cs]
            u_ref[:, cs] = u
        ub = u.astype(BF16)
        gate_r = jnp.dot(ub, wa_ref[n], preferred_element_type=F32) + ba_ref[:, cs]
        gate_i = jnp.dot(ub, wx_ref[n], preferred_element_type=F32) + bx_ref[:, cs]
        log_a = (-LRU_C) * _sigmoid_tanh(gate_r) * softplus[:, cs]
        a = jnp.exp(log_a)
        a_scr[:, cs] = a
        b_scr[:, cs] = jnp.sqrt(-jnp.tanh(log_a) * (1.0 + a * a)) * (_sigmoid_tanh(gate_i) * u)

    n_groups = tt // SUBLANES
    row = lax.broadcasted_iota(jnp.int32, (SUBLANES, width), 0)

    def group(gi, carry):
        g = (n_groups - 1 - gi) if reverse else gi
        r0 = pl.multiple_of(g * SUBLANES, SUBLANES)
        a = a_scr[pl.ds(r0, SUBLANES), :]
        b = b_scr[pl.ds(r0, SUBLANES), :]
        for k in (1, 2, 4):
            if reverse:
                keep = row < SUBLANES - k
                shift = SUBLANES - k
            else:
                keep = row >= k
                shift = k
            a_sh = jnp.where(keep, pltpu.roll(a, shift, 0), 1.0)
            b_sh = jnp.where(keep, pltpu.roll(b, shift, 0), 0.0)
            b = a * b_sh + b
            a = a * a_sh
        h = b + a * carry
        b_scr[pl.ds(r0, SUBLANES), :] = h
        last = h[0:1, :] if reverse else h[SUBLANES - 1:SUBLANES, :]
        return jnp.broadcast_to(last, (SUBLANES, width))

    carry_scr[...] = lax.fori_loop(0, n_groups, group, carry_scr[...])

    if final:
        o_ref[...] = ((hf_ref[...] + b_scr[...]) * _gelu_tanh(gl_ref[...])).astype(o_ref.dtype)
    else:
        o_ref[...] = b_scr[...]


def _lru_pass(p, conv_w, conv_b, w_a, b_a, w_x, b_x, lam, first, *, reverse, n_batch, seq_len, ctx_len, tt):
    ta = p.shape[0]
    width = conv_w.shape[1]
    assert ctx_len == tt and seq_len % tt == 0
    n_lat_chunks = seq_len // tt
    ctx_block0 = n_batch * n_lat_chunks
    final = first is not None

    def chunk_index(b, s, off):
        j = (n_lat_chunks - s) if reverse else (s - 1)
        j = jnp.clip(j + off, 0, n_lat_chunks - 1)
        return jnp.where(s == 0, ctx_block0 + b, b * n_lat_chunks + j)

    def xspec(off):
        return pl.BlockSpec((tt, width), lambda b, s: (chunk_index(b, s, off), 0))

    row_spec = pl.BlockSpec((1, width), lambda b, s: (0, 0))
    w_spec = pl.BlockSpec(w_a.shape, lambda b, s: (0, 0, 0))
    gate_specs = [w_spec, row_spec, w_spec, row_spec, row_spec]
    gate_args = [w_a, b_a.reshape(1, width), w_x, b_x.reshape(1, width), lam.reshape(1, width)]
    scan_scratch = [pltpu.VMEM((tt, width), F32), pltpu.VMEM((tt, width), F32), pltpu.VMEM((SUBLANES, width), F32)]
    if final:
        h_first, u = first
        in_specs = [xspec(0)] + gate_specs + [xspec(0),
                                              pl.BlockSpec((tt, width), lambda b, s: (chunk_index(b, s, 0), 1))]
        args = [u] + gate_args + [h_first, p]
        out_shape, out_specs, scratch = jax.ShapeDtypeStruct((ta, width), BF16), xspec(0), scan_scratch
    else:
        in_specs = [xspec(-1), xspec(0), xspec(1),
                    pl.BlockSpec((CONV_W, width), lambda b, s: (0, 0)), row_spec] + gate_specs
        args = [p, p, p, conv_w, conv_b.reshape(1, width)] + gate_args
        out_shape = [jax.ShapeDtypeStruct((ta, width), F32)] * 2
        out_specs = [xspec(0), xspec(0)]
        scratch = [pltpu.VMEM((tt + 2 * SUBLANES, width), F32)] + scan_scratch
    kern = functools.partial(_lru_kernel, tt=tt, reverse=reverse, final=final, n_lat_chunks=n_lat_chunks)
    return pl.pallas_call(
        kern,
        out_shape=out_shape,
        grid=(n_batch, n_lat_chunks + 1),
        in_specs=in_specs,
        out_specs=out_specs,
        scratch_shapes=scratch,
        compiler_params=_cparams(2),
        name="rglru_rev" if reverse else "rglru_fwd",
    )(*args)


LOG2E = float(np.log2(np.e))
LOGIT_SCALE = ATTN_SCALE * LOG2E


def _softmax_parts(s2, extra_logit2=None):
    m = jnp.max(s2, axis=-1, keepdims=True)
    if extra_logit2 is not None:
        m = jnp.maximum(m, extra_logit2)
    e = jnp.exp2(s2 - m)
    denom = jnp.sum(e, axis=-1, keepdims=True)
    if extra_logit2 is not None:
        denom = denom + jnp.exp2(extra_logit2 - m)
    return e, 1.0 / denom


def _swa_kernel(sink_ref, q_ref, kp_ref, kc_ref, kn_ref, kx_ref, vp_ref, vc_ref, vn_ref, vx_ref, mask_ref, o_ref,
                *, n_kv_heads):
    blk = q_ref.shape[0]
    head = lax.broadcasted_iota(jnp.int32, (GQA_GROUP * blk, 1), 0) // blk
    logits = []
    for kh in range(n_kv_heads):
        ks = slice(kh * HEAD_DIM, (kh + 1) * HEAD_DIM)
        q0 = kh * GQA_GROUP
        q = jnp.concatenate([q_ref[:, (q0 + h) * HEAD_DIM:(q0 + h + 1) * HEAD_DIM] for h in range(GQA_GROUP)],
                            axis=0)
        k = jnp.concatenate([kp_ref[:, ks], kc_ref[:, ks], kn_ref[:, ks], kx_ref[:, ks]], axis=0)
        logits.append(lax.dot_general(q, k, (((1,), (1,)), ((), ())), preferred_element_type=F32))
    for kh in range(n_kv_heads):
        ks = slice(kh * HEAD_DIM, (kh + 1) * HEAD_DIM)
        q0 = kh * GQA_GROUP
        v = jnp.concatenate([vp_ref[:, ks], vc_ref[:, ks], vn_ref[:, ks], vx_ref[:, ks]], axis=0)
        s = logits[kh] * LOGIT_SCALE + mask_ref[...]
        sink = jnp.zeros((GQA_GROUP * blk, 1), F32)
        for h in range(GQA_GROUP):
            sink = jnp.where(head == h, sink_ref[q0 + h] * LOG2E, sink)
        e, inv = _softmax_parts(s, sink)
        o = jnp.dot(e.astype(BF16), v, preferred_element_type=F32) * inv
        for h in range(GQA_GROUP):
            o_ref[:, (q0 + h) * HEAD_DIM:(q0 + h + 1) * HEAD_DIM] = o[h * blk:(h + 1) * blk, :].astype(o_ref.dtype)


def _swa_mask_table(blk, ctx_len):
    shape = (4, GQA_GROUP * blk, 3 * blk + ctx_len)
    cls = lax.broadcasted_iota(jnp.int32, shape, 0)
    qi = lax.broadcasted_iota(jnp.int32, shape, 1) % blk
    col = lax.broadcasted_iota(jnp.int32, shape, 2)
    lo = jnp.where(cls == 0, blk, 0)
    hi = jnp.where(cls == 3, 0, jnp.where(cls == 2, 2 * blk, 3 * blk))
    rel = col - blk - qi
    ok = ((col >= lo) & (col < hi) & (rel >= -WINDOW) & (rel <= WINDOW)) | (col >= 3 * blk)
    return jnp.where(ok, 0.0, NEG).astype(F32)


def _swa(qkv, sink, *, n_batch, seq_len, ctx_len, n_kv_heads, blk):
    ta = qkv.shape[0]
    n_heads = n_kv_heads * GQA_GROUP
    n_lat_blocks = seq_len // blk
    n_ctx_blocks = ctx_len // blk
    assert blk == WINDOW and seq_len % blk == 0 and ctx_len % blk == 0 and n_lat_blocks >= 2
    ctx_q0 = n_batch * n_lat_blocks
    ctx_kv0 = n_batch * seq_len // ctx_len
    kv_w = n_kv_heads * HEAD_DIM
    k_col, v_col = n_heads * HEAD_DIM // kv_w, n_heads * HEAD_DIM // kv_w + 1

    def q_index(b, n):
        return (jnp.where(n < n_lat_blocks, b * n_lat_blocks + n, ctx_q0 + b * n_ctx_blocks + n - n_lat_blocks), 0)

    def kv_spec(off, col):
        return pl.BlockSpec((blk, kv_w),
                            lambda b, n: (b * n_lat_blocks + jnp.clip(n + off, 0, n_lat_blocks - 1), col))

    def ctx_spec(col):
        return pl.BlockSpec((ctx_len, kv_w), lambda b, n: (ctx_kv0 + b, col))

    def mask_index(b, n):
        interior = jnp.where(n == 0, 0, jnp.where(n == n_lat_blocks - 1, 2, 1))
        return (jnp.where(n < n_lat_blocks, interior, 3), 0, 0)

    mask = _swa_mask_table(blk, ctx_len)
    kern = functools.partial(_swa_kernel, n_kv_heads=n_kv_heads)
    return pl.pallas_call(
        kern,
        out_shape=jax.ShapeDtypeStruct((ta, n_heads * HEAD_DIM), BF16),
        grid=(n_batch, n_lat_blocks + n_ctx_blocks),
        in_specs=[pl.BlockSpec(memory_space=pltpu.SMEM),
                  pl.BlockSpec((blk, n_heads * HEAD_DIM), q_index),
                  kv_spec(-1, k_col), kv_spec(0, k_col), kv_spec(1, k_col), ctx_spec(k_col),
                  kv_spec(-1, v_col), kv_spec(0, v_col), kv_spec(1, v_col), ctx_spec(v_col),
                  pl.BlockSpec((None,) + mask.shape[1:], mask_index)],
        out_specs=pl.BlockSpec((blk, n_heads * HEAD_DIM), q_index),
        compiler_params=_cparams(2),
        name="windowed_gqa",
    )(sink, *([qkv] * 9), mask)


NA_MAX_HEADS_PER_STEP = 16


def _na_heads_per_step(n_heads):
    return min(NA_MAX_HEADS_PER_STEP, n_heads)


def _na_kernel(q_ref, kvp_ref, kvc_ref, kvn_ref, kvx_ref, bias_ref, o_ref):
    n_local = bias_ref.shape[-1]
    gw = q_ref.shape[1]
    n = gw // HEAD_DIM
    kv_refs = (kvp_ref, kvc_ref, kvn_ref, kvx_ref)
    heads = [slice(h * HEAD_DIM, (h + 1) * HEAD_DIM) for h in range(n)]
    def qk(h):
        k = jnp.concatenate([ref[:, heads[h]] for ref in kv_refs], axis=0)
        return lax.dot_general(q_ref[:, heads[h]], k, (((1,), (1,)), ((), ())), preferred_element_type=F32)

    def softmax(h, logit):
        s = logit * LOGIT_SCALE
        s = jnp.concatenate([s[:, :n_local] + bias_ref[h], s[:, n_local:]], axis=1)
        e, inv = _softmax_parts(s)
        return e.astype(BF16), inv

    def pv(h, e, inv):
        vs = slice(gw + heads[h].start, gw + heads[h].stop)
        v = jnp.concatenate([ref[:, vs] for ref in kv_refs], axis=0)
        o_ref[:, heads[h]] = (jnp.dot(e, v, preferred_element_type=F32) * inv).astype(o_ref.dtype)

    logits = {0: qk(0)}
    weights = {}
    for h in range(n):
        if h + 1 < n:
            logits[h + 1] = qk(h + 1)
        weights[h] = softmax(h, logits.pop(h))
        if h >= 1:
            pv(h - 1, *weights.pop(h - 1))
    pv(n - 1, *weights.pop(n - 1))


def _na_bias_table(rpb, n_grid_rows):
    r_in, w = NA_ROWS_PER_BLOCK, GRID_W
    kh = min(NA_KH, n_grid_rows)
    n_blocks = n_grid_rows // r_in
    n_ro, n_co = 2 * NA_KH - 1, 2 * NA_KW - 1
    qc, kc = np.arange(w)[:, None], np.arange(w)[None, :]
    e_col = (kc - qc + NA_KW - 1 == np.arange(n_co)[:, None, None]).astype(np.float32)
    win_start = np.clip(qc - NA_KW // 2, 0, w - NA_KW)
    col_ok = (kc >= win_start) & (kc < win_start + NA_KW)
    ri, krj = np.arange(r_in)[:, None], np.arange(3 * r_in)[None, :]
    row_ok = []
    for jb in (0, max(n_blocks // 2, 1) if n_blocks > 2 else 0, n_blocks - 1):
        r, kr = r_in * jb + ri, r_in * (jb - 1) + krj
        rs = np.clip(r - kh // 2, 0, n_grid_rows - kh)
        row_ok.append((kr >= rs) & (kr < rs + kh))
    ok = np.stack(row_ok)[:, :, None, :, None] & col_ok[None, None, :, None, :]
    ok = ok.reshape(3, r_in * w, 3 * r_in * w)
    tz = jnp.einsum('hab,bqc->haqc', rpb.astype(F32), e_col, precision=lax.Precision.HIGHEST) * LOG2E
    assert r_in - 1 <= NA_KH - 1 - r_in and 3 * r_in - 1 - r_in + NA_KH - 1 < n_ro
    full = jnp.concatenate(
        [jnp.concatenate([tz[:, k - i - r_in + NA_KH - 1] for k in range(3 * r_in)], axis=-1) for i in range(r_in)],
        axis=-2)
    return jnp.where(ok[None], full[:, None], NEG)


def _na(qkv, bias, *, n_batch, seq_len, ctx_len, n_heads):
    tq = NA_ROWS_PER_BLOCK * GRID_W
    n_blocks = seq_len // tq
    assert ctx_len == tq and seq_len % tq == 0 and NA_KH == 2 * NA_ROWS_PER_BLOCK and n_blocks >= 2
    hps = _na_heads_per_step(n_heads)
    assert n_heads % hps == 0
    n_groups = n_heads // hps
    gw = hps * HEAD_DIM
    ctx0 = n_batch * n_blocks

    def kv_spec(off):
        return pl.BlockSpec((tq, 2 * gw),
                            lambda b, g, jb: (b * n_blocks + jnp.clip(jb + off, 0, n_blocks - 1), g))

    def bias_index(b, g, jb):
        return (g, jnp.where(jb == 0, 0, jnp.where(jb == n_blocks - 1, 2, 1)), 0, 0)

    return pl.pallas_call(
        _na_kernel,
        out_shape=jax.ShapeDtypeStruct((n_batch * seq_len, n_heads * HEAD_DIM), BF16),
        grid=(n_batch, n_groups, n_blocks),
        in_specs=[pl.BlockSpec((tq, gw), lambda b, g, jb: (b * n_blocks + jb, 2 * n_groups + g)),
                  kv_spec(-1), kv_spec(0), kv_spec(1),
                  pl.BlockSpec((ctx_len, 2 * gw), lambda b, g, jb: (ctx0 + b, g)),
                  pl.BlockSpec((hps, None) + bias.shape[2:], bias_index)],
        out_specs=pl.BlockSpec((tq, gw), lambda b, g, jb: (b * n_blocks + jb, g)),
        compiler_params=_cparams(3),
        name="neighbourhood_attn",
    )(*([qkv] * 5), bias)


def _tiles():
    return dict(tm=1024, tm_proj=512, tm_out=512, tf=512, tn_mod=1024)


def _chunked(total, out_idx, out_col0=0, w_col0=0):
    return [(w_col0 + c, min(PROJ_CHUNK, total - c), out_idx, out_col0 + c, None)
            for c in range(0, total, PROJ_CHUNK)]


def _head_chunks(head_ops, w_col0, out_idx, out_col0):
    per = PROJ_CHUNK // HEAD_DIM
    return [(w_col0 + h0 * HEAD_DIM, len(head_ops[h0:h0 + per]) * HEAD_DIM, out_idx, out_col0 + h0 * HEAD_DIM,
             tuple(head_ops[h0:h0 + per])) for h0 in range(0, len(head_ops), per)]


def kernel(x, c, ctx, c_ctx, w_mod, b_mod, norm_g, ffn_w_in, ffn_w_out, ab_w_in, lru_conv_w, lru_conv_b,
           lru_w_a, lru_b_a, lru_w_x, lru_b_x, lru_lambda, attn_q_norm, attn_k_norm, attn_sink, ab_w_out,
           na_w_in, na_q_norm, na_k_norm, na_rpb, na_w_out):
    n_batch, seq_len, d = x.shape
    ctx_len = ctx.shape[1]
    depth = w_mod.shape[0]
    lru_w = lru_conv_w.shape[2]
    n_kv = attn_sink.shape[1] // GQA_GROUP
    n_att = n_kv * GQA_GROUP
    na_heads = na_rpb.shape[1]
    assert n_batch < MOD_ROWS and depth == 2
    t = _tiles()
    tl, tc = n_batch * seq_len, n_batch * ctx_len
    geom = dict(rows_per_batch=seq_len, n_batch=n_batch)

    cc = jnp.zeros((MOD_ROWS, d), F32).at[:n_batch].set(c).at[n_batch].set(c_ctx)
    mod = _mod_table(cc, w_mod, b_mod, t["tn_mod"])
    wg, wu, wo = _cast_ffn_weights(ffn_w_in, ffn_w_out)

    def ffn(x_src, n_rows, layer, which, row0=0):
        return _ffn(x_src, n_rows, row0, mod, layer, which, norm_g[layer, 2 * which], wg, wu, wo,
                    tm=t["tm"], tf=t["tf"], **geom)

    cos_t, sin_t = _rope_tables(seq_len)
    proj_geom = dict(tm=t["tm_proj"], n_lat_rows=tl, **geom)

    x_lat = ffn(x.reshape(tl, d), tl, 0, 0)
    x_ctx = ffn(ctx.reshape(tc, d), tc, 0, 0, row0=tl)
    gains = jnp.stack([attn_q_norm[0], attn_k_norm[0]])
    head_ops = [(0, True)] * n_att + [(1, True)] * n_kv + [None] * n_kv
    plan = _head_chunks(head_ops, 2 * lru_w, 1, 0) + _chunked(2 * lru_w, 0)
    p, qkv = _inproj(x_lat, x_ctx, mod, 0, norm_g[0, 1], ab_w_in[0].astype(BF16), gains, cos_t, sin_t, plan,
                     ((2 * lru_w, F32), (len(head_ops) * HEAD_DIM, BF16)), **proj_geom)
    lru_args = lambda dr: (lru_conv_w[0], lru_conv_b[0], lru_w_a[0, dr].astype(BF16), lru_b_a[0, dr],
                           lru_w_x[0, dr].astype(BF16), lru_b_x[0, dr], lru_lambda[0, dr])
    scan_geom = dict(n_batch=n_batch, seq_len=seq_len, ctx_len=ctx_len, tt=ctx_len)
    h_fwd = _lru_pass(p, *lru_args(0), None, reverse=False, **scan_geom)
    lru = _lru_pass(p, *lru_args(1), h_fwd, reverse=True, **scan_geom)
    att = _swa(qkv, attn_sink[0], n_batch=n_batch, seq_len=seq_len, ctx_len=ctx_len, n_kv_heads=n_kv,
               blk=WINDOW)
    x_all = _outproj(lru, att, 0, ab_w_out[0].astype(BF16), x_lat, x_ctx, tl + tc, mod, 0, tm=t["tm_out"], **geom)
    x_all = ffn(x_all, tl + tc, 0, 1)

    x_all = ffn(x_all, tl + tc, 1, 0)
    gains = jnp.stack([na_q_norm[0], na_k_norm[0]])
    na_d = na_heads * HEAD_DIM
    hps = _na_heads_per_step(na_heads)
    gw = hps * HEAD_DIM
    groups = range(0, na_d, gw)
    plan = sum([_head_chunks([(0, False)] * hps, g0, 0, 2 * na_d + g0) for g0 in groups], [])
    plan += sum([_head_chunks([(1, False)] * hps, na_d + g0, 0, 2 * g0) for g0 in groups], [])
    plan += sum([_head_chunks([None] * hps, 2 * na_d + g0, 0, 2 * g0 + gw) for g0 in groups], [])
    (qkv,) = _inproj(x_all, None, mod, 1, norm_g[1, 1], na_w_in[0].astype(BF16), gains, cos_t, sin_t, plan,
                     ((3 * na_d, BF16),), **proj_geom)
    o = _na(qkv, _na_bias_table(na_rpb[0], seq_len // GRID_W), n_batch=n_batch, seq_len=seq_len, ctx_len=ctx_len,
            n_heads=na_heads)
    x_lat = _outproj(o, o, 1, na_w_out[0].astype(BF16), x_all, None, tl, mod, 1, tm=t["tm_out"], **geom)
    x_lat = ffn(x_lat, tl, 1, 1)
    return x_lat.reshape(n_batch, seq_len, d)
```

```python
import functools

import numpy as np
import jax
import jax.numpy as jnp
from jax import lax
from jax.experimental import pallas as pl
from jax.experimental.pallas import tpu as pltpu

HEAD_DIM = 128
EPS = 1e-6
N_MOD = 9
GRID_W = 64
CONV_W = 4
LRU_C = 8.0
WINDOW = 128
ROPE_THETA = 10000.0
NA_KH = 8
NA_KW = 16
ATTN_SCALE = HEAD_DIM ** -0.5
GQA_GROUP = 4

LANES = 128
SUBLANES = 8
BF16_SUBLANES = 2 * SUBLANES
CAST_IN_MAX_ROWS = 256
CAST_OUT_MAX_ROWS = 1024
VMEM_LIMIT_BYTES = 56 * 1024 * 1024
MOD_ROWS = SUBLANES

NEG = -1e30
NA_ROWS_PER_BLOCK = 4

BF16 = jnp.bfloat16
F32 = jnp.float32


def _cparams(n_axes):
    return pltpu.CompilerParams(dimension_semantics=("arbitrary",) * n_axes,
                                vmem_limit_bytes=VMEM_LIMIT_BYTES)


def _sigmoid(x):
    return 1.0 / (1.0 + jnp.exp(-x))


def _sigmoid_tanh(x):
    return 0.5 * jnp.tanh(0.5 * x) + 0.5


NORM_CHUNK_ROWS = 16


def _norm_modulate_into(h_scr, x_ref, g, shift, scale, alt=None):
    gain = g * (1.0 + scale)
    n_chunks = x_ref.shape[0] // NORM_CHUNK_ROWS

    def load(c):
        r0 = pl.multiple_of(c * NORM_CHUNK_ROWS, NORM_CHUNK_ROWS)
        x = x_ref[pl.ds(r0, NORM_CHUNK_ROWS), :]
        if alt is not None:
            x = jnp.where(alt[0], alt[1][pl.ds(r0, NORM_CHUNK_ROWS), :], x)
        return r0, x

    def inv_rms(c):
        _, x = load(c)
        return lax.rsqrt(jnp.mean(x * x, axis=-1, keepdims=True) + EPS)

    def emit(c, inv):
        r0, x = load(c)
        h_scr[pl.ds(r0, NORM_CHUNK_ROWS), :] = ((x * inv) * gain + shift).astype(h_scr.dtype)

    def step(c, inv_prev):
        inv = inv_rms(c)
        emit(c - 1, inv_prev)
        return inv

    emit(n_chunks - 1, lax.fori_loop(1, n_chunks, step, inv_rms(0), unroll=8))


def _mod_kernel(c_ref, w_ref, b_ref, o_ref):
    c = c_ref[...]
    s = c * _sigmoid(c)
    o_ref[...] = jnp.dot(s.astype(BF16), w_ref[...].astype(BF16),
                         preferred_element_type=F32) + b_ref[...]


def _mod_table(cc, w_mod, b_mod, tn):
    depth, d, n = w_mod.shape
    return pl.pallas_call(
        _mod_kernel,
        out_shape=jax.ShapeDtypeStruct((depth, MOD_ROWS, n), F32),
        grid=(depth, n // tn),
        in_specs=[pl.BlockSpec((MOD_ROWS, d), lambda l, j: (0, 0)),
                  pl.BlockSpec((None, d, tn), lambda l, j: (l, 0, j)),
                  pl.BlockSpec((None, 1, tn), lambda l, j: (l, 0, j))],
        out_specs=pl.BlockSpec((None, MOD_ROWS, tn), lambda l, j: (l, 0, j)),
        compiler_params=_cparams(2),
        name="mod_table",
    )(cc, w_mod, b_mod.reshape(depth, 1, n))


def _mod_spec(d, layer, k, n_grid_axes):
    if n_grid_axes == 1:
        return pl.BlockSpec((None, MOD_ROWS, d), lambda i: (layer, 0, k))
    return pl.BlockSpec((None, MOD_ROWS, d), lambda i, j: (layer, 0, k))


def _mod_row(tile_idx, tm, rows_per_batch, n_batch):
    return jnp.minimum(tile_idx * tm // rows_per_batch, n_batch)


def _ffn_kernel(x_ref, g_ref, sh_ref, sc_ref, gt_ref, wg_ref, wu_ref, wo_ref, o_ref, h_scr,
                *, tm, tile0, rows_per_batch, n_batch, last_width):
    i, j = pl.program_id(0), pl.program_id(1)
    last = pl.num_programs(1) - 1
    r = _mod_row(i + tile0, tm, rows_per_batch, n_batch)

    def hidden_block(width, first):
        h = h_scr[...]
        a = jnp.dot(h, wg_ref[:, :width], preferred_element_type=F32)
        u = jnp.dot(h, wu_ref[:, :width], preferred_element_type=F32)
        act = (a * _sigmoid(a)) * u
        y = jnp.dot(act.astype(BF16), wo_ref[:width, :], preferred_element_type=F32)
        o_ref[...] = y if first else o_ref[...] + y

    @pl.when(j == 0)
    def _():
        _norm_modulate_into(h_scr, x_ref, g_ref[...], sh_ref[pl.ds(r, 1), :], sc_ref[pl.ds(r, 1), :])
        hidden_block(wg_ref.shape[1], True)

    pl.when(jnp.logical_and(j > 0, j < last))(lambda: hidden_block(wg_ref.shape[1], False))

    @pl.when(j == last)
    def _():
        hidden_block(last_width, False)
        o_ref[...] = x_ref[...] + 0.5 * gt_ref[pl.ds(r, 1), :] * o_ref[...]


def _ffn(x_src, n_rows, row0, mod, layer, which, g, wg, wu, wo, *, tm, tf, rows_per_batch, n_batch):
    d = x_src.shape[1]
    d_ff = wo.shape[2]
    nf = pl.cdiv(d_ff, tf)
    last_width = d_ff - (nf - 1) * tf
    assert last_width % LANES == 0 and row0 % tm == 0 and nf >= 2
    k0 = 6 * which
    kern = functools.partial(_ffn_kernel, tm=tm, tile0=row0 // tm, rows_per_batch=rows_per_batch,
                             n_batch=n_batch, last_width=last_width)
    return pl.pallas_call(
        kern,
        out_shape=jax.ShapeDtypeStruct((n_rows, d), F32),
        grid=(n_rows // tm, nf),
        in_specs=[pl.BlockSpec((tm, d), lambda i, j: (i, 0)),
                  pl.BlockSpec((1, d), lambda i, j: (0, 0)),
                  _mod_spec(d, layer, k0, 2), _mod_spec(d, layer, k0 + 1, 2), _mod_spec(d, layer, k0 + 2, 2),
                  pl.BlockSpec((None, None, d, tf), lambda i, j: (layer, which, 0, j)),
                  pl.BlockSpec((None, None, d, tf), lambda i, j: (layer, which, 0, j)),
                  pl.BlockSpec((None, None, tf, d), lambda i, j: (layer, which, j, 0))],
        out_specs=pl.BlockSpec((tm, d), lambda i, j: (i, 0)),
        scratch_shapes=[pltpu.VMEM((tm, d), BF16)],
        compiler_params=_cparams(2),
        name="half_ffn",
    )(x_src, g.reshape(1, d), mod, mod, mod, wg, wu, wo)


def _cast_kernel(*refs):
    n = len(refs) // 2
    for src, dst in zip(refs[:n], refs[n:]):
        dst[...] = src[...].astype(dst.dtype)


def _largest_tile(n, multiple, cap):
    return max(t for t in range(multiple, min(n, cap) + 1, multiple) if n % t == 0)


def _cast_ffn_weights(w_in, w_out):
    depth, n_ffn, d, f2 = w_in.shape
    f = f2 // 2
    assert f % LANES == 0
    rows = _largest_tile(d, BF16_SUBLANES, CAST_IN_MAX_ROWS)
    half = (None, None, rows, f)
    wg, wu = pl.pallas_call(
        _cast_kernel,
        out_shape=[jax.ShapeDtypeStruct((depth, n_ffn, d, f), BF16)] * 2,
        grid=(depth, n_ffn, d // rows),
        in_specs=[pl.BlockSpec(half, lambda l, w, r: (l, w, r, 0)), pl.BlockSpec(half, lambda l, w, r: (l, w, r, 1))],
        out_specs=[pl.BlockSpec(half, lambda l, w, r: (l, w, r, 0))] * 2,
        compiler_params=_cparams(3),
        name="cast_ffn_in",
    )(w_in, w_in)
    rows = _largest_tile(f, BF16_SUBLANES, CAST_OUT_MAX_ROWS)
    blk = (None, None, rows, d)
    wo = pl.pallas_call(
        _cast_kernel,
        out_shape=jax.ShapeDtypeStruct(w_out.shape, BF16),
        grid=(depth, n_ffn, f // rows),
        in_specs=[pl.BlockSpec(blk, lambda l, w, r: (l, w, r, 0))],
        out_specs=pl.BlockSpec(blk, lambda l, w, r: (l, w, r, 0)),
        compiler_params=_cparams(3),
        name="cast_ffn_out",
    )(w_out)
    return wg, wu, wo


PROJ_CHUNK = 4 * HEAD_DIM


def _swap_halves_32(y):
    lane = lax.broadcasted_iota(jnp.int32, y.shape, 1)
    return jnp.where((lane & 32) == 0, pltpu.roll(y, LANES - 32, 1), pltpu.roll(y, 32, 1))


def _inproj_kernel(x_ref, g_ref, sh_ref, sc_ref, w_ref, gains_ref, cos_ref, sin_ref, *rest,
                   plan, tm, rows_per_batch, n_batch, n_lat_tiles, n_tiles1):
    i = pl.program_id(0)
    alt = (i >= n_tiles1, rest[0]) if n_tiles1 is not None else None
    out_refs, h_scr = rest[(0 if alt is None else 1):-1], rest[-1]
    r = _mod_row(i, tm, rows_per_batch, n_batch)
    is_lat = i < n_lat_tiles
    _norm_modulate_into(h_scr, x_ref, g_ref[...], sh_ref[pl.ds(r, 1), :], sc_ref[pl.ds(r, 1), :], alt)
    for w_col, width, out_idx, out_col, ops in plan:
        y = jnp.dot(h_scr[...], w_ref[:, w_col:w_col + width], preferred_element_type=F32)
        o_ref = out_refs[out_idx]
        if ops is None:
            o_ref[:, out_col:out_col + width] = y.astype(o_ref.dtype)
            continue
        for hh, op in enumerate(ops):
            yh = y[:, hh * HEAD_DIM:(hh + 1) * HEAD_DIM]
            if op is not None:
                gain_row, rope = op
                yh = yh * lax.rsqrt(jnp.mean(yh * yh, axis=-1, keepdims=True) + EPS)
                yh = yh * gains_ref[gain_row:gain_row + 1, :]
                if rope:
                    yh = jnp.where(is_lat, yh * cos_ref[...] + _swap_halves_32(yh) * sin_ref[...], yh)
            o_ref[:, out_col + hh * HEAD_DIM:out_col + (hh + 1) * HEAD_DIM] = yh.astype(o_ref.dtype)


def _stream_specs(x_src, x_src2, tm):
    d = x_src.shape[1]
    if x_src2 is None:
        return [pl.BlockSpec((tm, d), lambda i: (i, 0))], None
    n1 = x_src.shape[0] // tm
    return [pl.BlockSpec((tm, d), lambda i: (jnp.minimum(i, n1 - 1), 0)),
            pl.BlockSpec((tm, d), lambda i: (jnp.maximum(i - n1, 0), 0))], n1


def _inproj(x_src, x_src2, mod, layer, g, w, gains, cos_t, sin_t, plan, outs, *, tm, n_lat_rows, rows_per_batch,
            n_batch):
    d = x_src.shape[1]
    ta = x_src.shape[0] + (0 if x_src2 is None else x_src2.shape[0])
    n_pos_tiles = cos_t.shape[0] // tm
    (x_spec, *x2_spec), n_tiles1 = _stream_specs(x_src, x_src2, tm)
    kern = functools.partial(_inproj_kernel, plan=tuple(plan), tm=tm, rows_per_batch=rows_per_batch,
                             n_batch=n_batch, n_lat_tiles=n_lat_rows // tm, n_tiles1=n_tiles1)
    return pl.pallas_call(
        kern,
        out_shape=[jax.ShapeDtypeStruct((ta, n), dt) for n, dt in outs],
        grid=(ta // tm,),
        in_specs=[x_spec,
                  pl.BlockSpec((1, d), lambda i: (0, 0)),
                  _mod_spec(d, layer, 3, 1), _mod_spec(d, layer, 4, 1),
                  pl.BlockSpec(w.shape, lambda i: (0, 0), pipeline_mode=pl.Buffered(1)),
                  pl.BlockSpec(gains.shape, lambda i: (0, 0)),
                  pl.BlockSpec((tm, HEAD_DIM), lambda i: (i % n_pos_tiles, 0)),
                  pl.BlockSpec((tm, HEAD_DIM), lambda i: (i % n_pos_tiles, 0))] + x2_spec,
        out_specs=[pl.BlockSpec((tm, n), lambda i: (i, 0)) for n, _ in outs],
        scratch_shapes=[pltpu.VMEM((tm, d), BF16)],
        compiler_params=_cparams(1),
        name="mixer_inproj",
    )(x_src, g.reshape(1, d), mod, mod, w, gains, cos_t, sin_t, *([] if x_src2 is None else [x_src2]))


def _outproj_kernel(a1_ref, a2_ref, w1_ref, w2_ref, gt_ref, x_ref, *rest, tm, rows_per_batch, n_batch, n_tiles1):
    o_ref = rest[-1]
    i = pl.program_id(0)
    r = _mod_row(i, tm, rows_per_batch, n_batch)
    d = o_ref.shape[1]
    for c0 in range(0, d, PROJ_CHUNK):
        cs = slice(c0, min(c0 + PROJ_CHUNK, d))
        y = (jnp.dot(a1_ref[...], w1_ref[:, cs], preferred_element_type=F32)
             + jnp.dot(a2_ref[...], w2_ref[:, cs], preferred_element_type=F32))
        x = x_ref[:, cs] if n_tiles1 is None else jnp.where(i >= n_tiles1, rest[0][:, cs], x_ref[:, cs])
        o_ref[:, cs] = x + gt_ref[pl.ds(r, 1), cs] * y


def _outproj(a1, a2, a2_col_block, w, x_src, x_src2, n_rows, mod, layer, *, tm, rows_per_batch, n_batch):
    d = x_src.shape[1]
    kh = w.shape[0] // 2
    x_specs, n_tiles1 = _stream_specs(x_src, x_src2, tm)
    kern = functools.partial(_outproj_kernel, tm=tm, rows_per_batch=rows_per_batch, n_batch=n_batch,
                             n_tiles1=n_tiles1)
    return pl.pallas_call(
        kern,
        out_shape=jax.ShapeDtypeStruct((n_rows, d), F32),
        grid=(n_rows // tm,),
        in_specs=[pl.BlockSpec((tm, kh), lambda i: (i, 0)),
                  pl.BlockSpec((tm, kh), lambda i: (i, a2_col_block)),
                  pl.BlockSpec((kh, d), lambda i: (0, 0)),
                  pl.BlockSpec((kh, d), lambda i: (1, 0)),
                  _mod_spec(d, layer, 5, 1)] + x_specs,
        out_specs=pl.BlockSpec((tm, d), lambda i: (i, 0)),
        compiler_params=_cparams(1),
        name="mixer_outproj",
    )(a1, a2, w, w, mod, x_src, *([] if x_src2 is None else [x_src2]))


def _rope_tables(seq_len):
    half = HEAD_DIM // 2
    nf = half // 2
    inv = ROPE_THETA ** (-jnp.arange(nf, dtype=F32) / nf)
    pos = jnp.arange(seq_len, dtype=jnp.int32)
    ang_r = (pos // GRID_W).astype(F32)[:, None] * inv[None, :]
    ang_c = (pos % GRID_W).astype(F32)[:, None] * inv[None, :]
    cr, sr, cc, sc = jnp.cos(ang_r), jnp.sin(ang_r), jnp.cos(ang_c), jnp.sin(ang_c)
    return (jnp.concatenate([cr, cr, cc, cc], axis=-1),
            jnp.concatenate([-sr, sr, -sc, sc], axis=-1))


def _gelu_tanh(x):
    return x * (0.5 * (1.0 + jnp.tanh(np.float32(np.sqrt(2.0 / np.pi)) * (x + 0.044715 * (x * x * x)))))


def _lru_kernel(*refs, tt, reverse, final, n_lat_chunks):
    if final:
        (u_ref, wa_ref, ba_ref, wx_ref, bx_ref, lam_ref, hf_ref, gl_ref, o_ref, a_scr, b_scr, carry_scr) = refs
    else:
        (xp_ref, xc_ref, xn_ref, cw_ref, cb_ref, wa_ref, ba_ref, wx_ref, bx_ref, lam_ref,
         o_ref, u_ref, xs_scr, a_scr, b_scr, carry_scr) = refs
    s = pl.program_id(1)
    is_ctx = s == 0

    @pl.when(is_ctx)
    def _():
        carry_scr[...] = jnp.zeros_like(carry_scr)

    width = a_scr.shape[1]
    halo = SUBLANES
    left = CONV_W // 2
    if not final:
        j = (n_lat_chunks - s) if reverse else (s - 1)
        has_prev = jnp.logical_and(jnp.logical_not(is_ctx), j > 0)
        has_next = jnp.logical_and(jnp.logical_not(is_ctx), j < n_lat_chunks - 1)
        xs_scr[0:halo, :] = jnp.where(has_prev, xp_ref[tt - halo:tt, :], 0.0)
        xs_scr[halo:halo + tt, :] = xc_ref[...]
        xs_scr[halo + tt:2 * halo + tt, :] = jnp.where(has_next, xn_ref[0:halo, :], 0.0)

    z = -lam_ref[...]
    softplus = jnp.maximum(z, 0.0) + jnp.log1p(jnp.exp(-jnp.abs(z)))
    for n in range(width // HEAD_DIM):
        cs = slice(n * HEAD_DIM, (n + 1) * HEAD_DIM)
        if final:
            u = u_ref[:, cs]
        else:
            u = cb_ref[:, cs]
            xa = xs_scr[:, cs]
            n_rows = xa.shape[0]
            for k in range(CONV_W):
                xk = xa if k == left else pltpu.roll(xa, (left - k) % n_rows, 0)
                u = u + xk[halo:halo + tt, :] * cw_ref[k:k + 1, cs]
            u_ref[:, cs] = u
        ub = u.astype(BF16)
        gate_r = jnp.dot(ub, wa_ref[n], preferred_element_type=F32) + ba_ref[:, cs]
        gate_i = jnp.dot(ub, wx_ref[n], preferred_element_type=F32) + bx_ref[:, cs]
        log_a = (-LRU_C) * _sigmoid_tanh(gate_r) * softplus[:, cs]
        a = jnp.exp(log_a)
        a_scr[:, cs] = a
        b_scr[:, cs] = jnp.sqrt(-jnp.tanh(log_a) * (1.0 + a * a)) * (_sigmoid_tanh(gate_i) * u)

    n_groups = tt // SUBLANES
    row = lax.broadcasted_iota(jnp.int32, (SUBLANES, width), 0)

    def group(gi, carry):
        g = (n_groups - 1 - gi) if reverse else gi
        r0 = pl.multiple_of(g * SUBLANES, SUBLANES)
        a = a_scr[pl.ds(r0, SUBLANES), :]
        b = b_scr[pl.ds(r0, SUBLANES), :]
        for k in (1, 2, 4):
            if reverse:
                keep = row < SUBLANES - k
                shift = SUBLANES - k
            else:
                keep = row >= k
                shift = k
            a_sh = jnp.where(keep, pltpu.roll(a, shift, 0), 1.0)
            b_sh = jnp.where(keep, pltpu.roll(b, shift, 0), 0.0)
            b = a * b_sh + b
            a = a * a_sh
        h = b + a * carry
        b_scr[pl.ds(r0, SUBLANES), :] = h
        last = h[0:1, :] if reverse else h[SUBLANES - 1:SUBLANES, :]
        return jnp.broadcast_to(last, (SUBLANES, width))

    carry_scr[...] = lax.fori_loop(0, n_groups, group, carry_scr[...])

    if final:
        o_ref[...] = ((hf_ref[...] + b_scr[...]) * _gelu_tanh(gl_ref[...])).astype(o_ref.dtype)
    else:
        o_ref[...] = b_scr[...]


def _lru_pass(p, conv_w, conv_b, w_a, b_a, w_x, b_x, lam, first, *, reverse, n_batch, seq_len, ctx_len, tt):
    ta = p.shape[0]
    width = conv_w.shape[1]
    assert ctx_len == tt and seq_len % tt == 0
    n_lat_chunks = seq_len // tt
    ctx_block0 = n_batch * n_lat_chunks
    final = first is not None

    def chunk_index(b, s, off):
        j = (n_lat_chunks - s) if reverse else (s - 1)
        j = jnp.clip(j + off, 0, n_lat_chunks - 1)
        return jnp.where(s == 0, ctx_block0 + b, b * n_lat_chunks + j)

    def xspec(off):
        return pl.BlockSpec((tt, width), lambda b, s: (chunk_index(b, s, off), 0))

    row_spec = pl.BlockSpec((1, width), lambda b, s: (0, 0))
    w_spec = pl.BlockSpec(w_a.shape, lambda b, s: (0, 0, 0))
    gate_specs = [w_spec, row_spec, w_spec, row_spec, row_spec]
    gate_args = [w_a, b_a.reshape(1, width), w_x, b_x.reshape(1, width), lam.reshape(1, width)]
    scan_scratch = [pltpu.VMEM((tt, width), F32), pltpu.VMEM((tt, width), F32), pltpu.VMEM((SUBLANES, width), F32)]
    if final:
        h_first, u = first
        in_specs = [xspec(0)] + gate_specs + [xspec(0),
                                              pl.BlockSpec((tt, width), lambda b, s: (chunk_index(b, s, 0), 1))]
        args = [u] + gate_args + [h_first, p]
        out_shape, out_specs, scratch = jax.ShapeDtypeStruct((ta, width), BF16), xspec(0), scan_scratch
    else:
        in_specs = [xspec(-1), xspec(0), xspec(1),
                    pl.BlockSpec((CONV_W, width), lambda b, s: (0, 0)), row_spec] + gate_specs
        args = [p, p, p, conv_w, conv_b.reshape(1, width)] + gate_args
        out_shape = [jax.ShapeDtypeStruct((ta, width), F32)] * 2
        out_specs = [xspec(0), xspec(0)]
        scratch = [pltpu.VMEM((tt + 2 * SUBLANES, width), F32)] + scan_scratch
    kern = functools.partial(_lru_kernel, tt=tt, reverse=reverse, final=final, n_lat_chunks=n_lat_chunks)
    return pl.pallas_call(
        kern,
        out_shape=out_shape,
        grid=(n_batch, n_lat_chunks + 1),
        in_specs=in_specs,
        out_specs=out_specs,
        scratch_shapes=scratch,
        compiler_params=_cparams(2),
        name="rglru_rev" if reverse else "rglru_fwd",
    )(*args)


LOG2E = float(np.log2(np.e))
LOGIT_SCALE = ATTN_SCALE * LOG2E


def _softmax_parts(s2, extra_logit2=None):
    m = jnp.max(s2, axis=-1, keepdims=True)
    if extra_logit2 is not None:
        m = jnp.maximum(m, extra_logit2)
    e = jnp.exp2(s2 - m)
    denom = jnp.sum(e, axis=-1, keepdims=True)
    if extra_logit2 is not None:
        denom = denom + jnp.exp2(extra_logit2 - m)
    return e, 1.0 / denom


def _swa_kernel(sink_ref, q_ref, kp_ref, kc_ref, kn_ref, kx_ref, vp_ref, vc_ref, vn_ref, vx_ref, mask_ref, o_ref,
                *, n_kv_heads):
    blk = q_ref.shape[0]
    head = lax.broadcasted_iota(jnp.int32, (GQA_GROUP * blk, 1), 0) // blk
    logits = []
    for kh in range(n_kv_heads):
        ks = slice(kh * HEAD_DIM, (kh + 1) * HEAD_DIM)
        q0 = kh * GQA_GROUP
        q = jnp.concatenate([q_ref[:, (q0 + h) * HEAD_DIM:(q0 + h + 1) * HEAD_DIM] for h in range(GQA_GROUP)],
                            axis=0)
        k = jnp.concatenate([kp_ref[:, ks], kc_ref[:, ks], kn_ref[:, ks], kx_ref[:, ks]], axis=0)
        logits.append(lax.dot_general(q, k, (((1,), (1,)), ((), ())), preferred_element_type=F32))
    for kh in range(n_kv_heads):
        ks = slice(kh * HEAD_DIM, (kh + 1) * HEAD_DIM)
        q0 = kh * GQA_GROUP
        v = jnp.concatenate([vp_ref[:, ks], vc_ref[:, ks], vn_ref[:, ks], vx_ref[:, ks]], axis=0)
        s = logits[kh] * LOGIT_SCALE + mask_ref[...]
        sink = jnp.zeros((GQA_GROUP * blk, 1), F32)
        for h in range(GQA_GROUP):
            sink = jnp.where(head == h, sink_ref[q0 + h] * LOG2E, sink)
        e, inv = _softmax_parts(s, sink)
        o = jnp.dot(e.astype(BF16), v, preferred_element_type=F32) * inv
        for h in range(GQA_GROUP):
            o_ref[:, (q0 + h) * HEAD_DIM:(q0 + h + 1) * HEAD_DIM] = o[h * blk:(h + 1) * blk, :].astype(o_ref.dtype)


def _swa_mask_table(blk, ctx_len):
    shape = (4, GQA_GROUP * blk, 3 * blk + ctx_len)
    cls = lax.broadcasted_iota(jnp.int32, shape, 0)
    qi = lax.broadcasted_iota(jnp.int32, shape, 1) % blk
    col = lax.broadcasted_iota(jnp.int32, shape, 2)
    lo = jnp.where(cls == 0, blk, 0)
    hi = jnp.where(cls == 3, 0, jnp.where(cls == 2, 2 * blk, 3 * blk))
    rel = col - blk - qi
    ok = ((col >= lo) & (col < hi) & (rel >= -WINDOW) & (rel <= WINDOW)) | (col >= 3 * blk)
    return jnp.where(ok, 0.0, NEG).astype(F32)


def _swa(qkv, sink, *, n_batch, seq_len, ctx_len, n_kv_heads, blk):
    ta = qkv.shape[0]
    n_heads = n_kv_heads * GQA_GROUP
    n_lat_blocks = seq_len // blk
    n_ctx_blocks = ctx_len // blk
    assert blk == WINDOW and seq_len % blk == 0 and ctx_len % blk == 0 and n_lat_blocks >= 2
    ctx_q0 = n_batch * n_lat_blocks
    ctx_kv0 = n_batch * seq_len // ctx_len
    kv_w = n_kv_heads * HEAD_DIM
    k_col, v_col = n_heads * HEAD_DIM // kv_w, n_heads * HEAD_DIM // kv_w + 1

    def q_index(b, n):
        return (jnp.where(n < n_lat_blocks, b * n_lat_blocks + n, ctx_q0 + b * n_ctx_blocks + n - n_lat_blocks), 0)

    def kv_spec(off, col):
        return pl.BlockSpec((blk, kv_w),
                            lambda b, n: (b * n_lat_blocks + jnp.clip(n + off, 0, n_lat_blocks - 1), col))

    def ctx_spec(col):
        return pl.BlockSpec((ctx_len, kv_w), lambda b, n: (ctx_kv0 + b, col))

    def mask_index(b, n):
        interior = jnp.where(n == 0, 0, jnp.where(n == n_lat_blocks - 1, 2, 1))
        return (jnp.where(n < n_lat_blocks, interior, 3), 0, 0)

    mask = _swa_mask_table(blk, ctx_len)
    kern = functools.partial(_swa_kernel, n_kv_heads=n_kv_heads)
    return pl.pallas_call(
        kern,
        out_shape=jax.ShapeDtypeStruct((ta, n_heads * HEAD_DIM), BF16),
        grid=(n_batch, n_lat_blocks + n_ctx_blocks),
        in_specs=[pl.BlockSpec(memory_space=pltpu.SMEM),
                  pl.BlockSpec((blk, n_heads * HEAD_DIM), q_index),
                  kv_spec(-1, k_col), kv_spec(0, k_col), kv_spec(1, k_col), ctx_spec(k_col),
                  kv_spec(-1, v_col), kv_spec(0, v_col), kv_spec(1, v_col), ctx_spec(v_col),
                  pl.BlockSpec((None,) + mask.shape[1:], mask_index)],
        out_specs=pl.BlockSpec((blk, n_heads * HEAD_DIM), q_index),
        compiler_params=_cparams(2),
        name="windowed_gqa",
    )(sink, *([qkv] * 9), mask)


NA_MAX_HEADS_PER_STEP = 16


def _na_heads_per_step(n_heads):
    return min(NA_MAX_HEADS_PER_STEP, n_heads)


def _na_kernel(q_ref, kvp_ref, kvc_ref, kvn_ref, kvx_ref, bias_ref, o_ref):
    n_local = bias_ref.shape[-1]
    gw = q_ref.shape[1]
    n = gw // HEAD_DIM
    kv_refs = (kvp_ref, kvc_ref, kvn_ref, kvx_ref)
    heads = [slice(h * HEAD_DIM, (h + 1) * HEAD_DIM) for h in range(n)]
    def qk(h):
        k = jnp.concatenate([ref[:, heads[h]] for ref in kv_refs], axis=0)
        return lax.dot_general(q_ref[:, heads[h]], k, (((1,), (1,)), ((), ())), preferred_element_type=F32)

    def softmax(h, logit):
        s = logit * LOGIT_SCALE
        s = jnp.concatenate([s[:, :n_local] + bias_ref[h], s[:, n_local:]], axis=1)
        e, inv = _softmax_parts(s)
        return e.astype(BF16), inv

    def pv(h, e, inv):
        vs = slice(gw + heads[h].start, gw + heads[h].stop)
        v = jnp.concatenate([ref[:, vs] for ref in kv_refs], axis=0)
        o_ref[:, heads[h]] = (jnp.dot(e, v, preferred_element_type=F32) * inv).astype(o_ref.dtype)

    logits = {0: qk(0)}
    weights = {}
    for h in range(n):
        if h + 1 < n:
            logits[h + 1] = qk(h + 1)
        weights[h] = softmax(h, logits.pop(h))
        if h >= 1:
            pv(h - 1, *weights.pop(h - 1))
    pv(n - 1, *weights.pop(n - 1))


def _na_bias_table(rpb, n_grid_rows):
    r_in, w = NA_ROWS_PER_BLOCK, GRID_W
    kh = min(NA_KH, n_grid_rows)
    n_blocks = n_grid_rows // r_in
    n_ro, n_co = 2 * NA_KH - 1, 2 * NA_KW - 1
    qc, kc = np.arange(w)[:, None], np.arange(w)[None, :]
    e_col = (kc - qc + NA_KW - 1 == np.arange(n_co)[:, None, None]).astype(np.float32)
    win_start = np.clip(qc - NA_KW // 2, 0, w - NA_KW)
    col_ok = (kc >= win_start) & (kc < win_start + NA_KW)
    ri, krj = np.arange(r_in)[:, None], np.arange(3 * r_in)[None, :]
    row_ok = []
    for jb in (0, max(n_blocks // 2, 1) if n_blocks > 2 else 0, n_blocks - 1):
        r, kr = r_in * jb + ri, r_in * (jb - 1) + krj
        rs = np.clip(r - kh // 2, 0, n_grid_rows - kh)
        row_ok.append((kr >= rs) & (kr < rs + kh))
    ok = np.stack(row_ok)[:, :, None, :, None] & col_ok[None, None, :, None, :]
    ok = ok.reshape(3, r_in * w, 3 * r_in * w)
    tz = jnp.einsum('hab,bqc->haqc', rpb.astype(F32), e_col, precision=lax.Precision.HIGHEST) * LOG2E
    assert r_in - 1 <= NA_KH - 1 - r_in and 3 * r_in - 1 - r_in + NA_KH - 1 < n_ro
    full = jnp.concatenate(
        [jnp.concatenate([tz[:, k - i - r_in + NA_KH - 1] for k in range(3 * r_in)], axis=-1) for i in range(r_in)],
        axis=-2)
    return jnp.where(ok[None], full[:, None], NEG)


def _na(qkv, bias, *, n_batch, seq_len, ctx_len, n_heads):
    tq = NA_ROWS_PER_BLOCK * GRID_W
    n_blocks = seq_len // tq
    assert ctx_len == tq and seq_len % tq == 0 and NA_KH == 2 * NA_ROWS_PER_BLOCK and n_blocks >= 2
    hps = _na_heads_per_step(n_heads)
    assert n_heads % hps == 0
    n_groups = n_heads // hps
    gw = hps * HEAD_DIM
    ctx0 = n_batch * n_blocks

    def kv_spec(off):
        return pl.BlockSpec((tq, 2 * gw),
                            lambda b, g, jb: (b * n_blocks + jnp.clip(jb + off, 0, n_blocks - 1), g))

    def bias_index(b, g, jb):
        return (g, jnp.where(jb == 0, 0, jnp.where(jb == n_blocks - 1, 2, 1)), 0, 0)

    return pl.pallas_call(
        _na_kernel,
        out_shape=jax.ShapeDtypeStruct((n_batch * seq_len, n_heads * HEAD_DIM), BF16),
        grid=(n_batch, n_groups, n_blocks),
        in_specs=[pl.BlockSpec((tq, gw), lambda b, g, jb: (b * n_blocks + jb, 2 * n_groups + g)),
                  kv_spec(-1), kv_spec(0), kv_spec(1),
                  pl.BlockSpec((ctx_len, 2 * gw), lambda b, g, jb: (ctx0 + b, g)),
                  pl.BlockSpec((hps, None) + bias.shape[2:], bias_index)],
        out_specs=pl.BlockSpec((tq, gw), lambda b, g, jb: (b * n_blocks + jb, g)),
        compiler_params=_cparams(3),
        name="neighbourhood_attn",
    )(*([qkv] * 5), bias)


def _tiles():
    return dict(tm=1024, tm_proj=512, tm_out=512, tf=512, tn_mod=1024)


def _chunked(total, out_idx, out_col0=0, w_col0=0):
    return [(w_col0 + c, min(PROJ_CHUNK, total - c), out_idx, out_col0 + c, None)
            for c in range(0, total, PROJ_CHUNK)]


def _head_chunks(head_ops, w_col0, out_idx, out_col0):
    per = PROJ_CHUNK // HEAD_DIM
    return [(w_col0 + h0 * HEAD_DIM, len(head_ops[h0:h0 + per]) * HEAD_DIM, out_idx, out_col0 + h0 * HEAD_DIM,
             tuple(head_ops[h0:h0 + per])) for h0 in range(0, len(head_ops), per)]


def kernel(x, c, ctx, c_ctx, w_mod, b_mod, norm_g, ffn_w_in, ffn_w_out, ab_w_in, lru_conv_w, lru_conv_b,
           lru_w_a, lru_b_a, lru_w_x, lru_b_x, lru_lambda, attn_q_norm, attn_k_norm, attn_sink, ab_w_out,
           na_w_in, na_q_norm, na_k_norm, na_rpb, na_w_out):
    n_batch, seq_len, d = x.shape
    ctx_len = ctx.shape[1]
    depth = w_mod.shape[0]
    lru_w = lru_conv_w.shape[2]
    n_kv = attn_sink.shape[1] // GQA_GROUP
    n_att = n_kv * GQA_GROUP
    na_heads = na_rpb.shape[1]
    assert n_batch < MOD_ROWS and depth == 2
    t = _tiles()
    tl, tc = n_batch * seq_len, n_batch * ctx_len
    geom = dict(rows_per_batch=seq_len, n_batch=n_batch)

    cc = jnp.zeros((MOD_ROWS, d), F32).at[:n_batch].set(c).at[n_batch].set(c_ctx)
    mod = _mod_table(cc, w_mod, b_mod, t["tn_mod"])
    wg, wu, wo = _cast_ffn_weights(ffn_w_in, ffn_w_out)

    def ffn(x_src, n_rows, layer, which, row0=0):
        return _ffn(x_src, n_rows, row0, mod, layer, which, norm_g[layer, 2 * which], wg, wu, wo,
                    tm=t["tm"], tf=t["tf"], **geom)

    cos_t, sin_t = _rope_tables(seq_len)
    proj_geom = dict(tm=t["tm_proj"], n_lat_rows=tl, **geom)

    x_lat = ffn(x.reshape(tl, d), tl, 0, 0)
    x_ctx = ffn(ctx.reshape(tc, d), tc, 0, 0, row0=tl)
    gains = jnp.stack([attn_q_norm[0], attn_k_norm[0]])
    head_ops = [(0, True)] * n_att + [(1, True)] * n_kv + [None] * n_kv
    plan = _head_chunks(head_ops, 2 * lru_w, 1, 0) + _chunked(2 * lru_w, 0)
    p, qkv = _inproj(x_lat, x_ctx, mod, 0, norm_g[0, 1], ab_w_in[0].astype(BF16), gains, cos_t, sin_t, plan,
                     ((2 * lru_w, F32), (len(head_ops) * HEAD_DIM, BF16)), **proj_geom)
    lru_args = lambda dr: (lru_conv_w[0], lru_conv_b[0], lru_w_a[0, dr].astype(BF16), lru_b_a[0, dr],
                           lru_w_x[0, dr].astype(BF16), lru_b_x[0, dr], lru_lambda[0, dr])
    scan_geom = dict(n_batch=n_batch, seq_len=seq_len, ctx_len=ctx_len, tt=ctx_len)
    h_fwd = _lru_pass(p, *lru_args(0), None, reverse=False, **scan_geom)
    lru = _lru_pass(p, *lru_args(1), h_fwd, reverse=True, **scan_geom)
    att = _swa(qkv, attn_sink[0], n_batch=n_batch, seq_len=seq_len, ctx_len=ctx_len, n_kv_heads=n_kv,
               blk=WINDOW)
    x_all = _outproj(lru, att, 0, ab_w_out[0].astype(BF16), x_lat, x_ctx, tl + tc, mod, 0, tm=t["tm_out"], **geom)
    x_all = ffn(x_all, tl + tc, 0, 1)

    x_all = ffn(x_all, tl + tc, 1, 0)
    gains = jnp.stack([na_q_norm[0], na_k_norm[0]])
    na_d = na_heads * HEAD_DIM
    hps = _na_heads_per_step(na_heads)
    gw = hps * HEAD_DIM
    groups = range(0, na_d, gw)
    plan = sum([_head_chunks([(0, False)] * hps, g0, 0, 2 * na_d + g0) for g0 in groups], [])
    plan += sum([_head_chunks([(1, False)] * hps, na_d + g0, 0, 2 * g0) for g0 in groups], [])
    plan += sum([_head_chunks([None] * hps, 2 * na_d + g0, 0, 2 * g0 + gw) for g0 in groups], [])
    (qkv,) = _inproj(x_all, None, mod, 1, norm_g[1, 1], na_w_in[0].astype(BF16), gains, cos_t, sin_t, plan,
                     ((3 * na_d, BF16),), **proj_geom)
    o = _na(qkv, _na_bias_table(na_rpb[0], seq_len // GRID_W), n_batch=n_batch, seq_len=seq_len, ctx_len=ctx_len,
            n_heads=na_heads)
    x_lat = _outproj(o, o, 1, na_w_out[0].astype(BF16), x_all, None, tl, mod, 1, tm=t["tm_out"], **geom)
    x_lat = ffn(x_lat, tl, 1, 1)
    return x_lat.reshape(n_batch, seq_len, d)
```

```python
import functools

import numpy as np
import jax
import jax.numpy as jnp
from jax import lax
from jax.experimental import pallas as pl
from jax.experimental.pallas import tpu as pltpu

HEAD_DIM = 128
EPS = 1e-6
N_MOD = 9
GRID_W = 64
CONV_W = 4
LRU_C = 8.0
WINDOW = 128
ROPE_THETA = 10000.0
NA_KH = 8
NA_KW = 16
ATTN_SCALE = HEAD_DIM ** -0.5
GQA_GROUP = 4

LANES = 128
SUBLANES = 8
BF16_SUBLANES = 2 * SUBLANES
CAST_IN_MAX_ROWS = 256
CAST_OUT_MAX_ROWS = 1024
VMEM_LIMIT_BYTES = 56 * 1024 * 1024
MOD_ROWS = SUBLANES

NEG = -1e30
NA_ROWS_PER_BLOCK = 4

BF16 = jnp.bfloat16
F32 = jnp.float32


def _cparams(n_axes):
    return pltpu.CompilerParams(dimension_semantics=("arbitrary",) * n_axes,
                                vmem_limit_bytes=VMEM_LIMIT_BYTES)


def _sigmoid(x):
    return 1.0 / (1.0 + jnp.exp(-x))


def _sigmoid_tanh(x):
    return 0.5 * jnp.tanh(0.5 * x) + 0.5


NORM_CHUNK_ROWS = 16


def _norm_modulate_into(h_scr, x_ref, g, shift, scale, alt=None):
    gain = g * (1.0 + scale)
    n_chunks = x_ref.shape[0] // NORM_CHUNK_ROWS

    def load(c):
        r0 = pl.multiple_of(c * NORM_CHUNK_ROWS, NORM_CHUNK_ROWS)
        x = x_ref[pl.ds(r0, NORM_CHUNK_ROWS), :]
        if alt is not None:
            x = jnp.where(alt[0], alt[1][pl.ds(r0, NORM_CHUNK_ROWS), :], x)
        return r0, x

    def inv_rms(c):
        _, x = load(c)
        return lax.rsqrt(jnp.mean(x * x, axis=-1, keepdims=True) + EPS)

    def emit(c, inv):
        r0, x = load(c)
        h_scr[pl.ds(r0, NORM_CHUNK_ROWS), :] = ((x * inv) * gain + shift).astype(h_scr.dtype)

    def step(c, inv_prev):
        inv = inv_rms(c)
        emit(c - 1, inv_prev)
        return inv

    emit(n_chunks - 1, lax.fori_loop(1, n_chunks, step, inv_rms(0), unroll=8))


def _mod_kernel(c_ref, w_ref, b_ref, o_ref):
    c = c_ref[...]
    s = c * _sigmoid(c)
    o_ref[...] = jnp.dot(s.astype(BF16), w_ref[...].astype(BF16),
                         preferred_element_type=F32) + b_ref[...]


def _mod_table(cc, w_mod, b_mod, tn):
    depth, d, n = w_mod.shape
    return pl.pallas_call(
        _mod_kernel,
        out_shape=jax.ShapeDtypeStruct((depth, MOD_ROWS, n), F32),
        grid=(depth, n // tn),
        in_specs=[pl.BlockSpec((MOD_ROWS, d), lambda l, j: (0, 0)),
                  pl.BlockSpec((None, d, tn), lambda l, j: (l, 0, j)),
                  pl.BlockSpec((None, 1, tn), lambda l, j: (l, 0, j))],
        out_specs=pl.BlockSpec((None, MOD_ROWS, tn), lambda l, j: (l, 0, j)),
        compiler_params=_cparams(2),
        name="mod_table",
    )(cc, w_mod, b_mod.reshape(depth, 1, n))


def _mod_spec(d, layer, k, n_grid_axes):
    if n_grid_axes == 1:
        return pl.BlockSpec((None, MOD_ROWS, d), lambda i: (layer, 0, k))
    return pl.BlockSpec((None, MOD_ROWS, d), lambda i, j: (layer, 0, k))


def _mod_row(tile_idx, tm, rows_per_batch, n_batch):
    return jnp.minimum(tile_idx * tm // rows_per_batch, n_batch)


def _ffn_kernel(x_ref, g_ref, sh_ref, sc_ref, gt_ref, wg_ref, wu_ref, wo_ref, o_ref, h_scr,
                *, tm, tile0, rows_per_batch, n_batch, last_width):
    i, j = pl.program_id(0), pl.program_id(1)
    last = pl.num_programs(1) - 1
    r = _mod_row(i + tile0, tm, rows_per_batch, n_batch)

    def hidden_block(width, first):
        h = h_scr[...]
        a = jnp.dot(h, wg_ref[:, :width], preferred_element_type=F32)
        u = jnp.dot(h, wu_ref[:, :width], preferred_element_type=F32)
        act = (a * _sigmoid(a)) * u
        y = jnp.dot(act.astype(BF16), wo_ref[:width, :], preferred_element_type=F32)
        o_ref[...] = y if first else o_ref[...] + y

    @pl.when(j == 0)
    def _():
        _norm_modulate_into(h_scr, x_ref, g_ref[...], sh_ref[pl.ds(r, 1), :], sc_ref[pl.ds(r, 1), :])
        hidden_block(wg_ref.shape[1], True)

    pl.when(jnp.logical_and(j > 0, j < last))(lambda: hidden_block(wg_ref.shape[1], False))

    @pl.when(j == last)
    def _():
        hidden_block(last_width, False)
        o_ref[...] = x_ref[...] + 0.5 * gt_ref[pl.ds(r, 1), :] * o_ref[...]


def _ffn(x_src, n_rows, row0, mod, layer, which, g, wg, wu, wo, *, tm, tf, rows_per_batch, n_batch):
    d = x_src.shape[1]
    d_ff = wo.shape[2]
    nf = pl.cdiv(d_ff, tf)
    last_width = d_ff - (nf - 1) * tf
    assert last_width % LANES == 0 and row0 % tm == 0 and nf >= 2
    k0 = 6 * which
    kern = functools.partial(_ffn_kernel, tm=tm, tile0=row0 // tm, rows_per_batch=rows_per_batch,
                             n_batch=n_batch, last_width=last_width)
    return pl.pallas_call(
        kern,
        out_shape=jax.ShapeDtypeStruct((n_rows, d), F32),
        grid=(n_rows // tm, nf),
        in_specs=[pl.BlockSpec((tm, d), lambda i, j: (i, 0)),
                  pl.BlockSpec((1, d), lambda i, j: (0, 0)),
                  _mod_spec(d, layer, k0, 2), _mod_spec(d, layer, k0 + 1, 2), _mod_spec(d, layer, k0 + 2, 2),
                  pl.BlockSpec((None, None, d, tf), lambda i, j: (layer, which, 0, j)),
                  pl.BlockSpec((None, None, d, tf), lambda i, j: (layer, which, 0, j)),
                  pl.BlockSpec((None, None, tf, d), lambda i, j: (layer, which, j, 0))],
        out_specs=pl.BlockSpec((tm, d), lambda i, j: (i, 0)),
        scratch_shapes=[pltpu.VMEM((tm, d), BF16)],
        compiler_params=_cparams(2),
        name="half_ffn",
    )(x_src, g.reshape(1, d), mod, mod, mod, wg, wu, wo)


def _cast_kernel(*refs):
    n = len(refs) // 2
    for src, dst in zip(refs[:n], refs[n:]):
        dst[...] = src[...].astype(dst.dtype)


def _largest_tile(n, multiple, cap):
    return max(t for t in range(multiple, min(n, cap) + 1, multiple) if n % t == 0)


def _cast_ffn_weights(w_in, w_out):
    depth, n_ffn, d, f2 = w_in.shape
    f = f2 // 2
    assert f % LANES == 0
    rows = _largest_tile(d, BF16_SUBLANES, CAST_IN_MAX_ROWS)
    half = (None, None, rows, f)
    wg, wu = pl.pallas_call(
        _cast_kernel,
        out_shape=[jax.ShapeDtypeStruct((depth, n_ffn, d, f), BF16)] * 2,
        grid=(depth, n_ffn, d // rows),
        in_specs=[pl.BlockSpec(half, lambda l, w, r: (l, w, r, 0)), pl.BlockSpec(half, lambda l, w, r: (l, w, r, 1))],
        out_specs=[pl.BlockSpec(half, lambda l, w, r: (l, w, r, 0))] * 2,
        compiler_params=_cparams(3),
        name="cast_ffn_in",
    )(w_in, w_in)
    rows = _largest_tile(f, BF16_SUBLANES, CAST_OUT_MAX_ROWS)
    blk = (None, None, rows, d)
    wo = pl.pallas_call(
        _cast_kernel,
        out_shape=jax.ShapeDtypeStruct(w_out.shape, BF16),
        grid=(depth, n_ffn, f // rows),
        in_specs=[pl.BlockSpec(blk, lambda l, w, r: (l, w, r, 0))],
        out_specs=pl.BlockSpec(blk, lambda l, w, r: (l, w, r, 0)),
        compiler_params=_cparams(3),
        name="cast_ffn_out",
    )(w_out)
    return wg, wu, wo


PROJ_CHUNK = 4 * HEAD_DIM


def _swap_halves_32(y):
    lane = lax.broadcasted_iota(jnp.int32, y.shape, 1)
    return jnp.where((lane & 32) == 0, pltpu.roll(y, LANES - 32, 1), pltpu.roll(y, 32, 1))


def _inproj_kernel(x_ref, g_ref, sh_ref, sc_ref, w_ref, gains_ref, cos_ref, sin_ref, *rest,
                   plan, tm, rows_per_batch, n_batch, n_lat_tiles, n_tiles1):
    i = pl.program_id(0)
    alt = (i >= n_tiles1, rest[0]) if n_tiles1 is not None else None
    out_refs, h_scr = rest[(0 if alt is None else 1):-1], rest[-1]
    r = _mod_row(i, tm, rows_per_batch, n_batch)
    is_lat = i < n_lat_tiles
    _norm_modulate_into(h_scr, x_ref, g_ref[...], sh_ref[pl.ds(r, 1), :], sc_ref[pl.ds(r, 1), :], alt)
    for w_col, width, out_idx, out_col, ops in plan:
        y = jnp.dot(h_scr[...], w_ref[:, w_col:w_col + width], preferred_element_type=F32)
        o_ref = out_refs[out_idx]
        if ops is None:
            o_ref[:, out_col:out_col + width] = y.astype(o_ref.dtype)
            continue
        for hh, op in enumerate(ops):
            yh = y[:, hh * HEAD_DIM:(hh + 1) * HEAD_DIM]
            if op is not None:
                gain_row, rope = op
                yh = yh * lax.rsqrt(jnp.mean(yh * yh, axis=-1, keepdims=True) + EPS)
                yh = yh * gains_ref[gain_row:gain_row + 1, :]
                if rope:
                    yh = jnp.where(is_lat, yh * cos_ref[...] + _swap_halves_32(yh) * sin_ref[...], yh)
            o_ref[:, out_col + hh * HEAD_DIM:out_col + (hh + 1) * HEAD_DIM] = yh.astype(o_ref.dtype)


def _stream_specs(x_src, x_src2, tm):
    d = x_src.shape[1]
    if x_src2 is None:
        return [pl.BlockSpec((tm, d), lambda i: (i, 0))], None
    n1 = x_src.shape[0] // tm
    return [pl.BlockSpec((tm, d), lambda i: (jnp.minimum(i, n1 - 1), 0)),
            pl.BlockSpec((tm, d), lambda i: (jnp.maximum(i - n1, 0), 0))], n1


def _inproj(x_src, x_src2, mod, layer, g, w, gains, cos_t, sin_t, plan, outs, *, tm, n_lat_rows, rows_per_batch,
            n_batch):
    d = x_src.shape[1]
    ta = x_src.shape[0] + (0 if x_src2 is None else x_src2.shape[0])
    n_pos_tiles = cos_t.shape[0] // tm
    (x_spec, *x2_spec), n_tiles1 = _stream_specs(x_src, x_src2, tm)
    kern = functools.partial(_inproj_kernel, plan=tuple(plan), tm=tm, rows_per_batch=rows_per_batch,
                             n_batch=n_batch, n_lat_tiles=n_lat_rows // tm, n_tiles1=n_tiles1)
    return pl.pallas_call(
        kern,
        out_shape=[jax.ShapeDtypeStruct((ta, n), dt) for n, dt in outs],
        grid=(ta // tm,),
        in_specs=[x_spec,
                  pl.BlockSpec((1, d), lambda i: (0, 0)),
                  _mod_spec(d, layer, 3, 1), _mod_spec(d, layer, 4, 1),
                  pl.BlockSpec(w.shape, lambda i: (0, 0), pipeline_mode=pl.Buffered(1)),
                  pl.BlockSpec(gains.shape, lambda i: (0, 0)),
                  pl.BlockSpec((tm, HEAD_DIM), lambda i: (i % n_pos_tiles, 0)),
                  pl.BlockSpec((tm, HEAD_DIM), lambda i: (i % n_pos_tiles, 0))] + x2_spec,
        out_specs=[pl.BlockSpec((tm, n), lambda i: (i, 0)) for n, _ in outs],
        scratch_shapes=[pltpu.VMEM((tm, d), BF16)],
        compiler_params=_cparams(1),
        name="mixer_inproj",
    )(x_src, g.reshape(1, d), mod, mod, w, gains, cos_t, sin_t, *([] if x_src2 is None else [x_src2]))


def _outproj_kernel(a1_ref, a2_ref, w1_ref, w2_ref, gt_ref, x_ref, *rest, tm, rows_per_batch, n_batch, n_tiles1):
    o_ref = rest[-1]
    i = pl.program_id(0)
    r = _mod_row(i, tm, rows_per_batch, n_batch)
    d = o_ref.shape[1]
    for c0 in range(0, d, PROJ_CHUNK):
        cs = slice(c0, min(c0 + PROJ_CHUNK, d))
        y = (jnp.dot(a1_ref[...], w1_ref[:, cs], preferred_element_type=F32)
             + jnp.dot(a2_ref[...], w2_ref[:, cs], preferred_element_type=F32))
        x = x_ref[:, cs] if n_tiles1 is None else jnp.where(i >= n_tiles1, rest[0][:, cs], x_ref[:, cs])
        o_ref[:, cs] = x + gt_ref[pl.ds(r, 1), cs] * y


def _outproj(a1, a2, a2_col_block, w, x_src, x_src2, n_rows, mod, layer, *, tm, rows_per_batch, n_batch):
    d = x_src.shape[1]
    kh = w.shape[0] // 2
    x_specs, n_tiles1 = _stream_specs(x_src, x_src2, tm)
    kern = functools.partial(_outproj_kernel, tm=tm, rows_per_batch=rows_per_batch, n_batch=n_batch,
                             n_tiles1=n_tiles1)
    return pl.pallas_call(
        kern,
        out_shape=jax.ShapeDtypeStruct((n_rows, d), F32),
        grid=(n_rows // tm,),
        in_specs=[pl.BlockSpec((tm, kh), lambda i: (i, 0)),
                  pl.BlockSpec((tm, kh), lambda i: (i, a2_col_block)),
                  pl.BlockSpec((kh, d), lambda i: (0, 0)),
                  pl.BlockSpec((kh, d), lambda i: (1, 0)),
                  _mod_spec(d, layer, 5, 1)] + x_specs,
        out_specs=pl.BlockSpec((tm, d), lambda i: (i, 0)),
        compiler_params=_cparams(1),
        name="mixer_outproj",
    )(a1, a2, w, w, mod, x_src, *([] if x_src2 is None else [x_src2]))


def _rope_tables(seq_len):
    half = HEAD_DIM // 2
    nf = half // 2
    inv = ROPE_THETA ** (-jnp.arange(nf, dtype=F32) / nf)
    pos = jnp.arange(seq_len, dtype=jnp.int32)
    ang_r = (pos // GRID_W).astype(F32)[:, None] * inv[None, :]
    ang_c = (pos % GRID_W).astype(F32)[:, None] * inv[None, :]
    cr, sr, cc, sc = jnp.cos(ang_r), jnp.sin(ang_r), jnp.cos(ang_c), jnp.sin(ang_c)
    return (jnp.concatenate([cr, cr, cc, cc], axis=-1),
            jnp.concatenate([-sr, sr, -sc, sc], axis=-1))


def _gelu_tanh(x):
    return x * (0.5 * (1.0 + jnp.tanh(np.float32(np.sqrt(2.0 / np.pi)) * (x + 0.044715 * (x * x * x)))))


def _lru_kernel(*refs, tt, reverse, final, n_lat_chunks):
    if final:
        (u_ref, wa_ref, ba_ref, wx_ref, bx_ref, lam_ref, hf_ref, gl_ref, o_ref, a_scr, b_scr, carry_scr) = refs
    else:
        (xp_ref, xc_ref, xn_ref, cw_ref, cb_ref, wa_ref, ba_ref, wx_ref, bx_ref, lam_ref,
         o_ref, u_ref, xs_scr, a_scr, b_scr, carry_scr) = refs
    s = pl.program_id(1)
    is_ctx = s == 0

    @pl.when(is_ctx)
    def _():
        carry_scr[...] = jnp.zeros_like(carry_scr)

    width = a_scr.shape[1]
    halo = SUBLANES
    left = CONV_W // 2
    if not final:
        j = (n_lat_chunks - s) if reverse else (s - 1)
        has_prev = jnp.logical_and(jnp.logical_not(is_ctx), j > 0)
        has_next = jnp.logical_and(jnp.logical_not(is_ctx), j < n_lat_chunks - 1)
        xs_scr[0:halo, :] = jnp.where(has_prev, xp_ref[tt - halo:tt, :], 0.0)
        xs_scr[halo:halo + tt, :] = xc_ref[...]
        xs_scr[halo + tt:2 * halo + tt, :] = jnp.where(has_next, xn_ref[0:halo, :], 0.0)

    z = -lam_ref[...]
    softplus = jnp.maximum(z, 0.0) + jnp.log1p(jnp.exp(-jnp.abs(z)))
    for n in range(width // HEAD_DIM):
        cs = slice(n * HEAD_DIM, (n + 1) * HEAD_DIM)
        if final:
            u = u_ref[:, cs]
        else:
            u = cb_ref[:, cs]
            xa = xs_scr[:, cs]
            n_rows = xa.shape[0]
            for k in range(CONV_W):
                xk = xa if k == left else pltpu.roll(xa, (left - k) % n_rows, 0)
                u = u + xk[halo:halo + tt, :] * cw_ref[k:k + 1, cs]
            u_ref[:, cs] = u
        ub = u.astype(BF16)
        gate_r = jnp.dot(ub, wa_ref[n], preferred_element_type=F32) + ba_ref[:, cs]
        gate_i = jnp.dot(ub, wx_ref[n], preferred_element_type=F32) + bx_ref[:, cs]
        log_a = (-LRU_C) * _sigmoid_tanh(gate_r) * softplus[:, cs]
        a = jnp.exp(log_a)
        a_scr[:, cs] = a
        b_scr[:, cs] = jnp.sqrt(-jnp.tanh(log_a) * (1.0 + a * a)) * (_sigmoid_tanh(gate_i) * u)

    n_groups = tt // SUBLANES
    row = lax.broadcasted_iota(jnp.int32, (SUBLANES, width), 0)

    def group(gi, carry):
        g = (n_groups - 1 - gi) if reverse else gi
        r0 = pl.multiple_of(g * SUBLANES, SUBLANES)
        a = a_scr[pl.ds(r0, SUBLANES), :]
        b = b_scr[pl.ds(r0, SUBLANES), :]
        for k in (1, 2, 4):
            if reverse:
                keep = row < SUBLANES - k
                shift = SUBLANES - k
            else:
                keep = row >= k
                shift = k
            a_sh = jnp.where(keep, pltpu.roll(a, shift, 0), 1.0)
            b_sh = jnp.where(keep, pltpu.roll(b, shift, 0), 0.0)
            b = a * b_sh + b
            a = a * a_sh
        h = b + a * carry
        b_scr[pl.ds(r0, SUBLANES), :] = h
        last = h[0:1, :] if reverse else h[SUBLANES - 1:SUBLANES, :]
        return jnp.broadcast_to(last, (SUBLANES, width))

    carry_scr[...] = lax.fori_loop(0, n_groups, group, carry_scr[...])

    if final:
        o_ref[...] = ((hf_ref[...] + b_scr[...]) * _gelu_tanh(gl_ref[...])).astype(o_ref.dtype)
    else:
        o_ref[...] = b_scr[...]


def _lru_pass(p, conv_w, conv_b, w_a, b_a, w_x, b_x, lam, first, *, reverse, n_batch, seq_len, ctx_len, tt):
    ta = p.shape[0]
    width = conv_w.shape[1]
    assert ctx_len == tt and seq_len % tt == 0
    n_lat_chunks = seq_len // tt
    ctx_block0 = n_batch * n_lat_chunks
    final = first is not None

    def chunk_index(b, s, off):
        j = (n_lat_chunks - s) if reverse else (s - 1)
        j = jnp.clip(j + off, 0, n_lat_chunks - 1)
        return jnp.where(s == 0, ctx_block0 + b, b * n_lat_chunks + j)

    def xspec(off):
        return pl.BlockSpec((tt, width), lambda b, s: (chunk_index(b, s, off), 0))

    row_spec = pl.BlockSpec((1, width), lambda b, s: (0, 0))
    w_spec = pl.BlockSpec(w_a.shape, lambda b, s: (0, 0, 0))
    gate_specs = [w_spec, row_spec, w_spec, row_spec, row_spec]
    gate_args = [w_a, b_a.reshape(1, width), w_x, b_x.reshape(1, width), lam.reshape(1, width)]
    scan_scratch = [pltpu.VMEM((tt, width), F32), pltpu.VMEM((tt, width), F32), pltpu.VMEM((SUBLANES, width), F32)]
    if final:
        h_first, u = first
        in_specs = [xspec(0)] + gate_specs + [xspec(0),
                                              pl.BlockSpec((tt, width), lambda b, s: (chunk_index(b, s, 0), 1))]
        args = [u] + gate_args + [h_first, p]
        out_shape, out_specs, scratch = jax.ShapeDtypeStruct((ta, width), BF16), xspec(0), scan_scratch
    else:
        in_specs = [xspec(-1), xspec(0), xspec(1),
                    pl.BlockSpec((CONV_W, width), lambda b, s: (0, 0)), row_spec] + gate_specs
        args = [p, p, p, conv_w, conv_b.reshape(1, width)] + gate_args
        out_shape = [jax.ShapeDtypeStruct((ta, width), F32)] * 2
        out_specs = [xspec(0), xspec(0)]
        scratch = [pltpu.VMEM((tt + 2 * SUBLANES, width), F32)] + scan_scratch
    kern = functools.partial(_lru_kernel, tt=tt, reverse=reverse, final=final, n_lat_chunks=n_lat_chunks)
    return pl.pallas_call(
        kern,
        out_shape=out_shape,
        grid=(n_batch, n_lat_chunks + 1),
        in_specs=in_specs,
        out_specs=out_specs,
        scratch_shapes=scratch,
        compiler_params=_cparams(2),
        name="rglru_rev" if reverse else "rglru_fwd",
    )(*args)


LOG2E = float(np.log2(np.e))
LOGIT_SCALE = ATTN_SCALE * LOG2E


def _softmax_parts(s2, extra_logit2):
    m = jnp.maximum(jnp.max(s2, axis=-1, keepdims=True), extra_logit2)
    e = jnp.exp2(s2 - m)
    return e, 1.0 / (jnp.sum(e, axis=-1, keepdims=True) + jnp.exp2(extra_logit2 - m))


def _swa_kernel(sink_ref, q_ref, kp_ref, kc_ref, kn_ref, kx_ref, vp_ref, vc_ref, vn_ref, vx_ref, mask_ref, o_ref,
                *, n_kv_heads):
    blk = q_ref.shape[0]
    head = lax.broadcasted_iota(jnp.int32, (GQA_GROUP * blk, 1), 0) // blk
    logits = []
    for kh in range(n_kv_heads):
        ks = slice(kh * HEAD_DIM, (kh + 1) * HEAD_DIM)
        q0 = kh * GQA_GROUP
        q = jnp.concatenate([q_ref[:, (q0 + h) * HEAD_DIM:(q0 + h + 1) * HEAD_DIM] for h in range(GQA_GROUP)],
                            axis=0)
        k = jnp.concatenate([kp_ref[:, ks], kc_ref[:, ks], kn_ref[:, ks], kx_ref[:, ks]], axis=0)
        logits.append(lax.dot_general(q, k, (((1,), (1,)), ((), ())), preferred_element_type=F32))
    for kh in range(n_kv_heads):
        ks = slice(kh * HEAD_DIM, (kh + 1) * HEAD_DIM)
        q0 = kh * GQA_GROUP
        v = jnp.concatenate([vp_ref[:, ks], vc_ref[:, ks], vn_ref[:, ks], vx_ref[:, ks]], axis=0)
        s = logits[kh] * LOGIT_SCALE + mask_ref[...]
        sink = jnp.zeros((GQA_GROUP * blk, 1), F32)
        for h in range(GQA_GROUP):
            sink = jnp.where(head == h, sink_ref[q0 + h] * LOG2E, sink)
        e, inv = _softmax_parts(s, sink)
        o = jnp.dot(e.astype(BF16), v, preferred_element_type=F32) * inv
        for h in range(GQA_GROUP):
            o_ref[:, (q0 + h) * HEAD_DIM:(q0 + h + 1) * HEAD_DIM] = o[h * blk:(h + 1) * blk, :].astype(o_ref.dtype)


def _swa_mask_table(blk, ctx_len):
    shape = (4, GQA_GROUP * blk, 3 * blk + ctx_len)
    cls = lax.broadcasted_iota(jnp.int32, shape, 0)
    qi = lax.broadcasted_iota(jnp.int32, shape, 1) % blk
    col = lax.broadcasted_iota(jnp.int32, shape, 2)
    lo = jnp.where(cls == 0, blk, 0)
    hi = jnp.where(cls == 3, 0, jnp.where(cls == 2, 2 * blk, 3 * blk))
    rel = col - blk - qi
    ok = ((col >= lo) & (col < hi) & (rel >= -WINDOW) & (rel <= WINDOW)) | (col >= 3 * blk)
    return jnp.where(ok, 0.0, NEG).astype(F32)


def _swa(qkv, sink, *, n_batch, seq_len, ctx_len, n_kv_heads, blk):
    ta = qkv.shape[0]
    n_heads = n_kv_heads * GQA_GROUP
    n_lat_blocks = seq_len // blk
    n_ctx_blocks = ctx_len // blk
    assert blk == WINDOW and seq_len % blk == 0 and ctx_len % blk == 0 and n_lat_blocks >= 2
    ctx_q0 = n_batch * n_lat_blocks
    ctx_kv0 = n_batch * seq_len // ctx_len
    kv_w = n_kv_heads * HEAD_DIM
    k_col, v_col = n_heads * HEAD_DIM // kv_w, n_heads * HEAD_DIM // kv_w + 1

    def q_index(b, n):
        return (jnp.where(n < n_lat_blocks, b * n_lat_blocks + n, ctx_q0 + b * n_ctx_blocks + n - n_lat_blocks), 0)

    def kv_spec(off, col):
        return pl.BlockSpec((blk, kv_w),
                            lambda b, n: (b * n_lat_blocks + jnp.clip(n + off, 0, n_lat_blocks - 1), col))

    def ctx_spec(col):
        return pl.BlockSpec((ctx_len, kv_w), lambda b, n: (ctx_kv0 + b, col))

    def mask_index(b, n):
        interior = jnp.where(n == 0, 0, jnp.where(n == n_lat_blocks - 1, 2, 1))
        return (jnp.where(n < n_lat_blocks, interior, 3), 0, 0)

    mask = _swa_mask_table(blk, ctx_len)
    kern = functools.partial(_swa_kernel, n_kv_heads=n_kv_heads)
    return pl.pallas_call(
        kern,
        out_shape=jax.ShapeDtypeStruct((ta, n_heads * HEAD_DIM), BF16),
        grid=(n_batch, n_lat_blocks + n_ctx_blocks),
        in_specs=[pl.BlockSpec(memory_space=pltpu.SMEM),
                  pl.BlockSpec((blk, n_heads * HEAD_DIM), q_index),
                  kv_spec(-1, k_col), kv_spec(0, k_col), kv_spec(1, k_col), ctx_spec(k_col),
                  kv_spec(-1, v_col), kv_spec(0, v_col), kv_spec(1, v_col), ctx_spec(v_col),
                  pl.BlockSpec((None,) + mask.shape[1:], mask_index)],
        out_specs=pl.BlockSpec((blk, n_heads * HEAD_DIM), q_index),
        compiler_params=_cparams(2),
        name="windowed_gqa",
    )(sink, *([qkv] * 9), mask)


NA_MAX_HEADS_PER_STEP = 16


def _na_heads_per_step(n_heads):
    return min(NA_MAX_HEADS_PER_STEP, n_heads)


def _na_kernel(q_ref, kvp_ref, kvc_ref, kvn_ref, kvx_ref, bias_ref, o_ref):
    n_local = bias_ref.shape[-1]
    gw = q_ref.shape[1]
    n = gw // HEAD_DIM
    kv_refs = (kvp_ref, kvc_ref, kvn_ref, kvx_ref)
    heads = [slice(h * HEAD_DIM, (h + 1) * HEAD_DIM) for h in range(n)]
    def qk(h):
        k = jnp.concatenate([ref[:, heads[h]] for ref in kv_refs], axis=0)
        return lax.dot_general(q_ref[:, heads[h]], k, (((1,), (1,)), ((), ())), preferred_element_type=F32)

    def softmax(h, logit):
        s = logit * LOGIT_SCALE
        s = jnp.concatenate([s[:, :n_local] + bias_ref[h], s[:, n_local:]], axis=1)
        return jnp.exp2(s - jnp.max(s, axis=-1, keepdims=True)).astype(BF16)

    def pv(h, e):
        vs = slice(gw + heads[h].start, gw + heads[h].stop)
        v = jnp.concatenate([ref[:, vs] for ref in kv_refs], axis=0)
        o = jnp.dot(e, jnp.concatenate([v, jnp.ones_like(v)], axis=1), preferred_element_type=F32)
        o_ref[:, heads[h]] = (o[:, :HEAD_DIM] * (1.0 / o[:, HEAD_DIM:])).astype(o_ref.dtype)

    logits = {0: qk(0)}
    weights = {}
    for h in range(n):
        if h + 1 < n:
            logits[h + 1] = qk(h + 1)
        weights[h] = softmax(h, logits.pop(h))
        if h >= 1:
            pv(h - 1, weights.pop(h - 1))
    pv(n - 1, weights.pop(n - 1))


def _na_bias_table(rpb, n_grid_rows):
    r_in, w = NA_ROWS_PER_BLOCK, GRID_W
    kh = min(NA_KH, n_grid_rows)
    n_blocks = n_grid_rows // r_in
    n_ro, n_co = 2 * NA_KH - 1, 2 * NA_KW - 1
    qc, kc = np.arange(w)[:, None], np.arange(w)[None, :]
    e_col = (kc - qc + NA_KW - 1 == np.arange(n_co)[:, None, None]).astype(np.float32)
    win_start = np.clip(qc - NA_KW // 2, 0, w - NA_KW)
    col_ok = (kc >= win_start) & (kc < win_start + NA_KW)
    ri, krj = np.arange(r_in)[:, None], np.arange(3 * r_in)[None, :]
    row_ok = []
    for jb in (0, max(n_blocks // 2, 1) if n_blocks > 2 else 0, n_blocks - 1):
        r, kr = r_in * jb + ri, r_in * (jb - 1) + krj
        rs = np.clip(r - kh // 2, 0, n_grid_rows - kh)
        row_ok.append((kr >= rs) & (kr < rs + kh))
    ok = np.stack(row_ok)[:, :, None, :, None] & col_ok[None, None, :, None, :]
    ok = ok.reshape(3, r_in * w, 3 * r_in * w)
    tz = jnp.einsum('hab,bqc->haqc', rpb.astype(F32), e_col, precision=lax.Precision.HIGHEST) * LOG2E
    assert r_in - 1 <= NA_KH - 1 - r_in and 3 * r_in - 1 - r_in + NA_KH - 1 < n_ro
    full = jnp.concatenate(
        [jnp.concatenate([tz[:, k - i - r_in + NA_KH - 1] for k in range(3 * r_in)], axis=-1) for i in range(r_in)],
        axis=-2)
    return jnp.where(ok[None], full[:, None], NEG)


def _na(qkv, bias, *, n_batch, seq_len, ctx_len, n_heads):
    tq = NA_ROWS_PER_BLOCK * GRID_W
    n_blocks = seq_len // tq
    assert ctx_len == tq and seq_len % tq == 0 and NA_KH == 2 * NA_ROWS_PER_BLOCK and n_blocks >= 2
    hps = _na_heads_per_step(n_heads)
    assert n_heads % hps == 0
    n_groups = n_heads // hps
    gw = hps * HEAD_DIM
    ctx0 = n_batch * n_blocks

    def kv_spec(off):
        return pl.BlockSpec((tq, 2 * gw),
                            lambda b, g, jb: (b * n_blocks + jnp.clip(jb + off, 0, n_blocks - 1), g))

    def bias_index(b, g, jb):
        return (g, jnp.where(jb == 0, 0, jnp.where(jb == n_blocks - 1, 2, 1)), 0, 0)

    return pl.pallas_call(
        _na_kernel,
        out_shape=jax.ShapeDtypeStruct((n_batch * seq_len, n_heads * HEAD_DIM), BF16),
        grid=(n_batch, n_groups, n_blocks),
        in_specs=[pl.BlockSpec((tq, gw), lambda b, g, jb: (b * n_blocks + jb, 2 * n_groups + g)),
                  kv_spec(-1), kv_spec(0), kv_spec(1),
                  pl.BlockSpec((ctx_len, 2 * gw), lambda b, g, jb: (ctx0 + b, g)),
                  pl.BlockSpec((hps, None) + bias.shape[2:], bias_index)],
        out_specs=pl.BlockSpec((tq, gw), lambda b, g, jb: (b * n_blocks + jb, g)),
        compiler_params=_cparams(3),
        name="neighbourhood_attn",
    )(*([qkv] * 5), bias)


def _tiles():
    return dict(tm=1024, tm_proj=512, tm_out=512, tf=512, tn_mod=1024)


def _chunked(total, out_idx, out_col0=0, w_col0=0):
    return [(w_col0 + c, min(PROJ_CHUNK, total - c), out_idx, out_col0 + c, None)
            for c in range(0, total, PROJ_CHUNK)]


def _head_chunks(head_ops, w_col0, out_idx, out_col0):
    per = PROJ_CHUNK // HEAD_DIM
    return [(w_col0 + h0 * HEAD_DIM, len(head_ops[h0:h0 + per]) * HEAD_DIM, out_idx, out_col0 + h0 * HEAD_DIM,
             tuple(head_ops[h0:h0 + per])) for h0 in range(0, len(head_ops), per)]


def kernel(x, c, ctx, c_ctx, w_mod, b_mod, norm_g, ffn_w_in, ffn_w_out, ab_w_in, lru_conv_w, lru_conv_b,
           lru_w_a, lru_b_a, lru_w_x, lru_b_x, lru_lambda, attn_q_norm, attn_k_norm, attn_sink, ab_w_out,
           na_w_in, na_q_norm, na_k_norm, na_rpb, na_w_out):
    n_batch, seq_len, d = x.shape
    ctx_len = ctx.shape[1]
    depth = w_mod.shape[0]
    lru_w = lru_conv_w.shape[2]
    n_kv = attn_sink.shape[1] // GQA_GROUP
    n_att = n_kv * GQA_GROUP
    na_heads = na_rpb.shape[1]
    assert n_batch < MOD_ROWS and depth == 2
    t = _tiles()
    tl, tc = n_batch * seq_len, n_batch * ctx_len
    geom = dict(rows_per_batch=seq_len, n_batch=n_batch)

    cc = jnp.zeros((MOD_ROWS, d), F32).at[:n_batch].set(c).at[n_batch].set(c_ctx)
    mod = _mod_table(cc, w_mod, b_mod, t["tn_mod"])
    wg, wu, wo = _cast_ffn_weights(ffn_w_in, ffn_w_out)

    def ffn(x_src, n_rows, layer, which, row0=0):
        return _ffn(x_src, n_rows, row0, mod, layer, which, norm_g[layer, 2 * which], wg, wu, wo,
                    tm=t["tm"], tf=t["tf"], **geom)

    cos_t, sin_t = _rope_tables(seq_len)
    proj_geom = dict(tm=t["tm_proj"], n_lat_rows=tl, **geom)

    x_lat = ffn(x.reshape(tl, d), tl, 0, 0)
    x_ctx = ffn(ctx.reshape(tc, d), tc, 0, 0, row0=tl)
    gains = jnp.stack([attn_q_norm[0], attn_k_norm[0]])
    head_ops = [(0, True)] * n_att + [(1, True)] * n_kv + [None] * n_kv
    plan = _head_chunks(head_ops, 2 * lru_w, 1, 0) + _chunked(2 * lru_w, 0)
    p, qkv = _inproj(x_lat, x_ctx, mod, 0, norm_g[0, 1], ab_w_in[0].astype(BF16), gains, cos_t, sin_t, plan,
                     ((2 * lru_w, F32), (len(head_ops) * HEAD_DIM, BF16)), **proj_geom)
    lru_args = lambda dr: (lru_conv_w[0], lru_conv_b[0], lru_w_a[0, dr].astype(BF16), lru_b_a[0, dr],
                           lru_w_x[0, dr].astype(BF16), lru_b_x[0, dr], lru_lambda[0, dr])
    scan_geom = dict(n_batch=n_batch, seq_len=seq_len, ctx_len=ctx_len, tt=ctx_len)
    h_fwd = _lru_pass(p, *lru_args(0), None, reverse=False, **scan_geom)
    lru = _lru_pass(p, *lru_args(1), h_fwd, reverse=True, **scan_geom)
    att = _swa(qkv, attn_sink[0], n_batch=n_batch, seq_len=seq_len, ctx_len=ctx_len, n_kv_heads=n_kv,
               blk=WINDOW)
    x_all = _outproj(lru, att, 0, ab_w_out[0].astype(BF16), x_lat, x_ctx, tl + tc, mod, 0, tm=t["tm_out"], **geom)
    x_all = ffn(x_all, tl + tc, 0, 1)

    x_all = ffn(x_all, tl + tc, 1, 0)
    gains = jnp.stack([na_q_norm[0], na_k_norm[0]])
    na_d = na_heads * HEAD_DIM
    hps = _na_heads_per_step(na_heads)
    gw = hps * HEAD_DIM
    groups = range(0, na_d, gw)
    plan = sum([_head_chunks([(0, False)] * hps, g0, 0, 2 * na_d + g0) for g0 in groups], [])
    plan += sum([_head_chunks([(1, False)] * hps, na_d + g0, 0, 2 * g0) for g0 in groups], [])
    plan += sum([_head_chunks([None] * hps, 2 * na_d + g0, 0, 2 * g0 + gw) for g0 in groups], [])
    (qkv,) = _inproj(x_all, None, mod, 1, norm_g[1, 1], na_w_in[0].astype(BF16), gains, cos_t, sin_t, plan,
                     ((3 * na_d, BF16),), **proj_geom)
    o = _na(qkv, _na_bias_table(na_rpb[0], seq_len // GRID_W), n_batch=n_batch, seq_len=seq_len, ctx_len=ctx_len,
            n_heads=na_heads)
    x_lat = _outproj(o, o, 1, na_w_out[0].astype(BF16), x_all, None, tl, mod, 1, tm=t["tm_out"], **geom)
    x_lat = ffn(x_lat, tl, 1, 1)
    return x_lat.reshape(n_batch, seq_len, d)
```

```python
import functools

import numpy as np
import jax
import jax.numpy as jnp
from jax import lax
from jax.experimental import pallas as pl
from jax.experimental.pallas import tpu as pltpu

HEAD_DIM = 128
EPS = 1e-6
N_MOD = 9
GRID_W = 64
CONV_W = 4
LRU_C = 8.0
WINDOW = 128
ROPE_THETA = 10000.0
NA_KH = 8
NA_KW = 16
ATTN_SCALE = HEAD_DIM ** -0.5
GQA_GROUP = 4

LANES = 128
SUBLANES = 8
BF16_SUBLANES = 2 * SUBLANES
CAST_IN_MAX_ROWS = 256
CAST_OUT_MAX_ROWS = 1024
VMEM_LIMIT_BYTES = 56 * 1024 * 1024
MOD_ROWS = SUBLANES

NEG = -1e30
NA_ROWS_PER_BLOCK = 4

BF16 = jnp.bfloat16
F32 = jnp.float32


def _cparams(n_axes):
    return pltpu.CompilerParams(dimension_semantics=("arbitrary",) * n_axes,
                                vmem_limit_bytes=VMEM_LIMIT_BYTES)


def _sigmoid(x):
    return 1.0 / (1.0 + jnp.exp(-x))


def _sigmoid_tanh(x):
    return 0.5 * jnp.tanh(0.5 * x) + 0.5


NORM_CHUNK_ROWS = 16


def _norm_modulate_into(h_scr, x_ref, g, shift, scale, alt=None):
    gain = g * (1.0 + scale)
    n_chunks = x_ref.shape[0] // NORM_CHUNK_ROWS

    def load(c):
        r0 = pl.multiple_of(c * NORM_CHUNK_ROWS, NORM_CHUNK_ROWS)
        x = x_ref[pl.ds(r0, NORM_CHUNK_ROWS), :]
        if alt is not None:
            x = jnp.where(alt[0], alt[1][pl.ds(r0, NORM_CHUNK_ROWS), :], x)
        return r0, x

    def inv_rms(c):
        _, x = load(c)
        return lax.rsqrt(jnp.mean(x * x, axis=-1, keepdims=True) + EPS)

    def emit(c, inv):
        r0, x = load(c)
        h_scr[pl.ds(r0, NORM_CHUNK_ROWS), :] = ((x * inv) * gain + shift).astype(h_scr.dtype)

    def step(c, inv_prev):
        inv = inv_rms(c)
        emit(c - 1, inv_prev)
        return inv

    emit(n_chunks - 1, lax.fori_loop(1, n_chunks, step, inv_rms(0), unroll=8))


def _mod_kernel(c_ref, w_ref, b_ref, o_ref):
    c = c_ref[...]
    s = c * _sigmoid(c)
    o_ref[...] = jnp.dot(s.astype(BF16), w_ref[...].astype(BF16),
                         preferred_element_type=F32) + b_ref[...]


def _mod_table(cc, w_mod, b_mod, tn):
    depth, d, n = w_mod.shape
    return pl.pallas_call(
        _mod_kernel,
        out_shape=jax.ShapeDtypeStruct((depth, MOD_ROWS, n), F32),
        grid=(depth, n // tn),
        in_specs=[pl.BlockSpec((MOD_ROWS, d), lambda l, j: (0, 0)),
                  pl.BlockSpec((None, d, tn), lambda l, j: (l, 0, j)),
                  pl.BlockSpec((None, 1, tn), lambda l, j: (l, 0, j))],
        out_specs=pl.BlockSpec((None, MOD_ROWS, tn), lambda l, j: (l, 0, j)),
        compiler_params=_cparams(2),
        name="mod_table",
    )(cc, w_mod, b_mod.reshape(depth, 1, n))


def _mod_spec(d, layer, k, n_grid_axes):
    if n_grid_axes == 1:
        return pl.BlockSpec((None, MOD_ROWS, d), lambda i: (layer, 0, k))
    return pl.BlockSpec((None, MOD_ROWS, d), lambda i, j: (layer, 0, k))


def _mod_row(tile_idx, tm, rows_per_batch, n_batch):
    return jnp.minimum(tile_idx * tm // rows_per_batch, n_batch)


def _ffn_kernel(x_ref, g_ref, sh_ref, sc_ref, gt_ref, wg_ref, wu_ref, wo_ref, o_ref, h_scr,
                *, tm, tile0, rows_per_batch, n_batch, last_width):
    i, j = pl.program_id(0), pl.program_id(1)
    last = pl.num_programs(1) - 1
    r = _mod_row(i + tile0, tm, rows_per_batch, n_batch)

    def hidden_block(width, first):
        h = h_scr[...]
        a = jnp.dot(h, wg_ref[:, :width], preferred_element_type=F32)
        u = jnp.dot(h, wu_ref[:, :width], preferred_element_type=F32)
        act = (a * _sigmoid(a)) * u
        y = jnp.dot(act.astype(BF16), wo_ref[:width, :], preferred_element_type=F32)
        o_ref[...] = y if first else o_ref[...] + y

    @pl.when(j == 0)
    def _():
        _norm_modulate_into(h_scr, x_ref, g_ref[...], sh_ref[pl.ds(r, 1), :], sc_ref[pl.ds(r, 1), :])
        hidden_block(wg_ref.shape[1], True)

    pl.when(jnp.logical_and(j > 0, j < last))(lambda: hidden_block(wg_ref.shape[1], False))

    @pl.when(j == last)
    def _():
        hidden_block(last_width, False)
        o_ref[...] = x_ref[...] + 0.5 * gt_ref[pl.ds(r, 1), :] * o_ref[...]


def _ffn(x_src, n_rows, row0, mod, layer, which, g, wg, wu, wo, *, tm, tf, rows_per_batch, n_batch):
    d = x_src.shape[1]
    d_ff = wo.shape[2]
    nf = pl.cdiv(d_ff, tf)
    last_width = d_ff - (nf - 1) * tf
    assert last_width % LANES == 0 and row0 % tm == 0 and nf >= 2
    k0 = 6 * which
    kern = functools.partial(_ffn_kernel, tm=tm, tile0=row0 // tm, rows_per_batch=rows_per_batch,
                             n_batch=n_batch, last_width=last_width)
    return pl.pallas_call(
        kern,
        out_shape=jax.ShapeDtypeStruct((n_rows, d), F32),
        grid=(n_rows // tm, nf),
        in_specs=[pl.BlockSpec((tm, d), lambda i, j: (i, 0)),
                  pl.BlockSpec((1, d), lambda i, j: (0, 0)),
                  _mod_spec(d, layer, k0, 2), _mod_spec(d, layer, k0 + 1, 2), _mod_spec(d, layer, k0 + 2, 2),
                  pl.BlockSpec((None, None, d, tf), lambda i, j: (layer, which, 0, j)),
                  pl.BlockSpec((None, None, d, tf), lambda i, j: (layer, which, 0, j)),
                  pl.BlockSpec((None, None, tf, d), lambda i, j: (layer, which, j, 0))],
        out_specs=pl.BlockSpec((tm, d), lambda i, j: (i, 0)),
        scratch_shapes=[pltpu.VMEM((tm, d), BF16)],
        compiler_params=_cparams(2),
        name="half_ffn",
    )(x_src, g.reshape(1, d), mod, mod, mod, wg, wu, wo)


def _cast_kernel(*refs):
    n = len(refs) // 2
    for src, dst in zip(refs[:n], refs[n:]):
        dst[...] = src[...].astype(dst.dtype)


def _largest_tile(n, multiple, cap):
    return max(t for t in range(multiple, min(n, cap) + 1, multiple) if n % t == 0)


def _cast_ffn_weights(w_in, w_out):
    depth, n_ffn, d, f2 = w_in.shape
    f = f2 // 2
    assert f % LANES == 0
    rows = _largest_tile(d, BF16_SUBLANES, CAST_IN_MAX_ROWS)
    half = (None, None, rows, f)
    wg, wu = pl.pallas_call(
        _cast_kernel,
        out_shape=[jax.ShapeDtypeStruct((depth, n_ffn, d, f), BF16)] * 2,
        grid=(depth, n_ffn, d // rows),
        in_specs=[pl.BlockSpec(half, lambda l, w, r: (l, w, r, 0)), pl.BlockSpec(half, lambda l, w, r: (l, w, r, 1))],
        out_specs=[pl.BlockSpec(half, lambda l, w, r: (l, w, r, 0))] * 2,
        compiler_params=_cparams(3),
        name="cast_ffn_in",
    )(w_in, w_in)
    rows = _largest_tile(f, BF16_SUBLANES, CAST_OUT_MAX_ROWS)
    blk = (None, None, rows, d)
    wo = pl.pallas_call(
        _cast_kernel,
        out_shape=jax.ShapeDtypeStruct(w_out.shape, BF16),
        grid=(depth, n_ffn, f // rows),
        in_specs=[pl.BlockSpec(blk, lambda l, w, r: (l, w, r, 0))],
        out_specs=pl.BlockSpec(blk, lambda l, w, r: (l, w, r, 0)),
        compiler_params=_cparams(3),
        name="cast_ffn_out",
    )(w_out)
    return wg, wu, wo


PROJ_CHUNK = 4 * HEAD_DIM


def _swap_halves_32(y):
    lane = lax.broadcasted_iota(jnp.int32, y.shape, 1)
    return jnp.where((lane & 32) == 0, pltpu.roll(y, LANES - 32, 1), pltpu.roll(y, 32, 1))


def _inproj_kernel(x_ref, g_ref, sh_ref, sc_ref, w_ref, gains_ref, cos_ref, sin_ref, *rest,
                   plan, tm, rows_per_batch, n_batch, n_lat_tiles, n_tiles1):
    i = pl.program_id(0)
    alt = (i >= n_tiles1, rest[0]) if n_tiles1 is not None else None
    out_refs, h_scr = rest[(0 if alt is None else 1):-1], rest[-1]
    r = _mod_row(i, tm, rows_per_batch, n_batch)
    is_lat = i < n_lat_tiles
    _norm_modulate_into(h_scr, x_ref, g_ref[...], sh_ref[pl.ds(r, 1), :], sc_ref[pl.ds(r, 1), :], alt)
    for w_col, width, out_idx, out_col, ops in plan:
        y = jnp.dot(h_scr[...], w_ref[:, w_col:w_col + width], preferred_element_type=F32)
        o_ref = out_refs[out_idx]
        if ops is None:
            o_ref[:, out_col:out_col + width] = y.astype(o_ref.dtype)
            continue
        for hh, op in enumerate(ops):
            yh = y[:, hh * HEAD_DIM:(hh + 1) * HEAD_DIM]
            if op is not None:
                gain_row, rope = op
                yh = yh * lax.rsqrt(jnp.mean(yh * yh, axis=-1, keepdims=True) + EPS)
                yh = yh * gains_ref[gain_row:gain_row + 1, :]
                if rope:
                    yh = jnp.where(is_lat, yh * cos_ref[...] + _swap_halves_32(yh) * sin_ref[...], yh)
            o_ref[:, out_col + hh * HEAD_DIM:out_col + (hh + 1) * HEAD_DIM] = yh.astype(o_ref.dtype)


def _stream_specs(x_src, x_src2, tm):
    d = x_src.shape[1]
    if x_src2 is None:
        return [pl.BlockSpec((tm, d), lambda i: (i, 0))], None
    n1 = x_src.shape[0] // tm
    return [pl.BlockSpec((tm, d), lambda i: (jnp.minimum(i, n1 - 1), 0)),
            pl.BlockSpec((tm, d), lambda i: (jnp.maximum(i - n1, 0), 0))], n1


def _inproj(x_src, x_src2, mod, layer, g, w, gains, cos_t, sin_t, plan, outs, *, tm, n_lat_rows, rows_per_batch,
            n_batch):
    d = x_src.shape[1]
    ta = x_src.shape[0] + (0 if x_src2 is None else x_src2.shape[0])
    n_pos_tiles = cos_t.shape[0] // tm
    (x_spec, *x2_spec), n_tiles1 = _stream_specs(x_src, x_src2, tm)
    kern = functools.partial(_inproj_kernel, plan=tuple(plan), tm=tm, rows_per_batch=rows_per_batch,
                             n_batch=n_batch, n_lat_tiles=n_lat_rows // tm, n_tiles1=n_tiles1)
    return pl.pallas_call(
        kern,
        out_shape=[jax.ShapeDtypeStruct((ta, n), dt) for n, dt in outs],
        grid=(ta // tm,),
        in_specs=[x_spec,
                  pl.BlockSpec((1, d), lambda i: (0, 0)),
                  _mod_spec(d, layer, 3, 1), _mod_spec(d, layer, 4, 1),
                  pl.BlockSpec(w.shape, lambda i: (0, 0), pipeline_mode=pl.Buffered(1)),
                  pl.BlockSpec(gains.shape, lambda i: (0, 0)),
                  pl.BlockSpec((tm, HEAD_DIM), lambda i: (i % n_pos_tiles, 0)),
                  pl.BlockSpec((tm, HEAD_DIM), lambda i: (i % n_pos_tiles, 0))] + x2_spec,
        out_specs=[pl.BlockSpec((tm, n), lambda i: (i, 0)) for n, _ in outs],
        scratch_shapes=[pltpu.VMEM((tm, d), BF16)],
        compiler_params=_cparams(1),
        name="mixer_inproj",
    )(x_src, g.reshape(1, d), mod, mod, w, gains, cos_t, sin_t, *([] if x_src2 is None else [x_src2]))


def _outproj_kernel(a1_ref, a2_ref, w1_ref, w2_ref, gt_ref, x_ref, *rest, tm, rows_per_batch, n_batch, n_tiles1):
    o_ref = rest[-1]
    i = pl.program_id(0)
    r = _mod_row(i, tm, rows_per_batch, n_batch)
    d = o_ref.shape[1]
    for c0 in range(0, d, PROJ_CHUNK):
        cs = slice(c0, min(c0 + PROJ_CHUNK, d))
        y = (jnp.dot(a1_ref[...], w1_ref[:, cs], preferred_element_type=F32)
             + jnp.dot(a2_ref[...], w2_ref[:, cs], preferred_element_type=F32))
        x = x_ref[:, cs] if n_tiles1 is None else jnp.where(i >= n_tiles1, rest[0][:, cs], x_ref[:, cs])
        o_ref[:, cs] = x + gt_ref[pl.ds(r, 1), cs] * y


def _outproj(a1, a2, a2_col_block, w, x_src, x_src2, n_rows, mod, layer, *, tm, rows_per_batch, n_batch):
    d = x_src.shape[1]
    kh = w.shape[0] // 2
    x_specs, n_tiles1 = _stream_specs(x_src, x_src2, tm)
    kern = functools.partial(_outproj_kernel, tm=tm, rows_per_batch=rows_per_batch, n_batch=n_batch,
                             n_tiles1=n_tiles1)
    return pl.pallas_call(
        kern,
        out_shape=jax.ShapeDtypeStruct((n_rows, d), F32),
        grid=(n_rows // tm,),
        in_specs=[pl.BlockSpec((tm, kh), lambda i: (i, 0)),
                  pl.BlockSpec((tm, kh), lambda i: (i, a2_col_block)),
                  pl.BlockSpec((kh, d), lambda i: (0, 0)),
                  pl.BlockSpec((kh, d), lambda i: (1, 0)),
                  _mod_spec(d, layer, 5, 1)] + x_specs,
        out_specs=pl.BlockSpec((tm, d), lambda i: (i, 0)),
        compiler_params=_cparams(1),
        name="mixer_outproj",
    )(a1, a2, w, w, mod, x_src, *([] if x_src2 is None else [x_src2]))


def _rope_tables(seq_len):
    half = HEAD_DIM // 2
    nf = half // 2
    inv = ROPE_THETA ** (-jnp.arange(nf, dtype=F32) / nf)
    pos = jnp.arange(seq_len, dtype=jnp.int32)
    ang_r = (pos // GRID_W).astype(F32)[:, None] * inv[None, :]
    ang_c = (pos % GRID_W).astype(F32)[:, None] * inv[None, :]
    cr, sr, cc, sc = jnp.cos(ang_r), jnp.sin(ang_r), jnp.cos(ang_c), jnp.sin(ang_c)
    return (jnp.concatenate([cr, cr, cc, cc], axis=-1),
            jnp.concatenate([-sr, sr, -sc, sc], axis=-1))


def _gelu_tanh(x):
    return x * (0.5 * (1.0 + jnp.tanh(np.float32(np.sqrt(2.0 / np.pi)) * (x + 0.044715 * (x * x * x)))))


def _lru_kernel(*refs, tt, reverse, final, n_lat_chunks):
    if final:
        (u_ref, wa_ref, ba_ref, wx_ref, bx_ref, lam_ref, hf_ref, gl_ref, o_ref, a_scr, b_scr, carry_scr) = refs
    else:
        (xp_ref, xc_ref, xn_ref, cw_ref, cb_ref, wa_ref, ba_ref, wx_ref, bx_ref, lam_ref,
         o_ref, u_ref, xs_scr, a_scr, b_scr, carry_scr) = refs
    s = pl.program_id(1)
    is_ctx = s == 0

    @pl.when(is_ctx)
    def _():
        carry_scr[...] = jnp.zeros_like(carry_scr)

    width = a_scr.shape[1]
    halo = SUBLANES
    left = CONV_W // 2
    if not final:
        j = (n_lat_chunks - s) if reverse else (s - 1)
        has_prev = jnp.logical_and(jnp.logical_not(is_ctx), j > 0)
        has_next = jnp.logical_and(jnp.logical_not(is_ctx), j < n_lat_chunks - 1)
        xs_scr[0:halo, :] = jnp.where(has_prev, xp_ref[tt - halo:tt, :], 0.0)
        xs_scr[halo:halo + tt, :] = xc_ref[...]
        xs_scr[halo + tt:2 * halo + tt, :] = jnp.where(has_next, xn_ref[0:halo, :], 0.0)

    z = -lam_ref[...]
    softplus = jnp.maximum(z, 0.0) + jnp.log1p(jnp.exp(-jnp.abs(z)))
    for n in range(width // HEAD_DIM):
        cs = slice(n * HEAD_DIM, (n + 1) * HEAD_DIM)
        if final:
            u = u_ref[:, cs]
        else:
            u = cb_ref[:, cs]
            xa = xs_scr[:, cs]
            n_rows = xa.shape[0]
            for k in range(CONV_W):
                xk = xa if k == left else pltpu.roll(xa, (left - k) % n_rows, 0)
                u = u + xk[halo:halo + tt, :] * cw_ref[k:k + 1, cs]
            u_ref[:, cs] = u
        ub = u.astype(BF16)
        gate_r = jnp.dot(ub, wa_ref[n], preferred_element_type=F32) + ba_ref[:, cs]
        gate_i = jnp.dot(ub, wx_ref[n], preferred_element_type=F32) + bx_ref[:, cs]
        log_a = (-LRU_C) * _sigmoid_tanh(gate_r) * softplus[:, cs]
        a = jnp.exp(log_a)
        a_scr[:, cs] = a
        b_scr[:, cs] = jnp.sqrt(-jnp.tanh(log_a) * (1.0 + a * a)) * (_sigmoid_tanh(gate_i) * u)

    n_groups = tt // SUBLANES
    row = lax.broadcasted_iota(jnp.int32, (SUBLANES, width), 0)

    def group(gi, carry):
        g = (n_groups - 1 - gi) if reverse else gi
        r0 = pl.multiple_of(g * SUBLANES, SUBLANES)
        a = a_scr[pl.ds(r0, SUBLANES), :]
        b = b_scr[pl.ds(r0, SUBLANES), :]
        for k in (1, 2, 4):
            if reverse:
                keep = row < SUBLANES - k
                shift = SUBLANES - k
            else:
                keep = row >= k
                shift = k
            a_sh = jnp.where(keep, pltpu.roll(a, shift, 0), 1.0)
            b_sh = jnp.where(keep, pltpu.roll(b, shift, 0), 0.0)
            b = a * b_sh + b
            a = a * a_sh
        h = b + a * carry
        b_scr[pl.ds(r0, SUBLANES), :] = h
        last = h[0:1, :] if reverse else h[SUBLANES - 1:SUBLANES, :]
        return jnp.broadcast_to(last, (SUBLANES, width))

    carry_scr[...] = lax.fori_loop(0, n_groups, group, carry_scr[...])

    if final:
        o_ref[...] = ((hf_ref[...] + b_scr[...]) * _gelu_tanh(gl_ref[...])).astype(o_ref.dtype)
    else:
        o_ref[...] = b_scr[...]


def _lru_pass(p, conv_w, conv_b, w_a, b_a, w_x, b_x, lam, first, *, reverse, n_batch, seq_len, ctx_len, tt):
    ta = p.shape[0]
    width = conv_w.shape[1]
    assert ctx_len == tt and seq_len % tt == 0
    n_lat_chunks = seq_len // tt
    ctx_block0 = n_batch * n_lat_chunks
    final = first is not None

    def chunk_index(b, s, off):
        j = (n_lat_chunks - s) if reverse else (s - 1)
        j = jnp.clip(j + off, 0, n_lat_chunks - 1)
        return jnp.where(s == 0, ctx_block0 + b, b * n_lat_chunks + j)

    def xspec(off):
        return pl.BlockSpec((tt, width), lambda b, s: (chunk_index(b, s, off), 0))

    row_spec = pl.BlockSpec((1, width), lambda b, s: (0, 0))
    w_spec = pl.BlockSpec(w_a.shape, lambda b, s: (0, 0, 0))
    gate_specs = [w_spec, row_spec, w_spec, row_spec, row_spec]
    gate_args = [w_a, b_a.reshape(1, width), w_x, b_x.reshape(1, width), lam.reshape(1, width)]
    scan_scratch = [pltpu.VMEM((tt, width), F32), pltpu.VMEM((tt, width), F32), pltpu.VMEM((SUBLANES, width), F32)]
    if final:
        h_first, u = first
        in_specs = [xspec(0)] + gate_specs + [xspec(0),
                                              pl.BlockSpec((tt, width), lambda b, s: (chunk_index(b, s, 0), 1))]
        args = [u] + gate_args + [h_first, p]
        out_shape, out_specs, scratch = jax.ShapeDtypeStruct((ta, width), BF16), xspec(0), scan_scratch
    else:
        in_specs = [xspec(-1), xspec(0), xspec(1),
                    pl.BlockSpec((CONV_W, width), lambda b, s: (0, 0)), row_spec] + gate_specs
        args = [p, p, p, conv_w, conv_b.reshape(1, width)] + gate_args
        out_shape = [jax.ShapeDtypeStruct((ta, width), F32)] * 2
        out_specs = [xspec(0), xspec(0)]
        scratch = [pltpu.VMEM((tt + 2 * SUBLANES, width), F32)] + scan_scratch
    kern = functools.partial(_lru_kernel, tt=tt, reverse=reverse, final=final, n_lat_chunks=n_lat_chunks)
    return pl.pallas_call(
        kern,
        out_shape=out_shape,
        grid=(n_batch, n_lat_chunks + 1),
        in_specs=in_specs,
        out_specs=out_specs,
        scratch_shapes=scratch,
        compiler_params=_cparams(2),
        name="rglru_rev" if reverse else "rglru_fwd",
    )(*args)


LOG2E = float(np.log2(np.e))
LOGIT_SCALE = ATTN_SCALE * LOG2E


def _softmax_parts(s2, extra_logit2):
    m = jnp.maximum(jnp.max(s2, axis=-1, keepdims=True), extra_logit2)
    e = jnp.exp2(s2 - m)
    return e, 1.0 / (jnp.sum(e, axis=-1, keepdims=True) + jnp.exp2(extra_logit2 - m))


def _swa_kernel(sink_ref, q_ref, kp_ref, kc_ref, kn_ref, kx_ref, vp_ref, vc_ref, vn_ref, vx_ref, mask_ref, o_ref,
                *, n_kv_heads):
    blk = q_ref.shape[0]
    head = lax.broadcasted_iota(jnp.int32, (GQA_GROUP * blk, 1), 0) // blk
    logits = []
    for kh in range(n_kv_heads):
        ks = slice(kh * HEAD_DIM, (kh + 1) * HEAD_DIM)
        q0 = kh * GQA_GROUP
        q = jnp.concatenate([q_ref[:, (q0 + h) * HEAD_DIM:(q0 + h + 1) * HEAD_DIM] for h in range(GQA_GROUP)],
                            axis=0)
        k = jnp.concatenate([kp_ref[:, ks], kc_ref[:, ks], kn_ref[:, ks], kx_ref[:, ks]], axis=0)
        logits.append(lax.dot_general(q, k, (((1,), (1,)), ((), ())), preferred_element_type=F32))
    for kh in range(n_kv_heads):
        ks = slice(kh * HEAD_DIM, (kh + 1) * HEAD_DIM)
        q0 = kh * GQA_GROUP
        v = jnp.concatenate([vp_ref[:, ks], vc_ref[:, ks], vn_ref[:, ks], vx_ref[:, ks]], axis=0)
        s = logits[kh] * LOGIT_SCALE + mask_ref[...]
        sink = jnp.zeros((GQA_GROUP * blk, 1), F32)
        for h in range(GQA_GROUP):
            sink = jnp.where(head == h, sink_ref[q0 + h] * LOG2E, sink)
        e, inv = _softmax_parts(s, sink)
        o = jnp.dot(e.astype(BF16), v, preferred_element_type=F32) * inv
        for h in range(GQA_GROUP):
            o_ref[:, (q0 + h) * HEAD_DIM:(q0 + h + 1) * HEAD_DIM] = o[h * blk:(h + 1) * blk, :].astype(o_ref.dtype)


def _swa_mask_table(blk, ctx_len):
    shape = (4, GQA_GROUP * blk, 3 * blk + ctx_len)
    cls = lax.broadcasted_iota(jnp.int32, shape, 0)
    qi = lax.broadcasted_iota(jnp.int32, shape, 1) % blk
    col = lax.broadcasted_iota(jnp.int32, shape, 2)
    lo = jnp.where(cls == 0, blk, 0)
    hi = jnp.where(cls == 3, 0, jnp.where(cls == 2, 2 * blk, 3 * blk))
    rel = col - blk - qi
    ok = ((col >= lo) & (col < hi) & (rel >= -WINDOW) & (rel <= WINDOW)) | (col >= 3 * blk)
    return jnp.where(ok, 0.0, NEG).astype(F32)


def _swa(qkv, sink, *, n_batch, seq_len, ctx_len, n_kv_heads, blk):
    ta = qkv.shape[0]
    n_heads = n_kv_heads * GQA_GROUP
    n_lat_blocks = seq_len // blk
    n_ctx_blocks = ctx_len // blk
    assert blk == WINDOW and seq_len % blk == 0 and ctx_len % blk == 0 and n_lat_blocks >= 2
    ctx_q0 = n_batch * n_lat_blocks
    ctx_kv0 = n_batch * seq_len // ctx_len
    kv_w = n_kv_heads * HEAD_DIM
    k_col, v_col = n_heads * HEAD_DIM // kv_w, n_heads * HEAD_DIM // kv_w + 1

    def q_index(b, n):
        return (jnp.where(n < n_lat_blocks, b * n_lat_blocks + n, ctx_q0 + b * n_ctx_blocks + n - n_lat_blocks), 0)

    def kv_spec(off, col):
        return pl.BlockSpec((blk, kv_w),
                            lambda b, n: (b * n_lat_blocks + jnp.clip(n + off, 0, n_lat_blocks - 1), col))

    def ctx_spec(col):
        return pl.BlockSpec((ctx_len, kv_w), lambda b, n: (ctx_kv0 + b, col))

    def mask_index(b, n):
        interior = jnp.where(n == 0, 0, jnp.where(n == n_lat_blocks - 1, 2, 1))
        return (jnp.where(n < n_lat_blocks, interior, 3), 0, 0)

    mask = _swa_mask_table(blk, ctx_len)
    kern = functools.partial(_swa_kernel, n_kv_heads=n_kv_heads)
    return pl.pallas_call(
        kern,
        out_shape=jax.ShapeDtypeStruct((ta, n_heads * HEAD_DIM), BF16),
        grid=(n_batch, n_lat_blocks + n_ctx_blocks),
        in_specs=[pl.BlockSpec(memory_space=pltpu.SMEM),
                  pl.BlockSpec((blk, n_heads * HEAD_DIM), q_index),
                  kv_spec(-1, k_col), kv_spec(0, k_col), kv_spec(1, k_col), ctx_spec(k_col),
                  kv_spec(-1, v_col), kv_spec(0, v_col), kv_spec(1, v_col), ctx_spec(v_col),
                  pl.BlockSpec((None,) + mask.shape[1:], mask_index)],
        out_specs=pl.BlockSpec((blk, n_heads * HEAD_DIM), q_index),
        compiler_params=_cparams(2),
        name="windowed_gqa",
    )(sink, *([qkv] * 9), mask)


NA_MAX_HEADS_PER_STEP = 16


def _na_heads_per_step(n_heads):
    return min(NA_MAX_HEADS_PER_STEP, n_heads)


def _na_kernel(q_ref, kvp_ref, kvc_ref, kvn_ref, kvx_ref, bias_ref, o_ref):
    n_local = bias_ref.shape[-1]
    gw = q_ref.shape[1]
    n = gw // HEAD_DIM
    kv_refs = (kvp_ref, kvc_ref, kvn_ref, kvx_ref)
    heads = [slice(h * HEAD_DIM, (h + 1) * HEAD_DIM) for h in range(n)]
    def qk(h):
        q = q_ref[:, heads[h]]
        return jnp.concatenate(
            [lax.dot_general(q, ref[:, heads[h]], (((1,), (1,)), ((), ())), preferred_element_type=F32)
             for ref in kv_refs], axis=1)

    def softmax(h, logit):
        s = logit * LOGIT_SCALE
        s = jnp.concatenate([s[:, :n_local] + bias_ref[h], s[:, n_local:]], axis=1)
        return jnp.exp2(s - jnp.max(s, axis=-1, keepdims=True)).astype(BF16)

    def pv(h, e):
        vs = slice(gw + heads[h].start, gw + heads[h].stop)
        o, c0 = None, 0
        for ref in kv_refs:
            v = ref[:, vs]
            part = jnp.dot(e[:, c0:c0 + v.shape[0]], jnp.concatenate([v, jnp.ones_like(v)], axis=1),
                           preferred_element_type=F32)
            o = part if o is None else o + part
            c0 += v.shape[0]
        o_ref[:, heads[h]] = (o[:, :HEAD_DIM] * (1.0 / o[:, HEAD_DIM:])).astype(o_ref.dtype)

    logits = {0: qk(0)}
    weights = {}
    for h in range(n):
        if h + 1 < n:
            logits[h + 1] = qk(h + 1)
        weights[h] = softmax(h, logits.pop(h))
        if h >= 1:
            pv(h - 1, weights.pop(h - 1))
    pv(n - 1, weights.pop(n - 1))


def _na_bias_table(rpb, n_grid_rows):
    r_in, w = NA_ROWS_PER_BLOCK, GRID_W
    kh = min(NA_KH, n_grid_rows)
    n_blocks = n_grid_rows // r_in
    n_ro, n_co = 2 * NA_KH - 1, 2 * NA_KW - 1
    qc, kc = np.arange(w)[:, None], np.arange(w)[None, :]
    e_col = (kc - qc + NA_KW - 1 == np.arange(n_co)[:, None, None]).astype(np.float32)
    win_start = np.clip(qc - NA_KW // 2, 0, w - NA_KW)
    col_ok = (kc >= win_start) & (kc < win_start + NA_KW)
    ri, krj = np.arange(r_in)[:, None], np.arange(3 * r_in)[None, :]
    row_ok = []
    for jb in (0, max(n_blocks // 2, 1) if n_blocks > 2 else 0, n_blocks - 1):
        r, kr = r_in * jb + ri, r_in * (jb - 1) + krj
        rs = np.clip(r - kh // 2, 0, n_grid_rows - kh)
        row_ok.append((kr >= rs) & (kr < rs + kh))
    ok = np.stack(row_ok)[:, :, None, :, None] & col_ok[None, None, :, None, :]
    ok = ok.reshape(3, r_in * w, 3 * r_in * w)
    tz = jnp.einsum('hab,bqc->haqc', rpb.astype(F32), e_col, precision=lax.Precision.HIGHEST) * LOG2E
    assert r_in - 1 <= NA_KH - 1 - r_in and 3 * r_in - 1 - r_in + NA_KH - 1 < n_ro
    full = jnp.concatenate(
        [jnp.concatenate([tz[:, k - i - r_in + NA_KH - 1] for k in range(3 * r_in)], axis=-1) for i in range(r_in)],
        axis=-2)
    return jnp.where(ok[None], full[:, None], NEG)


def _na(qkv, bias, *, n_batch, seq_len, ctx_len, n_heads):
    tq = NA_ROWS_PER_BLOCK * GRID_W
    n_blocks = seq_len // tq
    assert ctx_len == tq and seq_len % tq == 0 and NA_KH == 2 * NA_ROWS_PER_BLOCK and n_blocks >= 2
    hps = _na_heads_per_step(n_heads)
    assert n_heads % hps == 0
    n_groups = n_heads // hps
    gw = hps * HEAD_DIM
    ctx0 = n_batch * n_blocks

    def kv_spec(off):
        return pl.BlockSpec((tq, 2 * gw),
                            lambda b, g, jb: (b * n_blocks + jnp.clip(jb + off, 0, n_blocks - 1), g))

    def bias_index(b, g, jb):
        return (g, jnp.where(jb == 0, 0, jnp.where(jb == n_blocks - 1, 2, 1)), 0, 0)

    return pl.pallas_call(
        _na_kernel,
        out_shape=jax.ShapeDtypeStruct((n_batch * seq_len, n_heads * HEAD_DIM), BF16),
        grid=(n_batch, n_groups, n_blocks),
        in_specs=[pl.BlockSpec((tq, gw), lambda b, g, jb: (b * n_blocks + jb, 2 * n_groups + g)),
                  kv_spec(-1), kv_spec(0), kv_spec(1),
                  pl.BlockSpec((ctx_len, 2 * gw), lambda b, g, jb: (ctx0 + b, g)),
                  pl.BlockSpec((hps, None) + bias.shape[2:], bias_index)],
        out_specs=pl.BlockSpec((tq, gw), lambda b, g, jb: (b * n_blocks + jb, g)),
        compiler_params=_cparams(3),
        name="neighbourhood_attn",
    )(*([qkv] * 5), bias)


def _tiles():
    return dict(tm=1024, tm_proj=512, tm_out=512, tf=512, tn_mod=1024)


def _chunked(total, out_idx, out_col0=0, w_col0=0):
    return [(w_col0 + c, min(PROJ_CHUNK, total - c), out_idx, out_col0 + c, None)
            for c in range(0, total, PROJ_CHUNK)]


def _head_chunks(head_ops, w_col0, out_idx, out_col0):
    per = PROJ_CHUNK // HEAD_DIM
    return [(w_col0 + h0 * HEAD_DIM, len(head_ops[h0:h0 + per]) * HEAD_DIM, out_idx, out_col0 + h0 * HEAD_DIM,
             tuple(head_ops[h0:h0 + per])) for h0 in range(0, len(head_ops), per)]


def kernel(x, c, ctx, c_ctx, w_mod, b_mod, norm_g, ffn_w_in, ffn_w_out, ab_w_in, lru_conv_w, lru_conv_b,
           lru_w_a, lru_b_a, lru_w_x, lru_b_x, lru_lambda, attn_q_norm, attn_k_norm, attn_sink, ab_w_out,
           na_w_in, na_q_norm, na_k_norm, na_rpb, na_w_out):
    n_batch, seq_len, d = x.shape
    ctx_len = ctx.shape[1]
    depth = w_mod.shape[0]
    lru_w = lru_conv_w.shape[2]
    n_kv = attn_sink.shape[1] // GQA_GROUP
    n_att = n_kv * GQA_GROUP
    na_heads = na_rpb.shape[1]
    assert n_batch < MOD_ROWS and depth == 2
    t = _tiles()
    tl, tc = n_batch * seq_len, n_batch * ctx_len
    geom = dict(rows_per_batch=seq_len, n_batch=n_batch)

    cc = jnp.zeros((MOD_ROWS, d), F32).at[:n_batch].set(c).at[n_batch].set(c_ctx)
    mod = _mod_table(cc, w_mod, b_mod, t["tn_mod"])
    wg, wu, wo = _cast_ffn_weights(ffn_w_in, ffn_w_out)

    def ffn(x_src, n_rows, layer, which, row0=0):
        return _ffn(x_src, n_rows, row0, mod, layer, which, norm_g[layer, 2 * which], wg, wu, wo,
                    tm=t["tm"], tf=t["tf"], **geom)

    cos_t, sin_t = _rope_tables(seq_len)
    proj_geom = dict(tm=t["tm_proj"], n_lat_rows=tl, **geom)

    x_lat = ffn(x.reshape(tl, d), tl, 0, 0)
    x_ctx = ffn(ctx.reshape(tc, d), tc, 0, 0, row0=tl)
    gains = jnp.stack([attn_q_norm[0], attn_k_norm[0]])
    head_ops = [(0, True)] * n_att + [(1, True)] * n_kv + [None] * n_kv
    plan = _head_chunks(head_ops, 2 * lru_w, 1, 0) + _chunked(2 * lru_w, 0)
    p, qkv = _inproj(x_lat, x_ctx, mod, 0, norm_g[0, 1], ab_w_in[0].astype(BF16), gains, cos_t, sin_t, plan,
                     ((2 * lru_w, F32), (len(head_ops) * HEAD_DIM, BF16)), **proj_geom)
    lru_args = lambda dr: (lru_conv_w[0], lru_conv_b[0], lru_w_a[0, dr].astype(BF16), lru_b_a[0, dr],
                           lru_w_x[0, dr].astype(BF16), lru_b_x[0, dr], lru_lambda[0, dr])
    scan_geom = dict(n_batch=n_batch, seq_len=seq_len, ctx_len=ctx_len, tt=ctx_len)
    h_fwd = _lru_pass(p, *lru_args(0), None, reverse=False, **scan_geom)
    lru = _lru_pass(p, *lru_args(1), h_fwd, reverse=True, **scan_geom)
    att = _swa(qkv, attn_sink[0], n_batch=n_batch, seq_len=seq_len, ctx_len=ctx_len, n_kv_heads=n_kv,
               blk=WINDOW)
    x_all = _outproj(lru, att, 0, ab_w_out[0].astype(BF16), x_lat, x_ctx, tl + tc, mod, 0, tm=t["tm_out"], **geom)
    x_all = ffn(x_all, tl + tc, 0, 1)

    x_all = ffn(x_all, tl + tc, 1, 0)
    gains = jnp.stack([na_q_norm[0], na_k_norm[0]])
    na_d = na_heads * HEAD_DIM
    hps = _na_heads_per_step(na_heads)
    gw = hps * HEAD_DIM
    groups = range(0, na_d, gw)
    plan = sum([_head_chunks([(0, False)] * hps, g0, 0, 2 * na_d + g0) for g0 in groups], [])
    plan += sum([_head_chunks([(1, False)] * hps, na_d + g0, 0, 2 * g0) for g0 in groups], [])
    plan += sum([_head_chunks([None] * hps, 2 * na_d + g0, 0, 2 * g0 + gw) for g0 in groups], [])
    (qkv,) = _inproj(x_all, None, mod, 1, norm_g[1, 1], na_w_in[0].astype(BF16), gains, cos_t, sin_t, plan,
                     ((3 * na_d, BF16),), **proj_geom)
    o = _na(qkv, _na_bias_table(na_rpb[0], seq_len // GRID_W), n_batch=n_batch, seq_len=seq_len, ctx_len=ctx_len,
            n_heads=na_heads)
    x_lat = _outproj(o, o, 1, na_w_out[0].astype(BF16), x_all, None, tl, mod, 1, tm=t["tm_out"], **geom)
    x_lat = ffn(x_lat, tl, 1, 1)
    return x_lat.reshape(n_batch, seq_len, d)
```
